```python
import jax, jax.numpy as jnp
from jax import lax
import numpy as np

D_MODEL = 1024
BATCH = 8
SEQ = 4096
DEPTH = 1

ATTN_HEADS = 8
HEAD_DIM = 64
ATTN_WIDTH = ATTN_HEADS * HEAD_DIM
CONV_GROUPS = 8
CONV_WIDTH = D_MODEL // 2
CONV_K = 3
Q_BLOCK = 128
EPS = 1e-6
IN_SIZES = (ATTN_WIDTH, ATTN_WIDTH, ATTN_WIDTH, ATTN_HEADS, ATTN_WIDTH,
            CONV_WIDTH, CONV_WIDTH, CONV_WIDTH, CONV_WIDTH,
            D_MODEL, D_MODEL)
IN_WIDTH = sum(IN_SIZES)

kernel_name = "fox_shortconv_gated_hybrid"


def _rmsnorm(x, g):
    xf = x.astype(jnp.float32)
    xf = xf * lax.rsqrt(jnp.mean(xf * xf, axis=-1, keepdims=True) + EPS)
    return xf.astype(x.dtype) * g


def _fox_attention(q, k, v, log_f):
    b, s, h, dh = q.shape
    nb = s // Q_BLOCK
    cum = jnp.cumsum(log_f, axis=1).transpose(0, 2, 1)
    q_blocks = q.reshape(b, nb, Q_BLOCK, h, dh).swapaxes(0, 1)
    cum_q_blocks = cum.reshape(b, h, nb, Q_BLOCK).transpose(2, 0, 1, 3)
    k_pos = jnp.arange(s)
    scale = dh ** -0.5

    def one_block(args):
        qb, cq, blk = args
        logits = jnp.einsum('bqhd,bkhd->bhqk', qb, k).astype(jnp.float32) * scale
        logits = logits + cq[..., None] - cum[:, :, None, :]
        q_pos = blk * Q_BLOCK + jnp.arange(Q_BLOCK)
        causal = k_pos[None, :] <= q_pos[:, None]
        logits = jnp.where(causal, logits, -jnp.inf)
        p = jax.nn.softmax(logits, axis=-1).astype(v.dtype)
        return jnp.einsum('bhqk,bkhd->bqhd', p, v)

    out = lax.map(one_block, (q_blocks, cum_q_blocks, jnp.arange(nb)))
    return out.swapaxes(0, 1).reshape(b, s, h * dh)


def _causal_dwconv(v, w):
    return lax.conv_general_dilated(
        v, w[:, None, :].astype(v.dtype), window_strides=(1,),
        padding=[(CONV_K - 1, 0)], dimension_numbers=('NWC', 'WIO', 'NWC'),
        feature_group_count=v.shape[-1])


def _layer(x, c, w_ada, b_ada, norm_g, w_in, b_f, q_norm_g, k_norm_g,
           conv_w, w_attn_out, w_conv_out, w_o):
    b, s, _ = x.shape
    ada = c @ w_ada + b_ada
    shift, scale, gate = jnp.split(ada, 3, axis=-1)
    h = _rmsnorm(x, norm_g) * (1 + scale[:, None, :]) + shift[:, None, :]
    proj = h @ w_in
    split_points = np.cumsum(IN_SIZES)[:-1].tolist()
    q, k, v, f_logit, z_a, gb, gc, u, z_b, g_a, g_b = jnp.split(proj, split_points, axis=-1)

    q = _rmsnorm(q.reshape(b, s, ATTN_HEADS, HEAD_DIM), q_norm_g)
    k = _rmsnorm(k.reshape(b, s, ATTN_HEADS, HEAD_DIM), k_norm_g)
    v = v.reshape(b, s, ATTN_HEADS, HEAD_DIM)
    log_f = jax.nn.log_sigmoid((f_logit + b_f).astype(jnp.float32))
    o_a = _fox_attention(q, k, v, log_f) * jax.nn.silu(z_a)

    o_b = gb * _causal_dwconv(gc * u, conv_w) * jax.nn.silu(z_b)

    merged = jax.nn.sigmoid(g_a) * (o_a @ w_attn_out) + jax.nn.sigmoid(g_b) * (o_b @ w_conv_out)
    return x + gate[:, None, :] * (merged @ w_o)


def setup_inputs(seed: int = 0) -> dict:
    key = jax.random.key(seed)
    ks = jax.random.split(key, 14)
    f32 = jnp.float32
    nrm = lambda k, shape, s: jax.random.normal(k, shape, f32) * s
    x = jax.random.normal(ks[0], (BATCH, SEQ, D_MODEL), f32)
    c = jax.random.normal(ks[1], (BATCH, D_MODEL), f32)
    w_ada = nrm(ks[2], (DEPTH, D_MODEL, 3 * D_MODEL), 0.5 * D_MODEL ** -0.5)
    b_ada = nrm(ks[3], (DEPTH, 3 * D_MODEL), 0.02)
    norm_g = 1.0 + nrm(ks[4], (DEPTH, D_MODEL), 0.02)
    w_in = nrm(ks[5], (DEPTH, D_MODEL, IN_WIDTH), D_MODEL ** -0.5)
    b_f = 3.0 + nrm(ks[6], (DEPTH, ATTN_HEADS), 0.5)
    q_norm_g = 1.0 + nrm(ks[7], (DEPTH, HEAD_DIM), 0.02)
    k_norm_g = 1.0 + nrm(ks[8], (DEPTH, HEAD_DIM), 0.02)
    conv_w = nrm(ks[9], (DEPTH, CONV_K, CONV_WIDTH), CONV_K ** -0.5)
    w_attn_out = nrm(ks[10], (DEPTH, ATTN_WIDTH, D_MODEL), ATTN_WIDTH ** -0.5)
    w_conv_out = nrm(ks[11], (DEPTH, CONV_WIDTH, D_MODEL), CONV_WIDTH ** -0.5)
    w_o = nrm(ks[12], (DEPTH, D_MODEL, D_MODEL), D_MODEL ** -0.5)
    return {"x": x, "c": c, "w_ada": w_ada, "b_ada": b_ada, "norm_g": norm_g,
            "w_in": w_in, "b_f": b_f, "q_norm_g": q_norm_g, "k_norm_g": k_norm_g,
            "conv_w": conv_w, "w_attn_out": w_attn_out, "w_conv_out": w_conv_out,
            "w_o": w_o}


def reference(x, c, w_ada, b_ada, norm_g, w_in, b_f, q_norm_g, k_norm_g,
              conv_w, w_attn_out, w_conv_out, w_o):
    for i in range(DEPTH):
        x = _layer(x, c, w_ada[i], b_ada[i], norm_g[i], w_in[i], b_f[i],
                   q_norm_g[i], k_norm_g[i], conv_w[i], w_attn_out[i],
                   w_conv_out[i], w_o[i])
    return x
```

```python
import functools

import jax
import jax.numpy as jnp
from jax import lax
from jax.experimental import pallas as pl
from jax.experimental.pallas import tpu as pltpu

D_MODEL = 1024
HEADS = 8
HEAD_DIM = 64
ATTN_W = HEADS * HEAD_DIM
CONV_W = 512
CONV_K = 3
EPS = 1e-6

LANES = 128
TM = 512
TQ = 256
TK = 256
CUM_PARTS = 3
ONES_LANE = CUM_PARTS * HEADS
LO_LANE = 32
NEG_BIG = -1e30
VMEM_LIMIT = 56 * 1024 * 1024

F32 = jnp.float32
BF16 = jnp.bfloat16


def _log_sigmoid(x):
    return jnp.minimum(x, 0.0) - jnp.log(1.0 + jnp.exp(-jnp.abs(x)))


def _sigmoid(x):
    return 1.0 / (1.0 + jnp.exp(-x))


def _silu(x):
    return x * _sigmoid(x)


def _dot(a, b):
    return jnp.dot(a, b, preferred_element_type=F32)


def _dot_nt(a, b):
    return lax.dot_general(a, b, (((1,), (1,)), ((), ())), preferred_element_type=F32)


def _dot_tn(a, b):
    return lax.dot_general(a, b, (((0,), (0,)), ((), ())), preferred_element_type=F32)


def _ada_kernel(c_ref, w_ref, b_ref, o_ref):
    o_ref[...] = jnp.dot(c_ref[...], w_ref[...], preferred_element_type=F32,
                         precision=lax.Precision.HIGHEST) + b_ref[...]


def _ada(c, w_ada, b_ada):
    b, d = c.shape
    n = w_ada.shape[1]
    return pl.pallas_call(
        _ada_kernel,
        grid=(n // d,),
        in_specs=[pl.BlockSpec((b, d), lambda j: (0, 0)),
                  pl.BlockSpec((d, d), lambda j: (0, j)),
                  pl.BlockSpec((1, d), lambda j: (0, j))],
        out_specs=pl.BlockSpec((b, d), lambda j: (0, j)),
        out_shape=jax.ShapeDtypeStruct((b, n), F32),
        name="ada",
    )(c, w_ada, b_ada.reshape(1, n))


N_K = 0
N_F = N_K + ATTN_W
N_GB = N_F + LANES
N_GC = N_GB + CONV_W
N_U = N_GC + CONV_W
N_ZB = N_U + CONV_W
N_GA = N_ZB + CONV_W
N_GB2 = N_GA + D_MODEL
N_END = N_GB2 + D_MODEL
T_Q = 0
T_V = T_Q + ATTN_W
T_ZA = T_V + ATTN_W
T_F = T_ZA + ATTN_W
T_END = T_F + 2 * HEADS


def _proj_kernel(x_ref, ada_ref, ng_ref, wn_ref, wt_ref, bfn_ref, bft_ref, gq_ref, gk_ref,
                 cw_ref, wb_ref, pm_ref, lt_ref, ut_ref,
                 k_ref, kx_ref, qt_ref, cq_ref, vt_ref, za_ref, sga_ref, mb_ref,
                 cn_ref, ct_ref, cu_ref):
    s = pl.program_id(1)

    @pl.when(s == 0)
    def _():
        cn_ref[...] = jnp.zeros_like(cn_ref)
        ct_ref[...] = jnp.zeros_like(ct_ref)
        cu_ref[0:8, :] = jnp.zeros((8, CONV_W), F32)

    x = x_ref[0]
    shift = ada_ref[0, :, 0:D_MODEL]
    scale = ada_ref[0, :, D_MODEL:2 * D_MODEL]
    xn = x * lax.rsqrt(jnp.mean(x * x, axis=-1, keepdims=True) + EPS)
    h = (xn * ng_ref[...] * (1.0 + scale) + shift).astype(BF16)

    kraw = _dot(h, wn_ref[:, N_K:N_F])
    kms = _dot((kraw * kraw).astype(BF16), pm_ref[...])
    k_ref[0] = (kraw * lax.rsqrt(kms + EPS) * gk_ref[...]).astype(BF16)

    lane = lax.broadcasted_iota(jnp.int32, (TM, LANES), 1)
    lf = _log_sigmoid(_dot(h, wn_ref[:, N_F:N_GB]) + bfn_ref[...])
    lf_lo = lf - lf.astype(BF16).astype(F32)
    in_hi = lane < ONES_LANE
    in_lo = (lane >= LO_LANE) & (lane < LO_LANE + ONES_LANE)
    parts = jnp.where(in_hi, lf, jnp.where(in_lo, lf_lo, 0.0)).astype(BF16)
    csum = _dot(lt_ref[...], parts)
    cum = csum + pltpu.roll(csum, LANES - LO_LANE, 1) + cn_ref[...]
    cn_ref[...] = cum[TM - 1:TM, :]
    r1 = cum - cum.astype(BF16).astype(F32)
    r2 = r1 - r1.astype(BF16).astype(F32)
    piece = jnp.where(lane < HEADS, cum, jnp.where(lane < 2 * HEADS, r1, r2))
    is_one = (lane >= ONES_LANE) & (lane < ONES_LANE + CUM_PARTS)
    kx_ref[0] = jnp.where(in_hi, -piece, jnp.where(is_one, 1.0, 0.0)).astype(BF16)

    qraw = _dot_nt(wt_ref[T_Q:T_V, :], h)
    q3 = qraw.reshape(HEADS, HEAD_DIM, TM)
    qn = q3 * lax.rsqrt(jnp.mean(q3 * q3, axis=1, keepdims=True) + EPS)
    qt_ref[0] = (qn.reshape(ATTN_W, TM) * gq_ref[...]).astype(BF16)

    vt = _dot_nt(wt_ref[T_V:T_ZA, :], h).astype(BF16)
    for i in range(TM // TK):
        vt_ref[0, i] = vt[:, i * TK:(i + 1) * TK]

    za_ref[0] = _silu(_dot_nt(wt_ref[T_ZA:T_F, :], h)).astype(BF16)

    lft = _log_sigmoid(_dot_nt(wt_ref[T_F:T_END, :], h) + bft_ref[...])
    lft_lo = lft - lft.astype(BF16).astype(F32)
    row = lax.broadcasted_iota(jnp.int32, (2 * HEADS, TM), 0)
    ct = _dot(jnp.where(row < HEADS, lft, lft_lo).astype(BF16), ut_ref[...])
    ct = ct[0:HEADS, :] + ct[HEADS:2 * HEADS, :]
    carry = ct_ref[...]
    cq_ref[0] = ct[:, 0:TM] + jnp.concatenate([carry] * (TM // LANES), axis=1)
    ct_ref[...] = carry + ct[:, TM:TM + LANES]

    gb = _dot(h, wn_ref[:, N_GB:N_GC])
    gc = _dot(h, wn_ref[:, N_GC:N_U])
    u = _dot(h, wn_ref[:, N_U:N_ZB])
    zb = _dot(h, wn_ref[:, N_ZB:N_GA])
    cu = gc * u
    cu_ref[8:8 + TM, :] = cu
    conv = (cw_ref[2:3, :] * cu + cw_ref[1:2, :] * cu_ref[7:7 + TM, :]
            + cw_ref[0:1, :] * cu_ref[6:6 + TM, :])
    cu_ref[0:8, :] = cu[TM - 8:TM, :]
    ob = (gb * conv * _silu(zb)).astype(BF16)
    mbr = _dot(ob, wb_ref[...])
    mb_ref[0] = (_sigmoid(_dot(h, wn_ref[:, N_GB2:N_END])) * mbr).astype(BF16)
    sga_ref[0] = _sigmoid(_dot(h, wn_ref[:, N_GA:N_GB2])).astype(BF16)


def _proj(x, ada3, ng, wn, wt, bfn, bft, gq, gk, cw, wb, pm, lt, ut):
    b, s, d = x.shape
    const = lambda shape: pl.BlockSpec(shape, lambda bi, si: (0,) * len(shape))
    out_shape = [
        jax.ShapeDtypeStruct((b, s, ATTN_W), BF16),
        jax.ShapeDtypeStruct((b, s, LANES), BF16),
        jax.ShapeDtypeStruct((b, ATTN_W, s), BF16),
        jax.ShapeDtypeStruct((b, HEADS, s), F32),
        jax.ShapeDtypeStruct((b, s // TK, ATTN_W, TK), BF16),
        jax.ShapeDtypeStruct((b, ATTN_W, s), BF16),
        jax.ShapeDtypeStruct((b, s, d), BF16),
        jax.ShapeDtypeStruct((b, s, d), BF16),
    ]
    out_specs = [
        pl.BlockSpec((1, TM, ATTN_W), lambda bi, si: (bi, si, 0)),
        pl.BlockSpec((1, TM, LANES), lambda bi, si: (bi, si, 0)),
        pl.BlockSpec((1, ATTN_W, TM), lambda bi, si: (bi, 0, si)),
        pl.BlockSpec((1, HEADS, TM), lambda bi, si: (bi, 0, si)),
        pl.BlockSpec((1, TM // TK, ATTN_W, TK), lambda bi, si: (bi, si, 0, 0)),
        pl.BlockSpec((1, ATTN_W, TM), lambda bi, si: (bi, 0, si)),
        pl.BlockSpec((1, TM, d), lambda bi, si: (bi, si, 0)),
        pl.BlockSpec((1, TM, d), lambda bi, si: (bi, si, 0)),
    ]
    in_specs = [
        pl.BlockSpec((1, TM, d), lambda bi, si: (bi, si, 0)),
        pl.BlockSpec((1, 1, 3 * d), lambda bi, si: (bi, 0, 0)),
        const(ng.shape), const(wn.shape), const(wt.shape), const(bfn.shape), const(bft.shape),
        const(gq.shape), const(gk.shape), const(cw.shape), const(wb.shape), const(pm.shape),
        const(lt.shape), const(ut.shape),
    ]
    return pl.pallas_call(
        _proj_kernel,
        grid=(b, s // TM),
        in_specs=in_specs,
        out_specs=out_specs,
        out_shape=out_shape,
        scratch_shapes=[pltpu.VMEM((1, LANES), F32),
                        pltpu.VMEM((HEADS, LANES), F32),
                        pltpu.VMEM((TM + 8, CONV_W), F32)],
        compiler_params=pltpu.CompilerParams(
            dimension_semantics=("arbitrary", "arbitrary"),
            vmem_limit_bytes=VMEM_LIMIT),
        name="proj",
    )(x, ada3, ng, wn, wt, bfn, bft, gq, gk, cw, wb, pm, lt, ut)


def _attn_kernel(qt_ref, cq_ref, k_ref, kx_ref, vt_ref, za_ref, sga_ref, mb_ref, x_ref, gate_ref,
                 wa_ref, wo_ref, o_ref, rhs_ref, m_ref, l_ref, acc_ref):
    qi = pl.program_id(1)

    row = lax.broadcasted_iota(jnp.int32, (LANES, TQ), 0)
    for h in range(HEADS):
        pair, half = divmod(h, 2)
        qp = qt_ref[0, pair * LANES:(pair + 1) * LANES, :]
        mine = (row >= half * HEAD_DIM) & (row < (half + 1) * HEAD_DIM)
        rhs_ref[h, 0:LANES, :] = qp * jnp.where(mine, 1.0, 0.0).astype(BF16)
        cq = cq_ref[0, h:h + 1, :]
        r1 = cq - cq.astype(BF16).astype(F32)
        r2 = r1 - r1.astype(BF16).astype(F32)
        sel = (row == h) | (row == HEADS + h) | (row == 2 * HEADS + h)
        f = jnp.where(row == ONES_LANE, cq,
                      jnp.where(row == ONES_LANE + 1, r1,
                                jnp.where(row == ONES_LANE + 2, r2, 0.0)))
        rhs_ref[h, LANES:2 * LANES, :] = jnp.where(sel, 1.0, f).astype(BF16)

    m_ref[...] = jnp.full(m_ref.shape, NEG_BIG, F32)
    l_ref[...] = jnp.zeros_like(l_ref)
    acc_ref[...] = jnp.zeros_like(acc_ref)

    def block(j, masked):
        start = pl.multiple_of(j * TK, TK)
        kb = k_ref[0, pl.ds(start, TK), :]
        kxb = kx_ref[0, pl.ds(start, TK), :]
        for h in range(HEADS):
            pair = h // 2
            lhs = jnp.concatenate([kb[:, pair * LANES:(pair + 1) * LANES], kxb], axis=1)
            sc = _dot(lhs, rhs_ref[h])
            if masked:
                kpos = lax.broadcasted_iota(jnp.int32, (TK, TQ), 0)
                qpos = lax.broadcasted_iota(jnp.int32, (TK, TQ), 1)
                sc = jnp.where(kpos <= qpos, sc, NEG_BIG)
            m_prev = m_ref[h:h + 1, :]
            m_new = jnp.maximum(m_prev, jnp.max(sc, axis=0, keepdims=True))
            p = jnp.exp(sc - m_new)
            alpha = jnp.exp(m_prev - m_new)
            l_ref[h:h + 1, :] = alpha * l_ref[h:h + 1, :] + jnp.sum(p, axis=0, keepdims=True)
            m_ref[h:h + 1, :] = m_new
            rows = slice(h * HEAD_DIM, (h + 1) * HEAD_DIM)
            pv = _dot(vt_ref[0, j, rows, :], p.astype(BF16))
            acc_ref[rows, :] = alpha * acc_ref[rows, :] + pv

    def body(j, carry):
        block(j, False)
        return carry

    lax.fori_loop(0, qi, body, 0)
    block(qi, True)

    inv_l = 1.0 / l_ref[...]
    acc = jnp.concatenate(
        [acc_ref[h * HEAD_DIM:(h + 1) * HEAD_DIM, :] * inv_l[h:h + 1, :] for h in range(HEADS)],
        axis=0)
    at = (acc * za_ref[0].astype(F32)).astype(BF16)
    ya = _dot_tn(at, wa_ref[...])
    merged = (sga_ref[0].astype(F32) * ya + mb_ref[0].astype(F32)).astype(BF16)
    o_ref[0] = x_ref[0] + gate_ref[0] * _dot(merged, wo_ref[...])


def _attn(qt, cq, k, kx, vt, za, sga, mb, x, gate3, wa, wo):
    b, s, d = x.shape
    const = lambda shape: pl.BlockSpec(shape, lambda bi, qi: (0,) * len(shape))
    in_specs = [
        pl.BlockSpec((1, ATTN_W, TQ), lambda bi, qi: (bi, 0, qi)),
        pl.BlockSpec((1, HEADS, TQ), lambda bi, qi: (bi, 0, qi)),
        pl.BlockSpec((1, s, ATTN_W), lambda bi, qi: (bi, 0, 0)),
        pl.BlockSpec((1, s, LANES), lambda bi, qi: (bi, 0, 0)),
        pl.BlockSpec((1, s // TK, ATTN_W, TK), lambda bi, qi: (bi, 0, 0, 0)),
        pl.BlockSpec((1, ATTN_W, TQ), lambda bi, qi: (bi, 0, qi)),
        pl.BlockSpec((1, TQ, d), lambda bi, qi: (bi, qi, 0)),
        pl.BlockSpec((1, TQ, d), lambda bi, qi: (bi, qi, 0)),
        pl.BlockSpec((1, TQ, d), lambda bi, qi: (bi, qi, 0)),
        pl.BlockSpec((1, 1, d), lambda bi, qi: (bi, 0, 0)),
        const(wa.shape), const(wo.shape),
    ]
    return pl.pallas_call(
        _attn_kernel,
        grid=(b, s // TQ),
        in_specs=in_specs,
        out_specs=pl.BlockSpec((1, TQ, d), lambda bi, qi: (bi, qi, 0)),
        out_shape=jax.ShapeDtypeStruct((b, s, d), F32),
        scratch_shapes=[pltpu.VMEM((HEADS, 2 * LANES, TQ), BF16),
                        pltpu.VMEM((HEADS, TQ), F32),
                        pltpu.VMEM((HEADS, TQ), F32),
                        pltpu.VMEM((ATTN_W, TQ), F32)],
        compiler_params=pltpu.CompilerParams(
            dimension_semantics=("arbitrary", "arbitrary"),
            vmem_limit_bytes=VMEM_LIMIT),
        name="attn",
    )(qt, cq, k, kx, vt, za, sga, mb, x, gate3, wa, wo)


def _layer(x, c, w_ada, b_ada, norm_g, w_in, b_f, q_norm_g, k_norm_g, conv_w,
           w_attn_out, w_conv_out, w_o):
    b, s, d = x.shape
    ada = _ada(c, w_ada, b_ada)
    ada3 = ada.reshape(b, 1, 3 * d)
    gate3 = ada3[:, :, 2 * d:]

    sizes = (ATTN_W, ATTN_W, ATTN_W, HEADS, ATTN_W, CONV_W, CONV_W, CONV_W, CONV_W, d, d)
    offs = [0]
    for n in sizes:
        offs.append(offs[-1] + n)
    col = lambda i: w_in[:, offs[i]:offs[i + 1]]
    w_q, w_k, w_v, w_f, w_za, w_gb, w_gc, w_u, w_zb, w_ga, w_gb2 = (col(i) for i in range(11))

    rep = jnp.tile(w_f, (1, CUM_PARTS))
    fext = jnp.zeros((d, LANES), F32)
    fext = fext.at[:, 0:ONES_LANE].set(rep).at[:, LO_LANE:LO_LANE + ONES_LANE].set(rep)
    wn = jnp.concatenate([w_k, fext, w_gb, w_gc, w_u, w_zb, w_ga, w_gb2], axis=1).astype(BF16)
    wt = jnp.concatenate([w_q.T, w_v.T, w_za.T, w_f.T, w_f.T], axis=0).astype(BF16)

    bf_rep = jnp.tile(b_f, CUM_PARTS)
    bfn = jnp.zeros((1, LANES), F32)
    bfn = bfn.at[0, 0:ONES_LANE].set(bf_rep).at[0, LO_LANE:LO_LANE + ONES_LANE].set(bf_rep)
    bft = jnp.broadcast_to(jnp.concatenate([b_f, b_f])[:, None], (2 * HEADS, TM))
    gq = (jnp.tile(q_norm_g, HEADS) * HEAD_DIM ** -0.5).reshape(ATTN_W, 1)
    gk = jnp.tile(k_norm_g, HEADS).reshape(1, ATTN_W)

    head_of = jnp.arange(ATTN_W) // HEAD_DIM
    pm = jnp.where(head_of[:, None] == head_of[None, :], 1.0 / HEAD_DIM, 0.0).astype(BF16)
    tok = jnp.arange(TM)
    lt = (tok[None, :] <= tok[:, None]).astype(BF16)
    ut = jnp.concatenate([lt.T, jnp.ones((TM, LANES), BF16)], axis=1)

    k, kx, qt, cq, vt, za, sga, mb = _proj(
        x, ada3, norm_g.reshape(1, d), wn, wt, bfn, bft, gq, gk, conv_w,
        w_conv_out.astype(BF16), pm, lt, ut)
    return _attn(qt, cq, k, kx, vt, za, sga, mb, x, gate3,
                 w_attn_out.astype(BF16), w_o.astype(BF16))


@jax.jit
def kernel(x, c, w_ada, b_ada, norm_g, w_in, b_f, q_norm_g, k_norm_g, conv_w,
           w_attn_out, w_conv_out, w_o):
    for i in range(w_ada.shape[0]):
        x = _layer(x, c, w_ada[i], b_ada[i], norm_g[i], w_in[i], b_f[i], q_norm_g[i],
                   k_norm_g[i], conv_w[i], w_attn_out[i], w_conv_out[i], w_o[i])
    return x
```

```python
import functools

import jax
import jax.numpy as jnp
from jax import lax
from jax.experimental import pallas as pl
from jax.experimental.pallas import tpu as pltpu

D_MODEL = 1024
HEADS = 8
HEAD_DIM = 64
ATTN_W = HEADS * HEAD_DIM
CONV_W = 512
CONV_K = 3
EPS = 1e-6

LANES = 128
TM = 512
TQ = 256
TK = 256
CUM_PARTS = 3
ONES_LANE = CUM_PARTS * HEADS
LO_LANE = 32
V_ROWS = HEAD_DIM + 16
SCORE_LOOKAHEAD = 4
NEG_BIG = -1e30
MAX_RAW_LOGIT = 40.0
VMEM_LIMIT = 56 * 1024 * 1024

F32 = jnp.float32
BF16 = jnp.bfloat16


def _log_sigmoid(x):
    return jnp.minimum(x, 0.0) - jnp.log(1.0 + jnp.exp(-jnp.abs(x)))


def _sigmoid(x):
    return 1.0 / (1.0 + jnp.exp(-x))


def _silu(x):
    return x * _sigmoid(x)


def _dot(a, b):
    return jnp.dot(a, b, preferred_element_type=F32)


def _dot_nt(a, b):
    return lax.dot_general(a, b, (((1,), (1,)), ((), ())), preferred_element_type=F32)


def _dot_tn(a, b):
    return lax.dot_general(a, b, (((0,), (0,)), ((), ())), preferred_element_type=F32)


def _ada_kernel(c_ref, w_ref, b_ref, o_ref):
    o_ref[...] = jnp.dot(c_ref[...], w_ref[...], preferred_element_type=F32,
                         precision=lax.Precision.HIGHEST) + b_ref[...]


def _ada(c, w_ada, b_ada):
    b, d = c.shape
    n = w_ada.shape[1]
    return pl.pallas_call(
        _ada_kernel,
        grid=(n // d,),
        in_specs=[pl.BlockSpec((b, d), lambda j: (0, 0)),
                  pl.BlockSpec((d, d), lambda j: (0, j)),
                  pl.BlockSpec((1, d), lambda j: (0, j))],
        out_specs=pl.BlockSpec((b, d), lambda j: (0, j)),
        out_shape=jax.ShapeDtypeStruct((b, n), F32),
        name="ada",
    )(c, w_ada, b_ada.reshape(1, n))


N_K = 0
N_F = N_K + ATTN_W
N_GB = N_F + LANES
N_GC = N_GB + CONV_W
N_U = N_GC + CONV_W
N_ZB = N_U + CONV_W
N_GA = N_ZB + CONV_W
N_GB2 = N_GA + D_MODEL
N_END = N_GB2 + D_MODEL
T_Q = 0
T_V = T_Q + ATTN_W
T_ZA = T_V + ATTN_W
T_F = T_ZA + ATTN_W
T_END = T_F + 2 * HEADS


def _proj_kernel(x_ref, ada_ref, ng_ref, wn_ref, wt_ref, bfn_ref, bft_ref, gq_ref, gk_ref,
                 cw_ref, wb_ref, pm_ref, lt_ref, ut_ref,
                 k_ref, kx_ref, qt_ref, cq_ref, vt_ref, za_ref, sga_ref, mb_ref,
                 cn_ref, ct_ref, cu_ref):
    s = pl.program_id(1)

    @pl.when(s == 0)
    def _():
        cn_ref[...] = jnp.zeros_like(cn_ref)
        ct_ref[...] = jnp.zeros_like(ct_ref)
        cu_ref[0:8, :] = jnp.zeros((8, CONV_W), F32)

    x = x_ref[0]
    shift = ada_ref[0, :, 0:D_MODEL]
    scale = ada_ref[0, :, D_MODEL:2 * D_MODEL]
    xn = x * lax.rsqrt(jnp.mean(x * x, axis=-1, keepdims=True) + EPS)
    h = (xn * ng_ref[...] * (1.0 + scale) + shift).astype(BF16)

    kraw = _dot(h, wn_ref[:, N_K:N_F])
    kms = _dot((kraw * kraw).astype(BF16), pm_ref[...])
    k_ref[0] = (kraw * lax.rsqrt(kms + EPS) * gk_ref[...]).astype(BF16)

    lane = lax.broadcasted_iota(jnp.int32, (TM, LANES), 1)
    lf = _log_sigmoid(_dot(h, wn_ref[:, N_F:N_GB]) + bfn_ref[...])
    lf_lo = lf - lf.astype(BF16).astype(F32)
    in_hi = lane < ONES_LANE
    in_lo = (lane >= LO_LANE) & (lane < LO_LANE + ONES_LANE)
    parts = jnp.where(in_hi, lf, jnp.where(in_lo, lf_lo, 0.0)).astype(BF16)
    csum = _dot(lt_ref[...], parts)
    cum = csum + pltpu.roll(csum, LANES - LO_LANE, 1) + cn_ref[...]
    cn_ref[...] = cum[TM - 1:TM, :]
    r1 = cum - cum.astype(BF16).astype(F32)
    r2 = r1 - r1.astype(BF16).astype(F32)
    piece = jnp.where(lane < HEADS, cum, jnp.where(lane < 2 * HEADS, r1, r2))
    is_one = (lane >= ONES_LANE) & (lane < ONES_LANE + CUM_PARTS)
    kx_ref[0] = jnp.where(in_hi, -piece, jnp.where(is_one, 1.0, 0.0)).astype(BF16)

    qraw = _dot_nt(wt_ref[T_Q:T_V, :], h)
    q3 = qraw.reshape(HEADS, HEAD_DIM, TM)
    qn = q3 * lax.rsqrt(jnp.mean(q3 * q3, axis=1, keepdims=True) + EPS)
    qt_ref[0] = (qn.reshape(ATTN_W, TM) * gq_ref[...]).astype(BF16)

    vt = _dot_nt(wt_ref[T_V:T_ZA, :], h).astype(BF16)
    for i in range(TM // TK):
        for hd in range(HEADS):
            vt_ref[0, i, hd * V_ROWS:hd * V_ROWS + HEAD_DIM, :] = (
                vt[hd * HEAD_DIM:(hd + 1) * HEAD_DIM, i * TK:(i + 1) * TK])
            vt_ref[0, i, hd * V_ROWS + HEAD_DIM:(hd + 1) * V_ROWS, :] = (
                jnp.ones((V_ROWS - HEAD_DIM, TK), BF16))

    za_ref[0] = _silu(_dot_nt(wt_ref[T_ZA:T_F, :], h)).astype(BF16)

    lft = _log_sigmoid(_dot_nt(wt_ref[T_F:T_END, :], h) + bft_ref[...])
    lft_lo = lft - lft.astype(BF16).astype(F32)
    row = lax.broadcasted_iota(jnp.int32, (2 * HEADS, TM), 0)
    ct = _dot(jnp.where(row < HEADS, lft, lft_lo).astype(BF16), ut_ref[...])
    ct = ct[0:HEADS, :] + ct[HEADS:2 * HEADS, :]
    carry = ct_ref[...]
    cq_ref[0] = ct[:, 0:TM] + jnp.concatenate([carry] * (TM // LANES), axis=1)
    ct_ref[...] = carry + ct[:, TM:TM + LANES]

    gb = _dot(h, wn_ref[:, N_GB:N_GC])
    gc = _dot(h, wn_ref[:, N_GC:N_U])
    u = _dot(h, wn_ref[:, N_U:N_ZB])
    zb = _dot(h, wn_ref[:, N_ZB:N_GA])
    cu = gc * u
    cu_ref[8:8 + TM, :] = cu
    conv = (cw_ref[2:3, :] * cu + cw_ref[1:2, :] * cu_ref[7:7 + TM, :]
            + cw_ref[0:1, :] * cu_ref[6:6 + TM, :])
    cu_ref[0:8, :] = cu[TM - 8:TM, :]
    ob = (gb * conv * _silu(zb)).astype(BF16)
    mbr = _dot(ob, wb_ref[...])
    mb_ref[0] = (_sigmoid(_dot(h, wn_ref[:, N_GB2:N_END])) * mbr).astype(BF16)
    sga_ref[0] = _sigmoid(_dot(h, wn_ref[:, N_GA:N_GB2])).astype(BF16)


def _proj(x, ada3, ng, wn, wt, bfn, bft, gq, gk, cw, wb, pm, lt, ut):
    b, s, d = x.shape
    const = lambda shape: pl.BlockSpec(shape, lambda bi, si: (0,) * len(shape))
    out_shape = [
        jax.ShapeDtypeStruct((b, s, ATTN_W), BF16),
        jax.ShapeDtypeStruct((b, s, LANES), BF16),
        jax.ShapeDtypeStruct((b, ATTN_W, s), BF16),
        jax.ShapeDtypeStruct((b, HEADS, s), F32),
        jax.ShapeDtypeStruct((b, s // TK, HEADS * V_ROWS, TK), BF16),
        jax.ShapeDtypeStruct((b, ATTN_W, s), BF16),
        jax.ShapeDtypeStruct((b, s, d), BF16),
        jax.ShapeDtypeStruct((b, s, d), BF16),
    ]
    out_specs = [
        pl.BlockSpec((1, TM, ATTN_W), lambda bi, si: (bi, si, 0)),
        pl.BlockSpec((1, TM, LANES), lambda bi, si: (bi, si, 0)),
        pl.BlockSpec((1, ATTN_W, TM), lambda bi, si: (bi, 0, si)),
        pl.BlockSpec((1, HEADS, TM), lambda bi, si: (bi, 0, si)),
        pl.BlockSpec((1, TM // TK, HEADS * V_ROWS, TK), lambda bi, si: (bi, si, 0, 0)),
        pl.BlockSpec((1, ATTN_W, TM), lambda bi, si: (bi, 0, si)),
        pl.BlockSpec((1, TM, d), lambda bi, si: (bi, si, 0)),
        pl.BlockSpec((1, TM, d), lambda bi, si: (bi, si, 0)),
    ]
    in_specs = [
        pl.BlockSpec((1, TM, d), lambda bi, si: (bi, si, 0)),
        pl.BlockSpec((1, 1, 3 * d), lambda bi, si: (bi, 0, 0)),
        const(ng.shape), const(wn.shape), const(wt.shape), const(bfn.shape), const(bft.shape),
        const(gq.shape), const(gk.shape), const(cw.shape), const(wb.shape), const(pm.shape),
        const(lt.shape), const(ut.shape),
    ]
    return pl.pallas_call(
        _proj_kernel,
        grid=(b, s // TM),
        in_specs=in_specs,
        out_specs=out_specs,
        out_shape=out_shape,
        scratch_shapes=[pltpu.VMEM((1, LANES), F32),
                        pltpu.VMEM((HEADS, LANES), F32),
                        pltpu.VMEM((TM + 8, CONV_W), F32)],
        compiler_params=pltpu.CompilerParams(
            dimension_semantics=("arbitrary", "arbitrary"),
            vmem_limit_bytes=VMEM_LIMIT),
        name="proj",
    )(x, ada3, ng, wn, wt, bfn, bft, gq, gk, cw, wb, pm, lt, ut)


def _attn_kernel(qt_ref, cq_ref, k_ref, kx_ref, vt_ref, za_ref, sga_ref, mb_ref, x_ref, gate_ref,
                 wa_ref, wo_ref, o_ref, rhs_ref, m_ref, acc_ref, *, running_max):
    qi = pl.program_id(1)

    row = lax.broadcasted_iota(jnp.int32, (LANES, TQ), 0)
    for h in range(HEADS):
        pair, half = divmod(h, 2)
        qp = qt_ref[0, pair * LANES:(pair + 1) * LANES, :]
        mine = (row >= half * HEAD_DIM) & (row < (half + 1) * HEAD_DIM)
        rhs_ref[h, 0:LANES, :] = qp * jnp.where(mine, 1.0, 0.0).astype(BF16)
        cq = cq_ref[0, h:h + 1, :]
        r1 = cq - cq.astype(BF16).astype(F32)
        r2 = r1 - r1.astype(BF16).astype(F32)
        sel = (row == h) | (row == HEADS + h) | (row == 2 * HEADS + h)
        f = jnp.where(row == ONES_LANE, cq,
                      jnp.where(row == ONES_LANE + 1, r1,
                                jnp.where(row == ONES_LANE + 2, r2, 0.0)))
        rhs_ref[h, LANES:2 * LANES, :] = jnp.where(sel, 1.0, f).astype(BF16)

    if running_max:
        m_ref[...] = jnp.full(m_ref.shape, NEG_BIG, F32)
    acc_ref[...] = jnp.zeros_like(acc_ref)

    def block(j, masked):
        start = pl.multiple_of(j * TK, TK)
        kb = k_ref[0, pl.ds(start, TK), :]
        kxb = kx_ref[0, pl.ds(start, TK), :]
        def scores(h):
            pair = h // 2
            lhs = jnp.concatenate([kb[:, pair * LANES:(pair + 1) * LANES], kxb], axis=1)
            return _dot(lhs, rhs_ref[h])

        ahead = [scores(h) for h in range(SCORE_LOOKAHEAD)]
        for h in range(HEADS):
            sc = ahead.pop(0)
            if h + SCORE_LOOKAHEAD < HEADS:
                ahead.append(scores(h + SCORE_LOOKAHEAD))
            if masked:
                kpos = lax.broadcasted_iota(jnp.int32, (TK, TQ), 0)
                qpos = lax.broadcasted_iota(jnp.int32, (TK, TQ), 1)
                sc = jnp.where(kpos <= qpos, sc, NEG_BIG)
            rows = slice(h * V_ROWS, (h + 1) * V_ROWS)
            if running_max:
                m_prev = m_ref[h:h + 1, :]
                m_new = jnp.maximum(m_prev, jnp.max(sc, axis=0, keepdims=True))
                m_ref[h:h + 1, :] = m_new
                p = jnp.exp(sc - m_new).astype(BF16)
                acc_ref[rows, :] = (jnp.exp(m_prev - m_new) * acc_ref[rows, :]
                                    + _dot(vt_ref[0, j, rows, :], p))
            else:
                p = jnp.exp(sc).astype(BF16)
                acc_ref[rows, :] += _dot(vt_ref[0, j, rows, :], p)

    def body(j, carry):
        block(j, False)
        return carry

    lax.fori_loop(0, qi, body, 0)
    block(qi, True)

    acc = jnp.concatenate(
        [acc_ref[h * V_ROWS:h * V_ROWS + HEAD_DIM, :]
         * (1.0 / acc_ref[h * V_ROWS + HEAD_DIM:h * V_ROWS + HEAD_DIM + 1, :])
         for h in range(HEADS)], axis=0)
    at = (acc * za_ref[0].astype(F32)).astype(BF16)
    ya = _dot_tn(at, wa_ref[...])
    merged = (sga_ref[0].astype(F32) * ya + mb_ref[0].astype(F32)).astype(BF16)
    o_ref[0] = x_ref[0] + gate_ref[0] * _dot(merged, wo_ref[...])


def _attn(running_max, qt, cq, k, kx, vt, za, sga, mb, x, gate3, wa, wo):
    b, s, d = x.shape
    const = lambda shape: pl.BlockSpec(shape, lambda bi, qi: (0,) * len(shape))
    in_specs = [
        pl.BlockSpec((1, ATTN_W, TQ), lambda bi, qi: (bi, 0, qi)),
        pl.BlockSpec((1, HEADS, TQ), lambda bi, qi: (bi, 0, qi)),
        pl.BlockSpec((1, s, ATTN_W), lambda bi, qi: (bi, 0, 0)),
        pl.BlockSpec((1, s, LANES), lambda bi, qi: (bi, 0, 0)),
        pl.BlockSpec((1, s // TK, HEADS * V_ROWS, TK), lambda bi, qi: (bi, 0, 0, 0)),
        pl.BlockSpec((1, ATTN_W, TQ), lambda bi, qi: (bi, 0, qi)),
        pl.BlockSpec((1, TQ, d), lambda bi, qi: (bi, qi, 0)),
        pl.BlockSpec((1, TQ, d), lambda bi, qi: (bi, qi, 0)),
        pl.BlockSpec((1, TQ, d), lambda bi, qi: (bi, qi, 0)),
        pl.BlockSpec((1, 1, d), lambda bi, qi: (bi, 0, 0)),
        const(wa.shape), const(wo.shape),
    ]
    return pl.pallas_call(
        functools.partial(_attn_kernel, running_max=running_max),
        grid=(b, s // TQ),
        in_specs=in_specs,
        out_specs=pl.BlockSpec((1, TQ, d), lambda bi, qi: (bi, qi, 0)),
        out_shape=jax.ShapeDtypeStruct((b, s, d), F32),
        scratch_shapes=[pltpu.VMEM((HEADS, 2 * LANES, TQ), BF16),
                        pltpu.VMEM((HEADS, TQ), F32),
                        pltpu.VMEM((HEADS * V_ROWS, TQ), F32)],
        compiler_params=pltpu.CompilerParams(
            dimension_semantics=("arbitrary", "arbitrary"),
            vmem_limit_bytes=VMEM_LIMIT),
        name="attn_running_max" if running_max else "attn",
    )(qt, cq, k, kx, vt, za, sga, mb, x, gate3, wa, wo)


def _layer(x, c, w_ada, b_ada, norm_g, w_in, b_f, q_norm_g, k_norm_g, conv_w,
           w_attn_out, w_conv_out, w_o):
    b, s, d = x.shape
    ada = _ada(c, w_ada, b_ada)
    ada3 = ada.reshape(b, 1, 3 * d)
    gate3 = ada3[:, :, 2 * d:]

    sizes = (ATTN_W, ATTN_W, ATTN_W, HEADS, ATTN_W, CONV_W, CONV_W, CONV_W, CONV_W, d, d)
    offs = [0]
    for n in sizes:
        offs.append(offs[-1] + n)
    col = lambda i: w_in[:, offs[i]:offs[i + 1]]
    w_q, w_k, w_v, w_f, w_za, w_gb, w_gc, w_u, w_zb, w_ga, w_gb2 = (col(i) for i in range(11))

    rep = jnp.tile(w_f, (1, CUM_PARTS))
    fext = jnp.zeros((d, LANES), F32)
    fext = fext.at[:, 0:ONES_LANE].set(rep).at[:, LO_LANE:LO_LANE + ONES_LANE].set(rep)
    wn = jnp.concatenate([w_k, fext, w_gb, w_gc, w_u, w_zb, w_ga, w_gb2], axis=1).astype(BF16)
    wt = jnp.concatenate([w_q.T, w_v.T, w_za.T, w_f.T, w_f.T], axis=0).astype(BF16)

    bf_rep = jnp.tile(b_f, CUM_PARTS)
    bfn = jnp.zeros((1, LANES), F32)
    bfn = bfn.at[0, 0:ONES_LANE].set(bf_rep).at[0, LO_LANE:LO_LANE + ONES_LANE].set(bf_rep)
    bft = jnp.broadcast_to(jnp.concatenate([b_f, b_f])[:, None], (2 * HEADS, TM))
    gq = (jnp.tile(q_norm_g, HEADS) * HEAD_DIM ** -0.5).reshape(ATTN_W, 1)
    gk = jnp.tile(k_norm_g, HEADS).reshape(1, ATTN_W)

    head_of = jnp.arange(ATTN_W) // HEAD_DIM
    pm = jnp.where(head_of[:, None] == head_of[None, :], 1.0 / HEAD_DIM, 0.0).astype(BF16)
    tok = jnp.arange(TM)
    lt = (tok[None, :] <= tok[:, None]).astype(BF16)
    ut = jnp.concatenate([lt.T, jnp.ones((TM, LANES), BF16)], axis=1)

    k, kx, qt, cq, vt, za, sga, mb = _proj(
        x, ada3, norm_g.reshape(1, d), wn, wt, bfn, bft, gq, gk, conv_w,
        w_conv_out.astype(BF16), pm, lt, ut)
    logit_bound = HEAD_DIM ** 0.5 * jnp.max(jnp.abs(q_norm_g)) * jnp.max(jnp.abs(k_norm_g))
    args = (qt, cq, k, kx, vt, za, sga, mb, x, gate3, w_attn_out.astype(BF16), w_o.astype(BF16))
    return lax.cond(logit_bound <= MAX_RAW_LOGIT,
                    functools.partial(_attn, False), functools.partial(_attn, True), *args)


@jax.jit
def kernel(x, c, w_ada, b_ada, norm_g, w_in, b_f, q_norm_g, k_norm_g, conv_w,
           w_attn_out, w_conv_out, w_o):
    for i in range(w_ada.shape[0]):
        x = _layer(x, c, w_ada[i], b_ada[i], norm_g[i], w_in[i], b_f[i], q_norm_g[i],
                   k_norm_g[i], conv_w[i], w_attn_out[i], w_conv_out[i], w_o[i])
    return x
```

```python
import functools

import jax
import jax.numpy as jnp
from jax import lax
from jax.experimental import pallas as pl
from jax.experimental.pallas import tpu as pltpu

D_MODEL = 1024
HEADS = 8
HEAD_DIM = 64
ATTN_W = HEADS * HEAD_DIM
CONV_W = 512
CONV_K = 3
EPS = 1e-6

LANES = 128
TM = 512
TQ = 512
TK = 256
CUM_PARTS = 3
ONES_LANE = CUM_PARTS * HEADS
LO_LANE = 32
V_ROWS = HEAD_DIM + 16
SCORE_LOOKAHEAD = 4
NEG_BIG = -1e30
MAX_RAW_LOGIT = 40.0
VMEM_LIMIT = 56 * 1024 * 1024

F32 = jnp.float32
BF16 = jnp.bfloat16


def _log_sigmoid(x):
    return jnp.minimum(x, 0.0) - jnp.log(1.0 + jnp.exp(-jnp.abs(x)))


def _sigmoid(x):
    return 0.5 * jnp.tanh(0.5 * x) + 0.5


def _silu(x):
    hx = 0.5 * x
    return hx * jnp.tanh(hx) + hx


def _dot(a, b):
    return jnp.dot(a, b, preferred_element_type=F32)


def _dot_nt(a, b):
    return lax.dot_general(a, b, (((1,), (1,)), ((), ())), preferred_element_type=F32)


def _dot_tn(a, b):
    return lax.dot_general(a, b, (((0,), (0,)), ((), ())), preferred_element_type=F32)


def _ada_kernel(c_ref, w_ref, b_ref, o_ref):
    o_ref[...] = jnp.dot(c_ref[...], w_ref[...], preferred_element_type=F32,
                         precision=lax.Precision.HIGHEST) + b_ref[...]


def _ada(c, w_ada, b_ada):
    b, d = c.shape
    n = w_ada.shape[1]
    return pl.pallas_call(
        _ada_kernel,
        grid=(n // d,),
        in_specs=[pl.BlockSpec((b, d), lambda j: (0, 0)),
                  pl.BlockSpec((d, d), lambda j: (0, j)),
                  pl.BlockSpec((1, d), lambda j: (0, j))],
        out_specs=pl.BlockSpec((b, d), lambda j: (0, j)),
        out_shape=jax.ShapeDtypeStruct((b, n), F32),
        name="ada",
    )(c, w_ada, b_ada.reshape(1, n))


N_K = 0
N_F = N_K + ATTN_W
N_GB = N_F + LANES
N_GC = N_GB + CONV_W
N_U = N_GC + CONV_W
N_ZB = N_U + CONV_W
N_GA = N_ZB + CONV_W
N_GB2 = N_GA + D_MODEL
N_END = N_GB2 + D_MODEL
T_Q = 0
T_V = T_Q + ATTN_W
T_ZA = T_V + ATTN_W
T_F = T_ZA + ATTN_W
T_END = T_F + 2 * HEADS


def _proj_kernel(x_ref, xnext_ref, ada_ref, ng_ref, wn_ref, wt_ref, bfn_ref, bft_ref, gq_ref, gk_ref,
                 cw_ref, wb_ref, pm_ref, lt_ref, ut_ref,
                 k_ref, kx_ref, qt_ref, cq_ref, vt_ref, za_ref, sga_ref, mb_ref,
                 cn_ref, ct_ref, cu_ref, h_ref):
    s = pl.program_id(1)
    slot = lax.rem(s, 2)

    def modulated_norm(xr):
        x = xr[0]
        shift = ada_ref[0, :, 0:D_MODEL]
        scale = ada_ref[0, :, D_MODEL:2 * D_MODEL]
        xn = x * lax.rsqrt(jnp.mean(x * x, axis=-1, keepdims=True) + EPS)
        return (xn * ng_ref[...] * (1.0 + scale) + shift).astype(BF16)

    @pl.when(s == 0)
    def _():
        cn_ref[...] = jnp.zeros_like(cn_ref)
        ct_ref[...] = jnp.zeros_like(ct_ref)
        cu_ref[0:8, :] = jnp.zeros((8, CONV_W), F32)
        h_ref[0] = modulated_norm(x_ref)

    nat = lambda lo, hi: _dot(h_ref[slot], wn_ref[:, lo:hi])
    trn = lambda lo, hi: _dot_nt(wt_ref[lo:hi, :], h_ref[slot])
    lane = lax.broadcasted_iota(jnp.int32, (TM, LANES), 1)
    in_hi = lane < ONES_LANE
    in_lo = (lane >= LO_LANE) & (lane < LO_LANE + ONES_LANE)

    gc = nat(N_GC, N_U)
    u = nat(N_U, N_ZB)
    kraw = nat(N_K, N_F)
    flog = nat(N_F, N_GB)

    cu = gc * u
    cu_ref[8:8 + TM, :] = cu
    conv = (cw_ref[2:3, :] * cu + cw_ref[1:2, :] * cu_ref[7:7 + TM, :]
            + cw_ref[0:1, :] * cu_ref[6:6 + TM, :])
    cu_ref[0:8, :] = cu[TM - 8:TM, :]
    gb = nat(N_GB, N_GC)
    zb = nat(N_ZB, N_GA)

    kk = (kraw * kraw).astype(BF16)
    lf = _log_sigmoid(flog + bfn_ref[...])
    lf_lo = lf - lf.astype(BF16).astype(F32)
    parts = jnp.where(in_hi, lf, jnp.where(in_lo, lf_lo, 0.0)).astype(BF16)
    kms = _dot(kk, pm_ref[...])
    csum = _dot(lt_ref[...], parts)
    ga = nat(N_GA, N_GB2)

    gb2 = nat(N_GB2, N_END)
    ob = (gb * conv * _silu(zb)).astype(BF16)
    mbr = _dot(ob, wb_ref[...])

    sga_ref[0] = _sigmoid(ga).astype(BF16)
    k_ref[0] = (kraw * lax.rsqrt(kms + EPS) * gk_ref[...]).astype(BF16)
    cum = csum + pltpu.roll(csum, LANES - LO_LANE, 1) + cn_ref[...]
    cn_ref[...] = cum[TM - 1:TM, :]
    r1 = cum - cum.astype(BF16).astype(F32)
    r2 = r1 - r1.astype(BF16).astype(F32)
    piece = jnp.where(lane < HEADS, cum, jnp.where(lane < 2 * HEADS, r1, r2))
    is_one = (lane >= ONES_LANE) & (lane < ONES_LANE + CUM_PARTS)
    kx_ref[0] = jnp.where(in_hi, -piece, jnp.where(is_one, 1.0, 0.0)).astype(BF16)
    zf = trn(T_ZA, T_END)
    qraw = trn(T_Q, T_V)

    mb_ref[0] = (_sigmoid(gb2) * mbr).astype(BF16)
    lft = _log_sigmoid(zf[ATTN_W:, :] + bft_ref[...])
    lft_lo = lft - lft.astype(BF16).astype(F32)
    row = lax.broadcasted_iota(jnp.int32, (2 * HEADS, TM), 0)
    ct = _dot(jnp.where(row < HEADS, lft, lft_lo).astype(BF16), ut_ref[...])

    za_ref[0] = _silu(zf[0:ATTN_W, :]).astype(BF16)
    ct = ct[0:HEADS, :] + ct[HEADS:2 * HEADS, :]
    carry = ct_ref[...]
    cq_ref[0] = ct[:, 0:TM] + jnp.concatenate([carry] * (TM // LANES), axis=1)
    ct_ref[...] = carry + ct[:, TM:TM + LANES]
    vraw = trn(T_V, T_ZA)

    q3 = qraw.reshape(HEADS, HEAD_DIM, TM)
    qn = q3 * lax.rsqrt(jnp.mean(q3 * q3, axis=1, keepdims=True) + EPS)
    qt_ref[0] = (qn.reshape(ATTN_W, TM) * gq_ref[...]).astype(BF16)

    vt = vraw.astype(BF16)
    for i in range(TM // TK):
        for hd in range(HEADS):
            vt_ref[0, i, hd * V_ROWS:hd * V_ROWS + HEAD_DIM, :] = (
                vt[hd * HEAD_DIM:(hd + 1) * HEAD_DIM, i * TK:(i + 1) * TK])
            vt_ref[0, i, hd * V_ROWS + HEAD_DIM:(hd + 1) * V_ROWS, :] = (
                jnp.ones((V_ROWS - HEAD_DIM, TK), BF16))

    h_ref[1 - slot] = modulated_norm(xnext_ref)


def _proj(x, ada3, ng, wn, wt, bfn, bft, gq, gk, cw, wb, pm, lt, ut):
    b, s, d = x.shape
    const = lambda shape: pl.BlockSpec(shape, lambda bi, si: (0,) * len(shape))
    out_shape = [
        jax.ShapeDtypeStruct((b, s, ATTN_W), BF16),
        jax.ShapeDtypeStruct((b, s, LANES), BF16),
        jax.ShapeDtypeStruct((b, ATTN_W, s), BF16),
        jax.ShapeDtypeStruct((b, HEADS, s), F32),
        jax.ShapeDtypeStruct((b, s // TK, HEADS * V_ROWS, TK), BF16),
        jax.ShapeDtypeStruct((b, ATTN_W, s), BF16),
        jax.ShapeDtypeStruct((b, s, d), BF16),
        jax.ShapeDtypeStruct((b, s, d), BF16),
    ]
    out_specs = [
        pl.BlockSpec((1, TM, ATTN_W), lambda bi, si: (bi, si, 0)),
        pl.BlockSpec((1, TM, LANES), lambda bi, si: (bi, si, 0)),
        pl.BlockSpec((1, ATTN_W, TM), lambda bi, si: (bi, 0, si)),
        pl.BlockSpec((1, HEADS, TM), lambda bi, si: (bi, 0, si)),
        pl.BlockSpec((1, TM // TK, HEADS * V_ROWS, TK), lambda bi, si: (bi, si, 0, 0)),
        pl.BlockSpec((1, ATTN_W, TM), lambda bi, si: (bi, 0, si)),
        pl.BlockSpec((1, TM, d), lambda bi, si: (bi, si, 0)),
        pl.BlockSpec((1, TM, d), lambda bi, si: (bi, si, 0)),
    ]
    last = s // TM - 1
    in_specs = [
        pl.BlockSpec((1, TM, d), lambda bi, si: (bi, si, 0)),
        pl.BlockSpec((1, TM, d), lambda bi, si: (bi, jnp.minimum(si + 1, last), 0)),
        pl.BlockSpec((1, 1, 3 * d), lambda bi, si: (bi, 0, 0)),
        const(ng.shape), const(wn.shape), const(wt.shape), const(bfn.shape), const(bft.shape),
        const(gq.shape), const(gk.shape), const(cw.shape), const(wb.shape), const(pm.shape),
        const(lt.shape), const(ut.shape),
    ]
    return pl.pallas_call(
        _proj_kernel,
        grid=(b, s // TM),
        in_specs=in_specs,
        out_specs=out_specs,
        out_shape=out_shape,
        scratch_shapes=[pltpu.VMEM((1, LANES), F32),
                        pltpu.VMEM((HEADS, LANES), F32),
                        pltpu.VMEM((TM + 8, CONV_W), F32),
                        pltpu.VMEM((2, TM, D_MODEL), BF16)],
        compiler_params=pltpu.CompilerParams(
            dimension_semantics=("arbitrary", "arbitrary"),
            vmem_limit_bytes=VMEM_LIMIT),
        name="proj",
    )(x, x, ada3, ng, wn, wt, bfn, bft, gq, gk, cw, wb, pm, lt, ut)


def _attn_kernel(qt_ref, cq_ref, k_ref, kx_ref, vt_ref, za_ref, sga_ref, mb_ref, x_ref, gate_ref,
                 wa_ref, wo_ref, o_ref, rhs_ref, m_ref, acc_ref, *, running_max):
    qi = pl.program_id(1)

    row = lax.broadcasted_iota(jnp.int32, (LANES, TQ), 0)
    for h in range(HEADS):
        pair, half = divmod(h, 2)
        qp = qt_ref[0, pair * LANES:(pair + 1) * LANES, :]
        mine = (row >= half * HEAD_DIM) & (row < (half + 1) * HEAD_DIM)
        rhs_ref[h, 0:LANES, :] = qp * jnp.where(mine, 1.0, 0.0).astype(BF16)
        cq = cq_ref[0, h:h + 1, :]
        r1 = cq - cq.astype(BF16).astype(F32)
        r2 = r1 - r1.astype(BF16).astype(F32)
        sel = (row == h) | (row == HEADS + h) | (row == 2 * HEADS + h)
        f = jnp.where(row == ONES_LANE, cq,
                      jnp.where(row == ONES_LANE + 1, r1,
                                jnp.where(row == ONES_LANE + 2, r2, 0.0)))
        rhs_ref[h, LANES:2 * LANES, :] = jnp.where(sel, 1.0, f).astype(BF16)

    if running_max:
        m_ref[...] = jnp.full(m_ref.shape, NEG_BIG, F32)
    acc_ref[...] = jnp.zeros_like(acc_ref)

    def run_blocks(blocks):
        units = [(blk, h) for blk in blocks for h in range(HEADS)]

        def scores(unit):
            (j, _, q_lo), h = unit
            start = pl.multiple_of(j * TK, TK)
            pair = h // 2
            lhs = jnp.concatenate(
                [k_ref[0, pl.ds(start, TK), pair * LANES:(pair + 1) * LANES],
                 kx_ref[0, pl.ds(start, TK), :]], axis=1)
            return _dot(lhs, rhs_ref[h, :, q_lo:TQ])

        ahead = [scores(u) for u in units[:SCORE_LOOKAHEAD]]
        for i, ((j, masked, q_lo), h) in enumerate(units):
            sc = ahead.pop(0)
            if i + SCORE_LOOKAHEAD < len(units):
                ahead.append(scores(units[i + SCORE_LOOKAHEAD]))
            if masked:
                kpos = lax.broadcasted_iota(jnp.int32, sc.shape, 0)
                qpos = lax.broadcasted_iota(jnp.int32, sc.shape, 1)
                sc = jnp.where(kpos <= qpos, sc, NEG_BIG)
            rows = slice(h * V_ROWS, (h + 1) * V_ROWS)
            if running_max:
                m_prev = m_ref[h:h + 1, q_lo:TQ]
                m_new = jnp.maximum(m_prev, jnp.max(sc, axis=0, keepdims=True))
                m_ref[h:h + 1, q_lo:TQ] = m_new
                p = jnp.exp(sc - m_new).astype(BF16)
                acc_ref[rows, q_lo:TQ] = (jnp.exp(m_prev - m_new) * acc_ref[rows, q_lo:TQ]
                                          + _dot(vt_ref[0, j, rows, :], p))
            else:
                p = jnp.exp(sc).astype(BF16)
                acc_ref[rows, q_lo:TQ] += _dot(vt_ref[0, j, rows, :], p)

    blocks_per_tile = TQ // TK

    def body(i, carry):
        run_blocks([(i * blocks_per_tile + t, False, 0) for t in range(blocks_per_tile)])
        return carry

    lax.fori_loop(0, qi, body, 0)
    run_blocks([(qi * blocks_per_tile + t, True, t * TK) for t in range(blocks_per_tile)])

    acc = jnp.concatenate(
        [acc_ref[h * V_ROWS:h * V_ROWS + HEAD_DIM, :]
         * (1.0 / acc_ref[h * V_ROWS + HEAD_DIM:h * V_ROWS + HEAD_DIM + 1, :])
         for h in range(HEADS)], axis=0)
    at = (acc * za_ref[0].astype(F32)).astype(BF16)
    ya = _dot_tn(at, wa_ref[...])
    merged = (sga_ref[0].astype(F32) * ya + mb_ref[0].astype(F32)).astype(BF16)
    o_ref[0] = x_ref[0] + gate_ref[0] * _dot(merged, wo_ref[...])


def _attn(running_max, qt, cq, k, kx, vt, za, sga, mb, x, gate3, wa, wo):
    b, s, d = x.shape
    const = lambda shape: pl.BlockSpec(shape, lambda bi, qi: (0,) * len(shape))
    in_specs = [
        pl.BlockSpec((1, ATTN_W, TQ), lambda bi, qi: (bi, 0, qi)),
        pl.BlockSpec((1, HEADS, TQ), lambda bi, qi: (bi, 0, qi)),
        pl.BlockSpec((1, s, ATTN_W), lambda bi, qi: (bi, 0, 0)),
        pl.BlockSpec((1, s, LANES), lambda bi, qi: (bi, 0, 0)),
        pl.BlockSpec((1, s // TK, HEADS * V_ROWS, TK), lambda bi, qi: (bi, 0, 0, 0)),
        pl.BlockSpec((1, ATTN_W, TQ), lambda bi, qi: (bi, 0, qi)),
        pl.BlockSpec((1, TQ, d), lambda bi, qi: (bi, qi, 0)),
        pl.BlockSpec((1, TQ, d), lambda bi, qi: (bi, qi, 0)),
        pl.BlockSpec((1, TQ, d), lambda bi, qi: (bi, qi, 0)),
        pl.BlockSpec((1, 1, d), lambda bi, qi: (bi, 0, 0)),
        const(wa.shape), const(wo.shape),
    ]
    return pl.pallas_call(
        functools.partial(_attn_kernel, running_max=running_max),
        grid=(b, s // TQ),
        in_specs=in_specs,
        out_specs=pl.BlockSpec((1, TQ, d), lambda bi, qi: (bi, qi, 0)),
        out_shape=jax.ShapeDtypeStruct((b, s, d), F32),
        scratch_shapes=[pltpu.VMEM((HEADS, 2 * LANES, TQ), BF16),
                        pltpu.VMEM((HEADS, TQ), F32),
                        pltpu.VMEM((HEADS * V_ROWS, TQ), F32)],
        compiler_params=pltpu.CompilerParams(
            dimension_semantics=("arbitrary", "arbitrary"),
            vmem_limit_bytes=VMEM_LIMIT),
        name="attn_running_max" if running_max else "attn",
    )(qt, cq, k, kx, vt, za, sga, mb, x, gate3, wa, wo)


def _layer(x, c, w_ada, b_ada, norm_g, w_in, b_f, q_norm_g, k_norm_g, conv_w,
           w_attn_out, w_conv_out, w_o):
    b, s, d = x.shape
    ada = _ada(c, w_ada, b_ada)
    ada3 = ada.reshape(b, 1, 3 * d)
    gate3 = ada3[:, :, 2 * d:]

    sizes = (ATTN_W, ATTN_W, ATTN_W, HEADS, ATTN_W, CONV_W, CONV_W, CONV_W, CONV_W, d, d)
    offs = [0]
    for n in sizes:
        offs.append(offs[-1] + n)
    col = lambda i: w_in[:, offs[i]:offs[i + 1]]
    w_q, w_k, w_v, w_f, w_za, w_gb, w_gc, w_u, w_zb, w_ga, w_gb2 = (col(i) for i in range(11))

    rep = jnp.tile(w_f, (1, CUM_PARTS))
    fext = jnp.zeros((d, LANES), F32)
    fext = fext.at[:, 0:ONES_LANE].set(rep).at[:, LO_LANE:LO_LANE + ONES_LANE].set(rep)
    wn = jnp.concatenate([w_k, fext, w_gb, w_gc, w_u, w_zb, w_ga, w_gb2], axis=1).astype(BF16)
    wt = jnp.concatenate([w_q.T, w_v.T, w_za.T, w_f.T, w_f.T], axis=0).astype(BF16)

    bf_rep = jnp.tile(b_f, CUM_PARTS)
    bfn = jnp.zeros((1, LANES), F32)
    bfn = bfn.at[0, 0:ONES_LANE].set(bf_rep).at[0, LO_LANE:LO_LANE + ONES_LANE].set(bf_rep)
    bft = jnp.broadcast_to(jnp.concatenate([b_f, b_f])[:, None], (2 * HEADS, TM))
    gq = (jnp.tile(q_norm_g, HEADS) * HEAD_DIM ** -0.5).reshape(ATTN_W, 1)
    gk = jnp.tile(k_norm_g, HEADS).reshape(1, ATTN_W)

    head_of = jnp.arange(ATTN_W) // HEAD_DIM
    pm = jnp.where(head_of[:, None] == head_of[None, :], 1.0 / HEAD_DIM, 0.0).astype(BF16)
    tok = jnp.arange(TM)
    lt = (tok[None, :] <= tok[:, None]).astype(BF16)
    ut = jnp.concatenate([lt.T, jnp.ones((TM, LANES), BF16)], axis=1)

    k, kx, qt, cq, vt, za, sga, mb = _proj(
        x, ada3, norm_g.reshape(1, d), wn, wt, bfn, bft, gq, gk, conv_w,
        w_conv_out.astype(BF16), pm, lt, ut)
    logit_bound = HEAD_DIM ** 0.5 * jnp.max(jnp.abs(q_norm_g)) * jnp.max(jnp.abs(k_norm_g))
    args = (qt, cq, k, kx, vt, za, sga, mb, x, gate3, w_attn_out.astype(BF16), w_o.astype(BF16))
    return lax.cond(logit_bound <= MAX_RAW_LOGIT,
                    functools.partial(_attn, False), functools.partial(_attn, True), *args)


@jax.jit
def kernel(x, c, w_ada, b_ada, norm_g, w_in, b_f, q_norm_g, k_norm_g, conv_w,
           w_attn_out, w_conv_out, w_o):
    for i in range(w_ada.shape[0]):
        x = _layer(x, c, w_ada[i], b_ada[i], norm_g[i], w_in[i], b_f[i], q_norm_g[i],
                   k_norm_g[i], conv_w[i], w_attn_out[i], w_conv_out[i], w_o[i])
    return x
```

```python
import functools

import jax
import jax.numpy as jnp
import numpy as np
from jax import lax
from jax.experimental import pallas as pl
from jax.experimental.pallas import tpu as pltpu

D_MODEL = 1024
HEADS = 8
HEAD_DIM = 64
ATTN_W = HEADS * HEAD_DIM
CONV_W = 512
CONV_K = 3
EPS = 1e-6

LANES = 128
TM = 512
TQ = 512
TK = 256
CUM_PARTS = 3
ONES_LANE = CUM_PARTS * HEADS
LO_LANE = 32
V_ROWS = HEAD_DIM + 16
SCORE_LOOKAHEAD = 4
NEG_BIG = -1e30
MAX_RAW_LOGIT = 40.0
SKIP_LOG_WEIGHT = 30.0
LOGIT_BOUND_MARGIN = 1.05
VMEM_LIMIT = 56 * 1024 * 1024

F32 = jnp.float32
BF16 = jnp.bfloat16


def _log_sigmoid(x):
    return jnp.minimum(x, 0.0) - jnp.log(1.0 + jnp.exp(-jnp.abs(x)))


def _sigmoid(x):
    return 0.5 * jnp.tanh(0.5 * x) + 0.5


def _silu(x):
    hx = 0.5 * x
    return hx * jnp.tanh(hx) + hx


def _dot(a, b):
    return jnp.dot(a, b, preferred_element_type=F32)


def _dot_nt(a, b):
    return lax.dot_general(a, b, (((1,), (1,)), ((), ())), preferred_element_type=F32)


def _dot_tn(a, b):
    return lax.dot_general(a, b, (((0,), (0,)), ((), ())), preferred_element_type=F32)


def _ada_kernel(c_ref, w_ref, b_ref, o_ref):
    o_ref[...] = jnp.dot(c_ref[...], w_ref[...], preferred_element_type=F32,
                         precision=lax.Precision.HIGHEST) + b_ref[...]


def _ada(c, w_ada, b_ada):
    b, d = c.shape
    n = w_ada.shape[1]
    return pl.pallas_call(
        _ada_kernel,
        grid=(n // d,),
        in_specs=[pl.BlockSpec((b, d), lambda j: (0, 0)),
                  pl.BlockSpec((d, d), lambda j: (0, j)),
                  pl.BlockSpec((1, d), lambda j: (0, j))],
        out_specs=pl.BlockSpec((b, d), lambda j: (0, j)),
        out_shape=jax.ShapeDtypeStruct((b, n), F32),
        name="ada",
    )(c, w_ada, b_ada.reshape(1, n))


N_K = 0
N_F = N_K + ATTN_W
N_GB = N_F + LANES
N_GC = N_GB + CONV_W
N_U = N_GC + CONV_W
N_ZB = N_U + CONV_W
N_GA = N_ZB + CONV_W
N_GB2 = N_GA + D_MODEL
N_END = N_GB2 + D_MODEL
T_Q = 0
T_V = T_Q + ATTN_W
T_ZA = T_V + ATTN_W
T_F = T_ZA + ATTN_W
T_END = T_F + 2 * HEADS


def _proj_kernel(x_ref, xnext_ref, ada_ref, ng_ref, wn_ref, wt_ref, bfn_ref, bft_ref, gq_ref, gk_ref,
                 cw_ref, wb_ref, pm_ref, lt_ref, ut_ref,
                 k_ref, kx_ref, qt_ref, cq_ref, vt_ref, za_ref, sga_ref, mb_ref,
                 cn_ref, ct_ref, cu_ref, h_ref):
    s = pl.program_id(1)
    slot = lax.rem(s, 2)

    def modulated_norm(xr):
        x = xr[0]
        shift = ada_ref[0, :, 0:D_MODEL]
        scale = ada_ref[0, :, D_MODEL:2 * D_MODEL]
        xn = x * lax.rsqrt(jnp.mean(x * x, axis=-1, keepdims=True) + EPS)
        return (xn * ng_ref[...] * (1.0 + scale) + shift).astype(BF16)

    @pl.when(s == 0)
    def _():
        cn_ref[...] = jnp.zeros_like(cn_ref)
        ct_ref[...] = jnp.zeros_like(ct_ref)
        cu_ref[0:8, :] = jnp.zeros((8, CONV_W), F32)
        h_ref[0] = modulated_norm(x_ref)

    nat = lambda lo, hi: _dot(h_ref[slot], wn_ref[:, lo:hi])
    trn = lambda lo, hi: _dot_nt(wt_ref[lo:hi, :], h_ref[slot])
    lane = lax.broadcasted_iota(jnp.int32, (TM, LANES), 1)
    in_hi = lane < ONES_LANE
    in_lo = (lane >= LO_LANE) & (lane < LO_LANE + ONES_LANE)

    gc = nat(N_GC, N_U)
    u = nat(N_U, N_ZB)
    kf = nat(N_K, N_GB)
    kraw = kf[:, 0:ATTN_W]
    flog = kf[:, ATTN_W:]

    cu = gc * u
    cu_ref[8:8 + TM, :] = cu
    conv = (cw_ref[2:3, :] * cu + cw_ref[1:2, :] * cu_ref[7:7 + TM, :]
            + cw_ref[0:1, :] * cu_ref[6:6 + TM, :])
    cu_ref[0:8, :] = cu[TM - 8:TM, :]
    gb = nat(N_GB, N_GC)
    zb = nat(N_ZB, N_GA)

    kk = (kraw * kraw).astype(BF16)
    lf = _log_sigmoid(flog + bfn_ref[...])
    lf_lo = lf - lf.astype(BF16).astype(F32)
    parts = jnp.where(in_hi, lf, jnp.where(in_lo, lf_lo, 0.0)).astype(BF16)
    kms = _dot(kk, pm_ref[...])
    csum = _dot(lt_ref[...], parts)
    ga = nat(N_GA, N_GB2)

    gb2 = nat(N_GB2, N_END)
    ob = (gb * conv * _silu(zb)).astype(BF16)
    mbr = _dot(ob, wb_ref[...])

    sga_ref[0] = _sigmoid(ga).astype(BF16)
    k_ref[0] = (kraw * lax.rsqrt(kms + EPS) * gk_ref[...]).astype(BF16)
    cum = csum + pltpu.roll(csum, LANES - LO_LANE, 1) + cn_ref[...]
    cn_ref[...] = cum[TM - 1:TM, :]
    r1 = cum - cum.astype(BF16).astype(F32)
    r2 = r1 - r1.astype(BF16).astype(F32)
    piece = jnp.where(lane < HEADS, cum, jnp.where(lane < 2 * HEADS, r1, r2))
    is_one = (lane >= ONES_LANE) & (lane < ONES_LANE + CUM_PARTS)
    kx_ref[0] = jnp.where(in_hi, -piece, jnp.where(is_one, 1.0, 0.0)).astype(BF16)
    zf = trn(T_ZA, T_END)
    qraw = trn(T_Q, T_V)

    mb_ref[0] = (_sigmoid(gb2) * mbr).astype(BF16)
    lft = _log_sigmoid(zf[ATTN_W:, :] + bft_ref[...])
    lft_lo = lft - lft.astype(BF16).astype(F32)
    row = lax.broadcasted_iota(jnp.int32, (2 * HEADS, TM), 0)
    ct = _dot(jnp.where(row < HEADS, lft, lft_lo).astype(BF16), ut_ref[...])

    za_ref[0] = _silu(zf[0:ATTN_W, :]).astype(BF16)
    ct = ct[0:HEADS, :] + ct[HEADS:2 * HEADS, :]
    carry = ct_ref[...]
    cq_ref[0] = ct[:, 0:TM] + jnp.concatenate([carry] * (TM // LANES), axis=1)
    ct_ref[...] = carry + ct[:, TM:TM + LANES]
    vraw = trn(T_V, T_ZA)

    q3 = qraw.reshape(HEADS, HEAD_DIM, TM)
    qn = q3 * lax.rsqrt(jnp.mean(q3 * q3, axis=1, keepdims=True) + EPS)
    qt_ref[0] = (qn.reshape(ATTN_W, TM) * gq_ref[...]).astype(BF16)

    vt = vraw.astype(BF16)
    for i in range(TM // TK):
        for hd in range(HEADS):
            vt_ref[0, i, hd * V_ROWS:hd * V_ROWS + HEAD_DIM, :] = (
                vt[hd * HEAD_DIM:(hd + 1) * HEAD_DIM, i * TK:(i + 1) * TK])
            vt_ref[0, i, hd * V_ROWS + HEAD_DIM:(hd + 1) * V_ROWS, :] = (
                jnp.ones((V_ROWS - HEAD_DIM, TK), BF16))

    h_ref[1 - slot] = modulated_norm(xnext_ref)


def _proj(x, ada3, ng, wn, wt, bfn, bft, gq, gk, cw, wb, pm, lt, ut):
    b, s, d = x.shape
    const = lambda shape: pl.BlockSpec(shape, lambda bi, si: (0,) * len(shape))
    out_shape = [
        jax.ShapeDtypeStruct((b, s, ATTN_W), BF16),
        jax.ShapeDtypeStruct((b, s, LANES), BF16),
        jax.ShapeDtypeStruct((b, ATTN_W, s), BF16),
        jax.ShapeDtypeStruct((b, HEADS, s), F32),
        jax.ShapeDtypeStruct((b, s // TK, HEADS * V_ROWS, TK), BF16),
        jax.ShapeDtypeStruct((b, ATTN_W, s), BF16),
        jax.ShapeDtypeStruct((b, s, d), BF16),
        jax.ShapeDtypeStruct((b, s, d), BF16),
    ]
    out_specs = [
        pl.BlockSpec((1, TM, ATTN_W), lambda bi, si: (bi, si, 0)),
        pl.BlockSpec((1, TM, LANES), lambda bi, si: (bi, si, 0)),
        pl.BlockSpec((1, ATTN_W, TM), lambda bi, si: (bi, 0, si)),
        pl.BlockSpec((1, HEADS, TM), lambda bi, si: (bi, 0, si)),
        pl.BlockSpec((1, TM // TK, HEADS * V_ROWS, TK), lambda bi, si: (bi, si, 0, 0)),
        pl.BlockSpec((1, ATTN_W, TM), lambda bi, si: (bi, 0, si)),
        pl.BlockSpec((1, TM, d), lambda bi, si: (bi, si, 0)),
        pl.BlockSpec((1, TM, d), lambda bi, si: (bi, si, 0)),
    ]
    last = s // TM - 1
    in_specs = [
        pl.BlockSpec((1, TM, d), lambda bi, si: (bi, si, 0)),
        pl.BlockSpec((1, TM, d), lambda bi, si: (bi, jnp.minimum(si + 1, last), 0)),
        pl.BlockSpec((1, 1, 3 * d), lambda bi, si: (bi, 0, 0)),
        const(ng.shape), const(wn.shape), const(wt.shape), const(bfn.shape), const(bft.shape),
        const(gq.shape), const(gk.shape), const(cw.shape), const(wb.shape), const(pm.shape),
        const(lt.shape), const(ut.shape),
    ]
    return pl.pallas_call(
        _proj_kernel,
        grid=(b, s // TM),
        in_specs=in_specs,
        out_specs=out_specs,
        out_shape=out_shape,
        scratch_shapes=[pltpu.VMEM((1, LANES), F32),
                        pltpu.VMEM((HEADS, LANES), F32),
                        pltpu.VMEM((TM + 8, CONV_W), F32),
                        pltpu.VMEM((2, TM, D_MODEL), BF16)],
        compiler_params=pltpu.CompilerParams(
            dimension_semantics=("arbitrary", "arbitrary"),
            vmem_limit_bytes=VMEM_LIMIT),
        name="proj",
    )(x, x, ada3, ng, wn, wt, bfn, bft, gq, gk, cw, wb, pm, lt, ut)


def _attn_kernel(cend_ref, bound_ref, qt_ref, cq_ref, k_ref, kx_ref, vt_ref, za_ref, sga_ref, mb_ref,
                 x_ref, gate_ref, wa_ref, wo_ref, o_ref, rhs_ref, m_ref, acc_ref, *, running_max):
    bi = pl.program_id(0)
    qi = pl.program_id(1)
    blocks_per_tile = TQ // TK
    n_blocks = k_ref.shape[1] // TK

    last_before = qi * blocks_per_tile - 1
    slack = LOGIT_BOUND_MARGIN * 2.0 * bound_ref[0] + SKIP_LOG_WEIGHT

    def count_skippable(j, n):
        gap = cend_ref[bi * HEADS * n_blocks + last_before] - cend_ref[bi * HEADS * n_blocks + j]
        for h in range(1, HEADS):
            base = (bi * HEADS + h) * n_blocks
            gap = jnp.maximum(gap, cend_ref[base + last_before] - cend_ref[base + j])
        return n + ((gap + slack <= 0.0) & (n == j)).astype(jnp.int32)

    n_skip = lax.fori_loop(0, jnp.maximum(last_before, 0), count_skippable, jnp.int32(0))
    first_iter = n_skip // blocks_per_tile

    row = lax.broadcasted_iota(jnp.int32, (LANES, TQ), 0)
    for h in range(HEADS):
        pair, half = divmod(h, 2)
        qp = qt_ref[0, pair * LANES:(pair + 1) * LANES, :]
        mine = (row >= half * HEAD_DIM) & (row < (half + 1) * HEAD_DIM)
        rhs_ref[h, 0:LANES, :] = qp * jnp.where(mine, 1.0, 0.0).astype(BF16)
        cq = cq_ref[0, h:h + 1, :]
        r1 = cq - cq.astype(BF16).astype(F32)
        r2 = r1 - r1.astype(BF16).astype(F32)
        sel = (row == h) | (row == HEADS + h) | (row == 2 * HEADS + h)
        f = jnp.where(row == ONES_LANE, cq,
                      jnp.where(row == ONES_LANE + 1, r1,
                                jnp.where(row == ONES_LANE + 2, r2, 0.0)))
        rhs_ref[h, LANES:2 * LANES, :] = jnp.where(sel, 1.0, f).astype(BF16)

    if running_max:
        m_ref[...] = jnp.full(m_ref.shape, NEG_BIG, F32)
    acc_ref[...] = jnp.zeros_like(acc_ref)

    def scores(j, h, q_lo):
        start = pl.multiple_of(j * TK, TK)
        pair = h // 2
        lhs = jnp.concatenate(
            [k_ref[0, pl.ds(start, TK), pair * LANES:(pair + 1) * LANES],
             kx_ref[0, pl.ds(start, TK), :]], axis=1)
        return _dot(lhs, rhs_ref[h, :, q_lo:TQ])

    def run_blocks(blocks):
        units = [(blk, h) for blk in blocks for h in range(HEADS)]
        ahead = [scores(j, h, q_lo) for (j, _, q_lo), h in units[:SCORE_LOOKAHEAD]]
        for i, ((j, masked, q_lo), h) in enumerate(units):
            sc = ahead.pop(0)
            if i + SCORE_LOOKAHEAD < len(units):
                (jn, _, qn), hn = units[i + SCORE_LOOKAHEAD]
                ahead.append(scores(jn, hn, qn))
            if masked:
                kpos = lax.broadcasted_iota(jnp.int32, sc.shape, 0)
                qpos = lax.broadcasted_iota(jnp.int32, sc.shape, 1)
                sc = jnp.where(kpos <= qpos, sc, NEG_BIG)
            rows = slice(h * V_ROWS, (h + 1) * V_ROWS)
            if running_max:
                m_prev = m_ref[h:h + 1, q_lo:TQ]
                m_new = jnp.maximum(m_prev, jnp.max(sc, axis=0, keepdims=True))
                m_ref[h:h + 1, q_lo:TQ] = m_new
                p = jnp.exp(sc - m_new).astype(BF16)
                acc_ref[rows, q_lo:TQ] = (jnp.exp(m_prev - m_new) * acc_ref[rows, q_lo:TQ]
                                          + _dot(vt_ref[0, j, rows, :], p))
            else:
                p = jnp.exp(sc).astype(BF16)
                acc_ref[rows, q_lo:TQ] += _dot(vt_ref[0, j, rows, :], p)

    def body(i, carry):
        run_blocks([(i * blocks_per_tile + t, False, 0) for t in range(blocks_per_tile)])
        return carry

    lax.fori_loop(first_iter, qi, body, 0)
    run_blocks([(qi * blocks_per_tile + t, True, t * TK) for t in range(blocks_per_tile)])

    acc = jnp.concatenate(
        [acc_ref[h * V_ROWS:h * V_ROWS + HEAD_DIM, :]
         * (1.0 / acc_ref[h * V_ROWS + HEAD_DIM:h * V_ROWS + HEAD_DIM + 1, :])
         for h in range(HEADS)], axis=0)
    at = (acc * za_ref[0].astype(F32)).astype(BF16)
    ya = _dot_tn(at, wa_ref[...])
    merged = (sga_ref[0].astype(F32) * ya + mb_ref[0].astype(F32)).astype(BF16)
    o_ref[0] = x_ref[0] + gate_ref[0] * _dot(merged, wo_ref[...])


def _attn(running_max, cend, bound, qt, cq, k, kx, vt, za, sga, mb, x, gate3, wa, wo):
    b, s, d = x.shape
    const = lambda shape: pl.BlockSpec(shape, lambda bi, qi: (0,) * len(shape))
    in_specs = [
        pl.BlockSpec(memory_space=pltpu.SMEM),
        pl.BlockSpec(memory_space=pltpu.SMEM),
        pl.BlockSpec((1, ATTN_W, TQ), lambda bi, qi: (bi, 0, qi)),
        pl.BlockSpec((1, HEADS, TQ), lambda bi, qi: (bi, 0, qi)),
        pl.BlockSpec((1, s, ATTN_W), lambda bi, qi: (bi, 0, 0)),
        pl.BlockSpec((1, s, LANES), lambda bi, qi: (bi, 0, 0)),
        pl.BlockSpec((1, s // TK, HEADS * V_ROWS, TK), lambda bi, qi: (bi, 0, 0, 0)),
        pl.BlockSpec((1, ATTN_W, TQ), lambda bi, qi: (bi, 0, qi)),
        pl.BlockSpec((1, TQ, d), lambda bi, qi: (bi, qi, 0)),
        pl.BlockSpec((1, TQ, d), lambda bi, qi: (bi, qi, 0)),
        pl.BlockSpec((1, TQ, d), lambda bi, qi: (bi, qi, 0)),
        pl.BlockSpec((1, 1, d), lambda bi, qi: (bi, 0, 0)),
        const(wa.shape), const(wo.shape),
    ]
    return pl.pallas_call(
        functools.partial(_attn_kernel, running_max=running_max),
        grid=(b, s // TQ),
        in_specs=in_specs,
        out_specs=pl.BlockSpec((1, TQ, d), lambda bi, qi: (bi, qi, 0)),
        out_shape=jax.ShapeDtypeStruct((b, s, d), F32),
        scratch_shapes=[pltpu.VMEM((HEADS, 2 * LANES, TQ), BF16),
                        pltpu.VMEM((HEADS, TQ), F32),
                        pltpu.VMEM((HEADS * V_ROWS, TQ), F32)],
        compiler_params=pltpu.CompilerParams(
            dimension_semantics=("arbitrary", "arbitrary"),
            vmem_limit_bytes=VMEM_LIMIT),
        name="attn_running_max" if running_max else "attn",
    )(cend, bound, qt, cq, k, kx, vt, za, sga, mb, x, gate3, wa, wo)


def _layer(x, c, w_ada, b_ada, norm_g, w_in, b_f, q_norm_g, k_norm_g, conv_w,
           w_attn_out, w_conv_out, w_o):
    b, s, d = x.shape
    ada = _ada(c, w_ada, b_ada)
    ada3 = ada.reshape(b, 1, 3 * d)
    gate3 = ada3[:, :, 2 * d:]

    sizes = (ATTN_W, ATTN_W, ATTN_W, HEADS, ATTN_W, CONV_W, CONV_W, CONV_W, CONV_W, d, d)
    offs = [0]
    for n in sizes:
        offs.append(offs[-1] + n)
    col = lambda i: w_in[:, offs[i]:offs[i + 1]]
    w_q, w_k, w_v, w_f, w_za, w_gb, w_gc, w_u, w_zb, w_ga, w_gb2 = (col(i) for i in range(11))

    def hi_lo_lanes(v):
        rep = jnp.concatenate([v] * CUM_PARTS, axis=-1)
        gap = jnp.zeros(v.shape[:-1] + (LO_LANE - ONES_LANE,), v.dtype)
        tail = jnp.zeros(v.shape[:-1] + (LANES - LO_LANE - ONES_LANE,), v.dtype)
        return jnp.concatenate([rep, gap, rep, tail], axis=-1)

    wn = jnp.concatenate([w_k, hi_lo_lanes(w_f), w_gb, w_gc, w_u, w_zb, w_ga, w_gb2],
                         axis=1).astype(BF16)
    wt = jnp.concatenate([w_q, w_v, w_za, w_f, w_f], axis=1).astype(BF16).T

    bfn = hi_lo_lanes(b_f.reshape(1, HEADS))
    bft = jnp.broadcast_to(jnp.concatenate([b_f, b_f])[:, None], (2 * HEADS, TM))
    gq = (jnp.tile(q_norm_g, HEADS) * HEAD_DIM ** -0.5).reshape(ATTN_W, 1)
    gk = jnp.tile(k_norm_g, HEADS).reshape(1, ATTN_W)

    head_of = np.arange(ATTN_W) // HEAD_DIM
    pm = jnp.asarray(np.where(head_of[:, None] == head_of[None, :], 1.0 / HEAD_DIM, 0.0), BF16)
    tok = np.arange(TM)
    lower = (tok[None, :] <= tok[:, None]).astype(np.float32)
    lt = jnp.asarray(lower, BF16)
    ut = jnp.asarray(np.concatenate([lower.T, np.ones((TM, LANES), np.float32)], axis=1), BF16)

    k, kx, qt, cq, vt, za, sga, mb = _proj(
        x, ada3, norm_g.reshape(1, d), wn, wt, bfn, bft, gq, gk, conv_w,
        w_conv_out.astype(BF16), pm, lt, ut)
    logit_bound = HEAD_DIM ** 0.5 * jnp.max(jnp.abs(q_norm_g)) * jnp.max(jnp.abs(k_norm_g))
    cend = cq[:, :, TK - 1::TK].reshape(-1)
    args = (cend, logit_bound.reshape(1), qt, cq, k, kx, vt, za, sga, mb, x, gate3,
            w_attn_out.astype(BF16), w_o.astype(BF16))
    return lax.cond(logit_bound <= MAX_RAW_LOGIT,
                    functools.partial(_attn, False), functools.partial(_attn, True), *args)


@jax.jit
def kernel(x, c, w_ada, b_ada, norm_g, w_in, b_f, q_norm_g, k_norm_g, conv_w,
           w_attn_out, w_conv_out, w_o):
    for i in range(w_ada.shape[0]):
        x = _layer(x, c, w_ada[i], b_ada[i], norm_g[i], w_in[i], b_f[i], q_norm_g[i],
                   k_norm_g[i], conv_w[i], w_attn_out[i], w_conv_out[i], w_o[i])
    return x
```

```python
import functools

import jax
import jax.numpy as jnp
import numpy as np
from jax import lax
from jax.experimental import pallas as pl
from jax.experimental.pallas import tpu as pltpu

D_MODEL = 1024
HEADS = 8
HEAD_DIM = 64
ATTN_W = HEADS * HEAD_DIM
CONV_W = 512
CONV_K = 3
EPS = 1e-6

LANES = 128
TM = 512
TQ = 512
TK = 256
CUM_PARTS = 3
ONES_LANE = CUM_PARTS * HEADS
LO_LANE = 32
V_ROWS = HEAD_DIM + 16
SCORE_LOOKAHEAD = 4
UNITS_PER_ITER = 8
NEG_BIG = -1e30
MAX_RAW_LOGIT = 40.0
SKIP_LOG_WEIGHT = 30.0
LOGIT_BOUND_MARGIN = 1.05
VMEM_LIMIT = 56 * 1024 * 1024

F32 = jnp.float32
BF16 = jnp.bfloat16


def _log_sigmoid(x):
    return jnp.minimum(x, 0.0) - jnp.log(1.0 + jnp.exp(-jnp.abs(x)))


def _sigmoid(x):
    return 0.5 * jnp.tanh(0.5 * x) + 0.5


def _silu(x):
    hx = 0.5 * x
    return hx * jnp.tanh(hx) + hx


def _dot(a, b):
    return jnp.dot(a, b, preferred_element_type=F32)


def _dot_nt(a, b):
    return lax.dot_general(a, b, (((1,), (1,)), ((), ())), preferred_element_type=F32)


def _dot_tn(a, b):
    return lax.dot_general(a, b, (((0,), (0,)), ((), ())), preferred_element_type=F32)


def _ada_kernel(c_ref, w_ref, b_ref, o_ref):
    o_ref[...] = jnp.dot(c_ref[...], w_ref[...], preferred_element_type=F32,
                         precision=lax.Precision.HIGHEST) + b_ref[...]


def _ada(c, w_ada, b_ada):
    b, d = c.shape
    n = w_ada.shape[1]
    return pl.pallas_call(
        _ada_kernel,
        grid=(n // d,),
        in_specs=[pl.BlockSpec((b, d), lambda j: (0, 0)),
                  pl.BlockSpec((d, d), lambda j: (0, j)),
                  pl.BlockSpec((1, d), lambda j: (0, j))],
        out_specs=pl.BlockSpec((b, d), lambda j: (0, j)),
        out_shape=jax.ShapeDtypeStruct((b, n), F32),
        name="ada",
    )(c, w_ada, b_ada.reshape(1, n))


N_K = 0
N_F = N_K + ATTN_W
N_GB = N_F + LANES
N_GC = N_GB + CONV_W
N_U = N_GC + CONV_W
N_ZB = N_U + CONV_W
N_GA = N_ZB + CONV_W
N_GB2 = N_GA + D_MODEL
N_END = N_GB2 + D_MODEL
T_Q = 0
T_V = T_Q + ATTN_W
T_ZA = T_V + ATTN_W
T_F = T_ZA + ATTN_W
T_END = T_F + 2 * HEADS


def _proj_kernel(x_ref, xnext_ref, ada_ref, ng_ref, wn_ref, wt_ref, bfn_ref, bft_ref, gq_ref, gk_ref,
                 cw_ref, wb_ref, pm_ref, lt_ref, ut_ref,
                 k_ref, kx_ref, qt_ref, cq_ref, vt_ref, za_ref, sga_ref, mb_ref,
                 cn_ref, ct_ref, cu_ref, h_ref):
    s = pl.program_id(1)
    slot = lax.rem(s, 2)

    def modulated_norm(xr):
        x = xr[0]
        shift = ada_ref[0, :, 0:D_MODEL]
        scale = ada_ref[0, :, D_MODEL:2 * D_MODEL]
        xn = x * lax.rsqrt(jnp.mean(x * x, axis=-1, keepdims=True) + EPS)
        return (xn * ng_ref[...] * (1.0 + scale) + shift).astype(BF16)

    @pl.when(s == 0)
    def _():
        cn_ref[...] = jnp.zeros_like(cn_ref)
        ct_ref[...] = jnp.zeros_like(ct_ref)
        cu_ref[0:8, :] = jnp.zeros((8, CONV_W), F32)
        h_ref[0] = modulated_norm(x_ref)

    nat = lambda lo, hi: _dot(h_ref[slot], wn_ref[:, lo:hi])
    trn = lambda lo, hi: _dot_nt(wt_ref[lo:hi, :], h_ref[slot])
    lane = lax.broadcasted_iota(jnp.int32, (TM, LANES), 1)
    in_hi = lane < ONES_LANE
    in_lo = (lane >= LO_LANE) & (lane < LO_LANE + ONES_LANE)

    gc = nat(N_GC, N_U)
    u = nat(N_U, N_ZB)
    kf = nat(N_K, N_GB)
    kraw = kf[:, 0:ATTN_W]
    flog = kf[:, ATTN_W:]

    cu = gc * u
    cu_ref[8:8 + TM, :] = cu
    conv = (cw_ref[2:3, :] * cu + cw_ref[1:2, :] * cu_ref[7:7 + TM, :]
            + cw_ref[0:1, :] * cu_ref[6:6 + TM, :])
    cu_ref[0:8, :] = cu[TM - 8:TM, :]
    gb = nat(N_GB, N_GC)
    zb = nat(N_ZB, N_GA)

    kk = (kraw * kraw).astype(BF16)
    lf = _log_sigmoid(flog + bfn_ref[...])
    lf_lo = lf - lf.astype(BF16).astype(F32)
    parts = jnp.where(in_hi, lf, jnp.where(in_lo, lf_lo, 0.0)).astype(BF16)
    kms = _dot(kk, pm_ref[...])
    csum = _dot(lt_ref[...], parts)
    ga = nat(N_GA, N_GB2)

    gb2 = nat(N_GB2, N_END)
    ob = (gb * conv * _silu(zb)).astype(BF16)
    mbr = _dot(ob, wb_ref[...])

    sga_ref[0] = _sigmoid(ga).astype(BF16)
    kn = (kraw * lax.rsqrt(kms + EPS) * gk_ref[...]).astype(BF16)
    for pair in range(HEADS // 2):
        k_ref[0, pair] = kn[:, pair * LANES:(pair + 1) * LANES]
    cum = csum + pltpu.roll(csum, LANES - LO_LANE, 1) + cn_ref[...]
    cn_ref[...] = cum[TM - 1:TM, :]
    r1 = cum - cum.astype(BF16).astype(F32)
    r2 = r1 - r1.astype(BF16).astype(F32)
    piece = jnp.where(lane < HEADS, cum, jnp.where(lane < 2 * HEADS, r1, r2))
    is_one = (lane >= ONES_LANE) & (lane < ONES_LANE + CUM_PARTS)
    kx_ref[0] = jnp.where(in_hi, -piece, jnp.where(is_one, 1.0, 0.0)).astype(BF16)
    zf = trn(T_ZA, T_END)
    qraw = trn(T_Q, T_V)

    mb_ref[0] = (_sigmoid(gb2) * mbr).astype(BF16)
    lft = _log_sigmoid(zf[ATTN_W:, :] + bft_ref[...])
    lft_lo = lft - lft.astype(BF16).astype(F32)
    row = lax.broadcasted_iota(jnp.int32, (2 * HEADS, TM), 0)
    ct = _dot(jnp.where(row < HEADS, lft, lft_lo).astype(BF16), ut_ref[...])

    za_ref[0] = _silu(zf[0:ATTN_W, :]).astype(BF16)
    ct = ct[0:HEADS, :] + ct[HEADS:2 * HEADS, :]
    carry = ct_ref[...]
    cq_ref[0] = ct[:, 0:TM] + jnp.concatenate([carry] * (TM // LANES), axis=1)
    ct_ref[...] = carry + ct[:, TM:TM + LANES]
    vraw = trn(T_V, T_ZA)

    q3 = qraw.reshape(HEADS, HEAD_DIM, TM)
    qn = q3 * lax.rsqrt(jnp.mean(q3 * q3, axis=1, keepdims=True) + EPS)
    qt_ref[0] = (qn.reshape(ATTN_W, TM) * gq_ref[...]).astype(BF16)

    vt = vraw.astype(BF16)
    for i in range(TM // TK):
        for hd in range(HEADS):
            vt_ref[0, i, hd, 0:HEAD_DIM, :] = (
                vt[hd * HEAD_DIM:(hd + 1) * HEAD_DIM, i * TK:(i + 1) * TK])
            vt_ref[0, i, hd, HEAD_DIM:V_ROWS, :] = jnp.ones((V_ROWS - HEAD_DIM, TK), BF16)

    h_ref[1 - slot] = modulated_norm(xnext_ref)


def _proj(x, ada3, ng, wn, wt, bfn, bft, gq, gk, cw, wb, pm, lt, ut):
    b, s, d = x.shape
    const = lambda shape: pl.BlockSpec(shape, lambda bi, si: (0,) * len(shape))
    out_shape = [
        jax.ShapeDtypeStruct((b, HEADS // 2, s, LANES), BF16),
        jax.ShapeDtypeStruct((b, s, LANES), BF16),
        jax.ShapeDtypeStruct((b, ATTN_W, s), BF16),
        jax.ShapeDtypeStruct((b, HEADS, s), F32),
        jax.ShapeDtypeStruct((b, s // TK, HEADS, V_ROWS, TK), BF16),
        jax.ShapeDtypeStruct((b, ATTN_W, s), BF16),
        jax.ShapeDtypeStruct((b, s, d), BF16),
        jax.ShapeDtypeStruct((b, s, d), BF16),
    ]
    out_specs = [
        pl.BlockSpec((1, HEADS // 2, TM, LANES), lambda bi, si: (bi, 0, si, 0)),
        pl.BlockSpec((1, TM, LANES), lambda bi, si: (bi, si, 0)),
        pl.BlockSpec((1, ATTN_W, TM), lambda bi, si: (bi, 0, si)),
        pl.BlockSpec((1, HEADS, TM), lambda bi, si: (bi, 0, si)),
        pl.BlockSpec((1, TM // TK, HEADS, V_ROWS, TK), lambda bi, si: (bi, si, 0, 0, 0)),
        pl.BlockSpec((1, ATTN_W, TM), lambda bi, si: (bi, 0, si)),
        pl.BlockSpec((1, TM, d), lambda bi, si: (bi, si, 0)),
        pl.BlockSpec((1, TM, d), lambda bi, si: (bi, si, 0)),
    ]
    last = s // TM - 1
    in_specs = [
        pl.BlockSpec((1, TM, d), lambda bi, si: (bi, si, 0)),
        pl.BlockSpec((1, TM, d), lambda bi, si: (bi, jnp.minimum(si + 1, last), 0)),
        pl.BlockSpec((1, 1, 3 * d), lambda bi, si: (bi, 0, 0)),
        const(ng.shape), const(wn.shape), const(wt.shape), const(bfn.shape), const(bft.shape),
        const(gq.shape), const(gk.shape), const(cw.shape), const(wb.shape), const(pm.shape),
        const(lt.shape), const(ut.shape),
    ]
    return pl.pallas_call(
        _proj_kernel,
        grid=(b, s // TM),
        in_specs=in_specs,
        out_specs=out_specs,
        out_shape=out_shape,
        scratch_shapes=[pltpu.VMEM((1, LANES), F32),
                        pltpu.VMEM((HEADS, LANES), F32),
                        pltpu.VMEM((TM + 8, CONV_W), F32),
                        pltpu.VMEM((2, TM, D_MODEL), BF16)],
        compiler_params=pltpu.CompilerParams(
            dimension_semantics=("arbitrary", "arbitrary"),
            vmem_limit_bytes=VMEM_LIMIT),
        name="proj",
    )(x, x, ada3, ng, wn, wt, bfn, bft, gq, gk, cw, wb, pm, lt, ut)


def _attn_kernel(cend_ref, bound_ref, qt_ref, cq_ref, k_ref, kx_ref, vt_ref, za_ref, sga_ref, mb_ref,
                 x_ref, gate_ref, wa_ref, wo_ref, o_ref, rhs_ref, m_ref, acc_ref, *, running_max):
    bi = pl.program_id(0)
    qi = pl.program_id(1)
    blocks_per_tile = TQ // TK
    n_blocks = kx_ref.shape[1] // TK
    i32 = jnp.int32

    last_before = qi * blocks_per_tile - 1
    slack = LOGIT_BOUND_MARGIN * 2.0 * bound_ref[0] + SKIP_LOG_WEIGHT

    def count_skippable(j, skipped):
        out = []
        for h in range(HEADS):
            base = (bi * HEADS + h) * n_blocks
            gap = cend_ref[base + last_before] - cend_ref[base + j]
            out.append(skipped[h] + ((gap + slack <= 0.0) & (skipped[h] == j)).astype(i32))
        return tuple(out)

    first = lax.fori_loop(0, jnp.maximum(last_before, 0), count_skippable, (i32(0),) * HEADS)
    upto, shift, total = [], [], i32(0)
    for h in range(HEADS):
        shift.append(first[h] - total)
        total = total + (qi * blocks_per_tile - first[h])
        upto.append(total)

    row = lax.broadcasted_iota(jnp.int32, (LANES, TQ), 0)
    for h in range(HEADS):
        pair, half = divmod(h, 2)
        qp = qt_ref[0, pair * LANES:(pair + 1) * LANES, :]
        mine = (row >= half * HEAD_DIM) & (row < (half + 1) * HEAD_DIM)
        rhs_ref[h, 0:LANES, :] = qp * jnp.where(mine, 1.0, 0.0).astype(BF16)
        cq = cq_ref[0, h:h + 1, :]
        r1 = cq - cq.astype(BF16).astype(F32)
        r2 = r1 - r1.astype(BF16).astype(F32)
        sel = (row == h) | (row == HEADS + h) | (row == 2 * HEADS + h)
        f = jnp.where(row == ONES_LANE, cq,
                      jnp.where(row == ONES_LANE + 1, r1,
                                jnp.where(row == ONES_LANE + 2, r2, 0.0)))
        rhs_ref[h, LANES:2 * LANES, :] = jnp.where(sel, 1.0, f).astype(BF16)
    rhs_ref[HEADS] = jnp.zeros(rhs_ref.shape[1:], BF16)

    if running_max:
        m_ref[...] = jnp.full(m_ref.shape, NEG_BIG, F32)
    acc_ref[...] = jnp.zeros_like(acc_ref)

    def scores(j, h, q_lo):
        start = pl.multiple_of(j * TK, TK)
        pair = h // 2 if isinstance(h, int) else jnp.minimum(h, HEADS - 1) // 2
        lhs = jnp.concatenate([k_ref[0, pair, pl.ds(start, TK), :],
                               kx_ref[0, pl.ds(start, TK), :]], axis=1)
        return _dot(lhs, rhs_ref[h, :, q_lo:TQ])

    def run_units(units):
        ahead = [scores(j, h, q_lo) for j, h, _, q_lo in units[:SCORE_LOOKAHEAD]]
        for i, (j, h, masked, q_lo) in enumerate(units):
            sc = ahead.pop(0)
            if i + SCORE_LOOKAHEAD < len(units):
                jn, hn, _, qn = units[i + SCORE_LOOKAHEAD]
                ahead.append(scores(jn, hn, qn))
            if masked:
                kpos = lax.broadcasted_iota(jnp.int32, sc.shape, 0)
                qpos = lax.broadcasted_iota(jnp.int32, sc.shape, 1)
                sc = jnp.where(kpos <= qpos, sc, NEG_BIG)
            hv = h if isinstance(h, int) else jnp.minimum(h, HEADS - 1)
            if running_max:
                m_prev = m_ref[h, :, q_lo:TQ]
                m_new = jnp.maximum(m_prev, jnp.max(sc, axis=0, keepdims=True))
                m_ref[h, :, q_lo:TQ] = m_new
                p = jnp.exp(sc - m_new).astype(BF16)
                acc_ref[h, :, q_lo:TQ] = (jnp.exp(m_prev - m_new) * acc_ref[h, :, q_lo:TQ]
                                          + _dot(vt_ref[0, j, hv], p))
            else:
                p = jnp.exp(sc).astype(BF16)
                acc_ref[h, :, q_lo:TQ] += _dot(vt_ref[0, j, hv], p)

    def body(i, carry):
        units = []
        for t in range(UNITS_PER_ITER):
            u = i * UNITS_PER_ITER + t
            h = i32(0)
            off = shift[0]
            for g in range(HEADS):
                past = u >= upto[g]
                h = h + past.astype(i32)
                if g + 1 < HEADS:
                    off = jnp.where(past, shift[g + 1], off)
            j = jnp.where(u >= total, 0, u + off)
            units.append((j, h, False, 0))
        run_units(units)
        return carry

    lax.fori_loop(0, (total + UNITS_PER_ITER - 1) // UNITS_PER_ITER, body, 0)
    run_units([(qi * blocks_per_tile + t, h, True, t * TK)
               for t in range(blocks_per_tile) for h in range(HEADS)])

    acc = jnp.concatenate(
        [acc_ref[h, 0:HEAD_DIM, :] * (1.0 / acc_ref[h, HEAD_DIM:HEAD_DIM + 1, :])
         for h in range(HEADS)], axis=0)
    at = (acc * za_ref[0].astype(F32)).astype(BF16)
    ya = _dot_tn(at, wa_ref[...])
    merged = (sga_ref[0].astype(F32) * ya + mb_ref[0].astype(F32)).astype(BF16)
    o_ref[0] = x_ref[0] + gate_ref[0] * _dot(merged, wo_ref[...])


def _attn(running_max, cend, bound, qt, cq, k, kx, vt, za, sga, mb, x, gate3, wa, wo):
    b, s, d = x.shape
    const = lambda shape: pl.BlockSpec(shape, lambda bi, qi: (0,) * len(shape))
    in_specs = [
        pl.BlockSpec(memory_space=pltpu.SMEM),
        pl.BlockSpec(memory_space=pltpu.SMEM),
        pl.BlockSpec((1, ATTN_W, TQ), lambda bi, qi: (bi, 0, qi)),
        pl.BlockSpec((1, HEADS, TQ), lambda bi, qi: (bi, 0, qi)),
        pl.BlockSpec((1, HEADS // 2, s, LANES), lambda bi, qi: (bi, 0, 0, 0)),
        pl.BlockSpec((1, s, LANES), lambda bi, qi: (bi, 0, 0)),
        pl.BlockSpec((1, s // TK, HEADS, V_ROWS, TK), lambda bi, qi: (bi, 0, 0, 0, 0)),
        pl.BlockSpec((1, ATTN_W, TQ), lambda bi, qi: (bi, 0, qi)),
        pl.BlockSpec((1, TQ, d), lambda bi, qi: (bi, qi, 0)),
        pl.BlockSpec((1, TQ, d), lambda bi, qi: (bi, qi, 0)),
        pl.BlockSpec((1, TQ, d), lambda bi, qi: (bi, qi, 0)),
        pl.BlockSpec((1, 1, d), lambda bi, qi: (bi, 0, 0)),
        const(wa.shape), const(wo.shape),
    ]
    return pl.pallas_call(
        functools.partial(_attn_kernel, running_max=running_max),
        grid=(b, s // TQ),
        in_specs=in_specs,
        out_specs=pl.BlockSpec((1, TQ, d), lambda bi, qi: (bi, qi, 0)),
        out_shape=jax.ShapeDtypeStruct((b, s, d), F32),
        scratch_shapes=[pltpu.VMEM((HEADS + 1, 2 * LANES, TQ), BF16),
                        pltpu.VMEM((HEADS + 1, 1, TQ), F32),
                        pltpu.VMEM((HEADS + 1, V_ROWS, TQ), F32)],
        compiler_params=pltpu.CompilerParams(
            dimension_semantics=("arbitrary", "arbitrary"),
            vmem_limit_bytes=VMEM_LIMIT),
        name="attn_running_max" if running_max else "attn",
    )(cend, bound, qt, cq, k, kx, vt, za, sga, mb, x, gate3, wa, wo)


def _layer(x, c, w_ada, b_ada, norm_g, w_in, b_f, q_norm_g, k_norm_g, conv_w,
           w_attn_out, w_conv_out, w_o):
    b, s, d = x.shape
    ada = _ada(c, w_ada, b_ada)
    ada3 = ada.reshape(b, 1, 3 * d)
    gate3 = ada3[:, :, 2 * d:]

    sizes = (ATTN_W, ATTN_W, ATTN_W, HEADS, ATTN_W, CONV_W, CONV_W, CONV_W, CONV_W, d, d)
    offs = [0]
    for n in sizes:
        offs.append(offs[-1] + n)
    col = lambda i: w_in[:, offs[i]:offs[i + 1]]
    w_q, w_k, w_v, w_f, w_za, w_gb, w_gc, w_u, w_zb, w_ga, w_gb2 = (col(i) for i in range(11))

    def hi_lo_lanes(v):
        rep = jnp.concatenate([v] * CUM_PARTS, axis=-1)
        gap = jnp.zeros(v.shape[:-1] + (LO_LANE - ONES_LANE,), v.dtype)
        tail = jnp.zeros(v.shape[:-1] + (LANES - LO_LANE - ONES_LANE,), v.dtype)
        return jnp.concatenate([rep, gap, rep, tail], axis=-1)

    wn = jnp.concatenate([w_k, hi_lo_lanes(w_f), w_gb, w_gc, w_u, w_zb, w_ga, w_gb2],
                         axis=1).astype(BF16)
    wt = jnp.concatenate([w_q, w_v, w_za, w_f, w_f], axis=1).astype(BF16).T

    bfn = hi_lo_lanes(b_f.reshape(1, HEADS))
    bft = jnp.broadcast_to(jnp.concatenate([b_f, b_f])[:, None], (2 * HEADS, TM))
    gq = (jnp.tile(q_norm_g, HEADS) * HEAD_DIM ** -0.5).reshape(ATTN_W, 1)
    gk = jnp.tile(k_norm_g, HEADS).reshape(1, ATTN_W)

    head_of = np.arange(ATTN_W) // HEAD_DIM
    pm = jnp.asarray(np.where(head_of[:, None] == head_of[None, :], 1.0 / HEAD_DIM, 0.0), BF16)
    tok = np.arange(TM)
    lower = (tok[None, :] <= tok[:, None]).astype(np.float32)
    lt = jnp.asarray(lower, BF16)
    ut = jnp.asarray(np.concatenate([lower.T, np.ones((TM, LANES), np.float32)], axis=1), BF16)

    k, kx, qt, cq, vt, za, sga, mb = _proj(
        x, ada3, norm_g.reshape(1, d), wn, wt, bfn, bft, gq, gk, conv_w,
        w_conv_out.astype(BF16), pm, lt, ut)
    logit_bound = HEAD_DIM ** 0.5 * jnp.max(jnp.abs(q_norm_g)) * jnp.max(jnp.abs(k_norm_g))
    cend = cq[:, :, TK - 1::TK].reshape(-1)
    args = (cend, logit_bound.reshape(1), qt, cq, k, kx, vt, za, sga, mb, x, gate3,
            w_attn_out.astype(BF16), w_o.astype(BF16))
    return lax.cond(logit_bound <= MAX_RAW_LOGIT,
                    functools.partial(_attn, False), functools.partial(_attn, True), *args)


@jax.jit
def kernel(x, c, w_ada, b_ada, norm_g, w_in, b_f, q_norm_g, k_norm_g, conv_w,
           w_attn_out, w_conv_out, w_o):
    for i in range(w_ada.shape[0]):
        x = _layer(x, c, w_ada[i], b_ada[i], norm_g[i], w_in[i], b_f[i], q_norm_g[i],
                   k_norm_g[i], conv_w[i], w_attn_out[i], w_conv_out[i], w_o[i])
    return x
```

```python
import functools

import jax
import jax.numpy as jnp
import numpy as np
from jax import lax
from jax.experimental import pallas as pl
from jax.experimental.pallas import tpu as pltpu

D_MODEL = 1024
HEADS = 8
HEAD_DIM = 64
ATTN_W = HEADS * HEAD_DIM
CONV_W = 512
CONV_K = 3
EPS = 1e-6

LANES = 128
TM = 512
TQ = 512
TK = 256
CUM_PARTS = 3
ONES_LANE = CUM_PARTS * HEADS
LO_LANE = 32
V_ROWS = HEAD_DIM + 16
SCORE_LOOKAHEAD = 4
UNITS_PER_ITER = 8
NEG_BIG = -1e30
MAX_RAW_LOGIT = 40.0
SKIP_LOG_WEIGHT = 30.0
LOGIT_BOUND_MARGIN = 1.05
VMEM_LIMIT = 56 * 1024 * 1024

F32 = jnp.float32
BF16 = jnp.bfloat16


def _log_sigmoid(x):
    return jnp.minimum(x, 0.0) - jnp.log(1.0 + jnp.exp(-jnp.abs(x)))


def _sigmoid(x):
    return 0.5 * jnp.tanh(0.5 * x) + 0.5


def _silu(x):
    hx = 0.5 * x
    return hx * jnp.tanh(hx) + hx


def _dot(a, b):
    return jnp.dot(a, b, preferred_element_type=F32)


def _dot_nt(a, b):
    return lax.dot_general(a, b, (((1,), (1,)), ((), ())), preferred_element_type=F32)


def _dot_tn(a, b):
    return lax.dot_general(a, b, (((0,), (0,)), ((), ())), preferred_element_type=F32)


def _ada_kernel(c_ref, w_ref, b_ref, o_ref):
    o_ref[...] = jnp.dot(c_ref[...], w_ref[...], preferred_element_type=F32,
                         precision=lax.Precision.HIGHEST) + b_ref[...]


def _ada(c, w_ada, b_ada):
    b, d = c.shape
    n = w_ada.shape[1]
    return pl.pallas_call(
        _ada_kernel,
        grid=(n // d,),
        in_specs=[pl.BlockSpec((b, d), lambda j: (0, 0)),
                  pl.BlockSpec((d, d), lambda j: (0, j)),
                  pl.BlockSpec((1, d), lambda j: (0, j))],
        out_specs=pl.BlockSpec((b, d), lambda j: (0, j)),
        out_shape=jax.ShapeDtypeStruct((b, n), F32),
        name="ada",
    )(c, w_ada, b_ada.reshape(1, n))


N_K = 0
N_F = N_K + ATTN_W
N_GB = N_F + LANES
N_GC = N_GB + CONV_W
N_U = N_GC + CONV_W
N_ZB = N_U + CONV_W
N_GA = N_ZB + CONV_W
N_GB2 = N_GA + D_MODEL
N_END = N_GB2 + D_MODEL
T_Q = 0
T_V = T_Q + ATTN_W
T_ZA = T_V + ATTN_W
T_F = T_ZA + ATTN_W
T_END = T_F + 2 * HEADS
T_PAD = -(-T_END // LANES) * LANES


def _proj_kernel(x_ref, xnext_ref, ada_ref, ng_ref, wn_ref, wtn_ref, bfn_ref, bft_ref, gq_ref, gk_ref,
                 cw_ref, wb_ref, pm_ref, lt_ref, ut_ref,
                 k_ref, kx_ref, qt_ref, cq_ref, vt_ref, za_ref, sga_ref, mb_ref,
                 cn_ref, ct_ref, cu_ref, h_ref, gc_ref, wt_ref):
    s = pl.program_id(1)

    @pl.when((pl.program_id(0) == 0) & (s == 0))
    def _():
        wt_ref[...] = wtn_ref[...].T
    slot = lax.rem(s, 2)

    def modulated_norm(xr):
        x = xr[0]
        shift = ada_ref[0, :, 0:D_MODEL]
        scale = ada_ref[0, :, D_MODEL:2 * D_MODEL]
        xn = x * lax.rsqrt(jnp.mean(x * x, axis=-1, keepdims=True) + EPS)
        return (xn * ng_ref[...] * (1.0 + scale) + shift).astype(BF16)

    @pl.when(s == 0)
    def _():
        cn_ref[...] = jnp.zeros_like(cn_ref)
        ct_ref[...] = jnp.zeros_like(ct_ref)
        cu_ref[0:8, :] = jnp.zeros((8, CONV_W), F32)
        h0 = modulated_norm(x_ref)
        h_ref[0] = h0
        gc_ref[...] = _dot(h0, wn_ref[:, N_GC:N_U])

    nat = lambda lo, hi: _dot(h_ref[slot], wn_ref[:, lo:hi])
    trn = lambda lo, hi: _dot_nt(wt_ref[lo:hi, :], h_ref[slot])
    lane = lax.broadcasted_iota(jnp.int32, (TM, LANES), 1)
    in_hi = lane < ONES_LANE
    in_lo = (lane >= LO_LANE) & (lane < LO_LANE + ONES_LANE)

    gc = gc_ref[...]
    u = nat(N_U, N_ZB)
    kf = nat(N_K, N_GB)
    kraw = kf[:, 0:ATTN_W]
    flog = kf[:, ATTN_W:]

    cu = gc * u
    cu_ref[8:8 + TM, :] = cu
    conv = (cw_ref[2:3, :] * cu + cw_ref[1:2, :] * cu_ref[7:7 + TM, :]
            + cw_ref[0:1, :] * cu_ref[6:6 + TM, :])
    cu_ref[0:8, :] = cu[TM - 8:TM, :]
    gb = nat(N_GB, N_GC)
    zb = nat(N_ZB, N_GA)

    kk = (kraw * kraw).astype(BF16)
    lf = _log_sigmoid(flog + bfn_ref[...])
    lf_lo = lf - lf.astype(BF16).astype(F32)
    parts = jnp.where(in_hi, lf, jnp.where(in_lo, lf_lo, 0.0)).astype(BF16)
    kms = _dot(kk, pm_ref[...])
    csum = _dot(lt_ref[...], parts)
    ga = nat(N_GA, N_GB2)

    gb2 = nat(N_GB2, N_END)
    ob = (gb * conv * _silu(zb)).astype(BF16)
    mbr = _dot(ob, wb_ref[...])

    sga_ref[0] = _sigmoid(ga).astype(BF16)
    kn = (kraw * lax.rsqrt(kms + EPS) * gk_ref[...]).astype(BF16)
    for pair in range(HEADS // 2):
        k_ref[0, pair] = kn[:, pair * LANES:(pair + 1) * LANES]
    cum = csum + pltpu.roll(csum, LANES - LO_LANE, 1) + cn_ref[...]
    cn_ref[...] = cum[TM - 1:TM, :]
    r1 = cum - cum.astype(BF16).astype(F32)
    r2 = r1 - r1.astype(BF16).astype(F32)
    piece = jnp.where(lane < HEADS, cum, jnp.where(lane < 2 * HEADS, r1, r2))
    is_one = (lane >= ONES_LANE) & (lane < ONES_LANE + CUM_PARTS)
    kx_ref[0] = jnp.where(in_hi, -piece, jnp.where(is_one, 1.0, 0.0)).astype(BF16)
    zf = trn(T_ZA, T_END)
    qraw = trn(T_Q, T_V)

    mb_ref[0] = (_sigmoid(gb2) * mbr).astype(BF16)
    lft = _log_sigmoid(zf[ATTN_W:, :] + bft_ref[...])
    lft_lo = lft - lft.astype(BF16).astype(F32)
    row = lax.broadcasted_iota(jnp.int32, (2 * HEADS, TM), 0)
    ct = _dot(jnp.where(row < HEADS, lft, lft_lo).astype(BF16), ut_ref[...])

    za_ref[0] = _silu(zf[0:ATTN_W, :]).astype(BF16)
    ct = ct[0:HEADS, :] + ct[HEADS:2 * HEADS, :]
    carry = ct_ref[...]
    cq_ref[0] = ct[:, 0:TM] + jnp.concatenate([carry] * (TM // LANES), axis=1)
    ct_ref[...] = carry + ct[:, TM:TM + LANES]
    vraw = trn(T_V, T_ZA)

    q3 = qraw.reshape(HEADS, HEAD_DIM, TM)
    qn = q3 * lax.rsqrt(jnp.mean(q3 * q3, axis=1, keepdims=True) + EPS)
    qt_ref[0] = (qn.reshape(ATTN_W, TM) * gq_ref[...]).astype(BF16)

    vt = vraw.astype(BF16)
    for i in range(TM // TK):
        for hd in range(HEADS):
            vt_ref[0, i, hd, 0:HEAD_DIM, :] = (
                vt[hd * HEAD_DIM:(hd + 1) * HEAD_DIM, i * TK:(i + 1) * TK])
            vt_ref[0, i, hd, HEAD_DIM:V_ROWS, :] = jnp.ones((V_ROWS - HEAD_DIM, TK), BF16)

    h_next = modulated_norm(xnext_ref)
    h_ref[1 - slot] = h_next
    gc_ref[...] = _dot(h_next, wn_ref[:, N_GC:N_U])


def _proj(x, ada3, ng, wn, wt, bfn, bft, gq, gk, cw, wb, pm, lt, ut):
    b, s, d = x.shape
    const = lambda shape: pl.BlockSpec(shape, lambda bi, si: (0,) * len(shape))
    out_shape = [
        jax.ShapeDtypeStruct((b, HEADS // 2, s, LANES), BF16),
        jax.ShapeDtypeStruct((b, s, LANES), BF16),
        jax.ShapeDtypeStruct((b, ATTN_W, s), BF16),
        jax.ShapeDtypeStruct((b, HEADS, s), F32),
        jax.ShapeDtypeStruct((b, s // TK, HEADS, V_ROWS, TK), BF16),
        jax.ShapeDtypeStruct((b, ATTN_W, s), BF16),
        jax.ShapeDtypeStruct((b, s, d), BF16),
        jax.ShapeDtypeStruct((b, s, d), BF16),
    ]
    out_specs = [
        pl.BlockSpec((1, HEADS // 2, TM, LANES), lambda bi, si: (bi, 0, si, 0)),
        pl.BlockSpec((1, TM, LANES), lambda bi, si: (bi, si, 0)),
        pl.BlockSpec((1, ATTN_W, TM), lambda bi, si: (bi, 0, si)),
        pl.BlockSpec((1, HEADS, TM), lambda bi, si: (bi, 0, si)),
        pl.BlockSpec((1, TM // TK, HEADS, V_ROWS, TK), lambda bi, si: (bi, si, 0, 0, 0)),
        pl.BlockSpec((1, ATTN_W, TM), lambda bi, si: (bi, 0, si)),
        pl.BlockSpec((1, TM, d), lambda bi, si: (bi, si, 0)),
        pl.BlockSpec((1, TM, d), lambda bi, si: (bi, si, 0)),
    ]
    last = s // TM - 1
    in_specs = [
        pl.BlockSpec((1, TM, d), lambda bi, si: (bi, si, 0)),
        pl.BlockSpec((1, TM, d), lambda bi, si: (bi, jnp.minimum(si + 1, last), 0)),
        pl.BlockSpec((1, 1, 3 * d), lambda bi, si: (bi, 0, 0)),
        const(ng.shape), const(wn.shape), const(wt.shape), const(bfn.shape), const(bft.shape),
        const(gq.shape), const(gk.shape), const(cw.shape), const(wb.shape), const(pm.shape),
        const(lt.shape), const(ut.shape),
    ]
    return pl.pallas_call(
        _proj_kernel,
        grid=(b, s // TM),
        in_specs=in_specs,
        out_specs=out_specs,
        out_shape=out_shape,
        scratch_shapes=[pltpu.VMEM((1, LANES), F32),
                        pltpu.VMEM((HEADS, LANES), F32),
                        pltpu.VMEM((TM + 8, CONV_W), F32),
                        pltpu.VMEM((2, TM, D_MODEL), BF16),
                        pltpu.VMEM((TM, CONV_W), F32),
                        pltpu.VMEM((T_PAD, D_MODEL), BF16)],
        compiler_params=pltpu.CompilerParams(
            dimension_semantics=("arbitrary", "arbitrary"),
            vmem_limit_bytes=VMEM_LIMIT),
        name="proj",
    )(x, x, ada3, ng, wn, wt, bfn, bft, gq, gk, cw, wb, pm, lt, ut)


def _attn_kernel(cend_ref, bound_ref, qt_ref, cq_ref, k_ref, kx_ref, vt_ref, za_ref, sga_ref, mb_ref,
                 x_ref, gate_ref, wa_ref, wo_ref, o_ref, rhs_ref, m_ref, acc_ref, *, running_max):
    bi = pl.program_id(0)
    qi = pl.program_id(1)
    blocks_per_tile = TQ // TK
    n_blocks = kx_ref.shape[1] // TK
    i32 = jnp.int32

    last_before = qi * blocks_per_tile - 1
    slack = LOGIT_BOUND_MARGIN * 2.0 * bound_ref[0] + SKIP_LOG_WEIGHT

    first = []
    for h in range(HEADS):
        base = (bi * HEADS + h) * n_blocks
        f_tile = cend_ref[base + jnp.maximum(last_before, 0)]
        skipped = i32(0)
        for j in range(n_blocks - blocks_per_tile - 1):
            ok = (j < last_before) & (f_tile - cend_ref[base + j] + slack <= 0.0)
            skipped = skipped + ok.astype(i32)
        first.append(skipped)
    upto, shift, total = [], [], i32(0)
    for h in range(HEADS):
        shift.append(first[h] - total)
        total = total + (qi * blocks_per_tile - first[h])
        upto.append(total)

    row = lax.broadcasted_iota(jnp.int32, (LANES, TQ), 0)
    for h in range(HEADS):
        pair, half = divmod(h, 2)
        qp = qt_ref[0, pair * LANES:(pair + 1) * LANES, :]
        mine = (row >= half * HEAD_DIM) & (row < (half + 1) * HEAD_DIM)
        rhs_ref[h, 0:LANES, :] = qp * jnp.where(mine, 1.0, 0.0).astype(BF16)
        cq = cq_ref[0, h:h + 1, :]
        r1 = cq - cq.astype(BF16).astype(F32)
        r2 = r1 - r1.astype(BF16).astype(F32)
        sel = (row == h) | (row == HEADS + h) | (row == 2 * HEADS + h)
        f = jnp.where(row == ONES_LANE, cq,
                      jnp.where(row == ONES_LANE + 1, r1,
                                jnp.where(row == ONES_LANE + 2, r2, 0.0)))
        rhs_ref[h, LANES:2 * LANES, :] = jnp.where(sel, 1.0, f).astype(BF16)
    rhs_ref[HEADS] = jnp.zeros(rhs_ref.shape[1:], BF16)

    if running_max:
        m_ref[...] = jnp.full(m_ref.shape, NEG_BIG, F32)
    acc_ref[...] = jnp.zeros_like(acc_ref)

    def scores(j, h, q_lo):
        start = pl.multiple_of(j * TK, TK)
        pair = h // 2 if isinstance(h, int) else jnp.minimum(h, HEADS - 1) // 2
        lhs = jnp.concatenate([k_ref[0, pair, pl.ds(start, TK), :],
                               kx_ref[0, pl.ds(start, TK), :]], axis=1)
        return _dot(lhs, rhs_ref[h, :, q_lo:TQ])

    def run_units(units):
        ahead = [scores(j, h, q_lo) for j, h, _, q_lo in units[:SCORE_LOOKAHEAD]]
        for i, (j, h, masked, q_lo) in enumerate(units):
            sc = ahead.pop(0)
            if i + SCORE_LOOKAHEAD < len(units):
                jn, hn, _, qn = units[i + SCORE_LOOKAHEAD]
                ahead.append(scores(jn, hn, qn))
            if masked:
                kpos = lax.broadcasted_iota(jnp.int32, sc.shape, 0)
                qpos = lax.broadcasted_iota(jnp.int32, sc.shape, 1)
                sc = jnp.where(kpos <= qpos, sc, NEG_BIG)
            hv = h if isinstance(h, int) else jnp.minimum(h, HEADS - 1)
            if running_max:
                m_prev = m_ref[h, :, q_lo:TQ]
                m_new = jnp.maximum(m_prev, jnp.max(sc, axis=0, keepdims=True))
                m_ref[h, :, q_lo:TQ] = m_new
                p = jnp.exp(sc - m_new).astype(BF16)
                acc_ref[h, :, q_lo:TQ] = (jnp.exp(m_prev - m_new) * acc_ref[h, :, q_lo:TQ]
                                          + _dot(vt_ref[0, j, hv], p))
            else:
                p = jnp.exp(sc).astype(BF16)
                acc_ref[h, :, q_lo:TQ] += _dot(vt_ref[0, j, hv], p)

    def body(i, carry):
        units = []
        for t in range(UNITS_PER_ITER):
            u = i * UNITS_PER_ITER + t
            h = i32(0)
            off = shift[0]
            for g in range(HEADS):
                past = u >= upto[g]
                h = h + past.astype(i32)
                if g + 1 < HEADS:
                    off = jnp.where(past, shift[g + 1], off)
            j = jnp.where(u >= total, 0, u + off)
            units.append((j, h, False, 0))
        run_units(units)
        return carry

    lax.fori_loop(0, (total + UNITS_PER_ITER - 1) // UNITS_PER_ITER, body, 0)
    run_units([(qi * blocks_per_tile + t, h, True, t * TK)
               for t in range(blocks_per_tile) for h in range(HEADS)])

    acc = jnp.concatenate(
        [acc_ref[h, 0:HEAD_DIM, :] * (1.0 / acc_ref[h, HEAD_DIM:HEAD_DIM + 1, :])
         for h in range(HEADS)], axis=0)
    at = (acc * za_ref[0].astype(F32)).astype(BF16)
    ya = _dot_tn(at, wa_ref[...])
    merged = (sga_ref[0].astype(F32) * ya + mb_ref[0].astype(F32)).astype(BF16)
    o_ref[0] = x_ref[0] + gate_ref[0] * _dot(merged, wo_ref[...])


def _attn(running_max, cend, bound, qt, cq, k, kx, vt, za, sga, mb, x, gate3, wa, wo):
    b, s, d = x.shape
    const = lambda shape: pl.BlockSpec(shape, lambda bi, qi: (0,) * len(shape))
    in_specs = [
        pl.BlockSpec(memory_space=pltpu.SMEM),
        pl.BlockSpec(memory_space=pltpu.SMEM),
        pl.BlockSpec((1, ATTN_W, TQ), lambda bi, qi: (bi, 0, qi)),
        pl.BlockSpec((1, HEADS, TQ), lambda bi, qi: (bi, 0, qi)),
        pl.BlockSpec((1, HEADS // 2, s, LANES), lambda bi, qi: (bi, 0, 0, 0)),
        pl.BlockSpec((1, s, LANES), lambda bi, qi: (bi, 0, 0)),
        pl.BlockSpec((1, s // TK, HEADS, V_ROWS, TK), lambda bi, qi: (bi, 0, 0, 0, 0)),
        pl.BlockSpec((1, ATTN_W, TQ), lambda bi, qi: (bi, 0, qi)),
        pl.BlockSpec((1, TQ, d), lambda bi, qi: (bi, qi, 0)),
        pl.BlockSpec((1, TQ, d), lambda bi, qi: (bi, qi, 0)),
        pl.BlockSpec((1, TQ, d), lambda bi, qi: (bi, qi, 0)),
        pl.BlockSpec((1, 1, d), lambda bi, qi: (bi, 0, 0)),
        const(wa.shape), const(wo.shape),
    ]
    return pl.pallas_call(
        functools.partial(_attn_kernel, running_max=running_max),
        grid=(b, s // TQ),
        in_specs=in_specs,
        out_specs=pl.BlockSpec((1, TQ, d), lambda bi, qi: (bi, qi, 0)),
        out_shape=jax.ShapeDtypeStruct((b, s, d), F32),
        scratch_shapes=[pltpu.VMEM((HEADS + 1, 2 * LANES, TQ), BF16),
                        pltpu.VMEM((HEADS + 1, 1, TQ), F32),
                        pltpu.VMEM((HEADS + 1, V_ROWS, TQ), F32)],
        compiler_params=pltpu.CompilerParams(
            dimension_semantics=("arbitrary", "arbitrary"),
            vmem_limit_bytes=VMEM_LIMIT),
        name="attn_running_max" if running_max else "attn",
    )(cend, bound, qt, cq, k, kx, vt, za, sga, mb, x, gate3, wa, wo)


def _layer(x, c, w_ada, b_ada, norm_g, w_in, b_f, q_norm_g, k_norm_g, conv_w,
           w_attn_out, w_conv_out, w_o):
    b, s, d = x.shape
    ada = _ada(c, w_ada, b_ada)
    ada3 = ada.reshape(b, 1, 3 * d)
    gate3 = ada3[:, :, 2 * d:]

    sizes = (ATTN_W, ATTN_W, ATTN_W, HEADS, ATTN_W, CONV_W, CONV_W, CONV_W, CONV_W, d, d)
    offs = [0]
    for n in sizes:
        offs.append(offs[-1] + n)
    col = lambda i: w_in[:, offs[i]:offs[i + 1]]
    w_q, w_k, w_v, w_f, w_za, w_gb, w_gc, w_u, w_zb, w_ga, w_gb2 = (col(i) for i in range(11))

    def hi_lo_lanes(v):
        rep = jnp.concatenate([v] * CUM_PARTS, axis=-1)
        gap = jnp.zeros(v.shape[:-1] + (LO_LANE - ONES_LANE,), v.dtype)
        tail = jnp.zeros(v.shape[:-1] + (LANES - LO_LANE - ONES_LANE,), v.dtype)
        return jnp.concatenate([rep, gap, rep, tail], axis=-1)

    wn = jnp.concatenate([w_k, hi_lo_lanes(w_f), w_in[:, offs[5]:]], axis=1).astype(BF16)
    wt = jnp.concatenate([w_q, w_v, w_za, w_f, w_f, jnp.zeros((d, T_PAD - T_END), F32)],
                         axis=1).astype(BF16)

    bfn = hi_lo_lanes(b_f.reshape(1, HEADS))
    bft = jnp.broadcast_to(jnp.concatenate([b_f, b_f])[:, None], (2 * HEADS, TM))
    gq = (jnp.tile(q_norm_g, HEADS) * HEAD_DIM ** -0.5).reshape(ATTN_W, 1)
    gk = jnp.tile(k_norm_g, HEADS).reshape(1, ATTN_W)

    head_of = np.arange(ATTN_W) // HEAD_DIM
    pm = jnp.asarray(np.where(head_of[:, None] == head_of[None, :], 1.0 / HEAD_DIM, 0.0), BF16)
    tok = np.arange(TM)
    lower = (tok[None, :] <= tok[:, None]).astype(np.float32)
    lt = jnp.asarray(lower, BF16)
    ut = jnp.asarray(np.concatenate([lower.T, np.ones((TM, LANES), np.float32)], axis=1), BF16)

    k, kx, qt, cq, vt, za, sga, mb = _proj(
        x, ada3, norm_g.reshape(1, d), wn, wt, bfn, bft, gq, gk, conv_w,
        w_conv_out.astype(BF16), pm, lt, ut)
    logit_bound = HEAD_DIM ** 0.5 * jnp.max(jnp.abs(q_norm_g)) * jnp.max(jnp.abs(k_norm_g))
    cend = cq[:, :, TK - 1::TK].reshape(-1)
    args = (cend, logit_bound.reshape(1), qt, cq, k, kx, vt, za, sga, mb, x, gate3,
            w_attn_out.astype(BF16), w_o.astype(BF16))
    return lax.cond(logit_bound <= MAX_RAW_LOGIT,
                    functools.partial(_attn, False), functools.partial(_attn, True), *args)


@jax.jit
def kernel(x, c, w_ada, b_ada, norm_g, w_in, b_f, q_norm_g, k_norm_g, conv_w,
           w_attn_out, w_conv_out, w_o):
    for i in range(w_ada.shape[0]):
        x = _layer(x, c, w_ada[i], b_ada[i], norm_g[i], w_in[i], b_f[i], q_norm_g[i],
                   k_norm_g[i], conv_w[i], w_attn_out[i], w_conv_out[i], w_o[i])
    return x
```

```python
import functools

import jax
import jax.numpy as jnp
import numpy as np
from jax import lax
from jax.experimental import pallas as pl
from jax.experimental.pallas import tpu as pltpu

D_MODEL = 1024
HEADS = 8
HEAD_DIM = 64
ATTN_W = HEADS * HEAD_DIM
CONV_W = 512
CONV_K = 3
EPS = 1e-6

LANES = 128
TM = 512
TQ = 512
TK = 256
CUM_PARTS = 3
ONES_LANE = CUM_PARTS * HEADS
LO_LANE = 32
V_ROWS = HEAD_DIM + 16
SCORE_LOOKAHEAD = 4
UNITS_PER_ITER = 8
NEG_BIG = -1e30
MAX_RAW_LOGIT = 40.0
SKIP_LOG_WEIGHT = 30.0
LOGIT_BOUND_MARGIN = 1.05
VMEM_LIMIT = 56 * 1024 * 1024

F32 = jnp.float32
BF16 = jnp.bfloat16


def _log_sigmoid(x):
    return jnp.minimum(x, 0.0) - jnp.log(1.0 + jnp.exp(-jnp.abs(x)))


def _sigmoid(x):
    return 0.5 * jnp.tanh(0.5 * x) + 0.5


def _silu(x):
    hx = 0.5 * x
    return hx * jnp.tanh(hx) + hx


def _dot(a, b):
    return jnp.dot(a, b, preferred_element_type=F32)


def _dot_nt(a, b):
    return lax.dot_general(a, b, (((1,), (1,)), ((), ())), preferred_element_type=F32)


def _dot_tn(a, b):
    return lax.dot_general(a, b, (((0,), (0,)), ((), ())), preferred_element_type=F32)


def _ada_kernel(c_ref, w_ref, b_ref, o_ref):
    o_ref[...] = jnp.dot(c_ref[...], w_ref[...], preferred_element_type=F32,
                         precision=lax.Precision.HIGHEST) + b_ref[...]


def _ada(c, w_ada, b_ada):
    b, d = c.shape
    n = w_ada.shape[1]
    return pl.pallas_call(
        _ada_kernel,
        grid=(n // d,),
        in_specs=[pl.BlockSpec((b, d), lambda j: (0, 0)),
                  pl.BlockSpec((d, d), lambda j: (0, j)),
                  pl.BlockSpec((1, d), lambda j: (0, j))],
        out_specs=pl.BlockSpec((b, d), lambda j: (0, j)),
        out_shape=jax.ShapeDtypeStruct((b, n), F32),
        name="ada",
    )(c, w_ada, b_ada.reshape(1, n))


N_K = 0
N_F = N_K + ATTN_W
N_GB = N_F + LANES
N_GC = N_GB + CONV_W
N_U = N_GC + CONV_W
N_ZB = N_U + CONV_W
N_GA = N_ZB + CONV_W
N_GB2 = N_GA + D_MODEL
N_END = N_GB2 + D_MODEL
T_Q = 0
T_V = T_Q + ATTN_W
T_ZA = T_V + ATTN_W
T_F = T_ZA + ATTN_W
T_END = T_F + 2 * HEADS
T_PAD = -(-T_END // LANES) * LANES

IN_SIZES = (ATTN_W, ATTN_W, ATTN_W, HEADS, ATTN_W, CONV_W, CONV_W, CONV_W, CONV_W, D_MODEL, D_MODEL)
IN_OFFS = tuple(sum(IN_SIZES[:i]) for i in range(len(IN_SIZES) + 1))
PREP_ROWS = 128


def _prep_kernel(w_ref, fn_ref, ft_ref, wn_ref, wtn_ref):
    col = lambda i, j: w_ref[:, IN_OFFS[i]:IN_OFFS[j]].astype(BF16)
    wn_ref[:, N_K:N_F] = col(1, 2)
    wn_ref[:, N_F:N_GB] = fn_ref[...].astype(BF16)
    wn_ref[:, N_GB:N_END] = col(5, 11)
    wtn_ref[:, T_Q:T_V] = col(0, 1)
    wtn_ref[:, T_V:T_ZA] = col(2, 3)
    wtn_ref[:, T_ZA:T_F] = col(4, 5)
    wtn_ref[:, T_F:T_PAD] = ft_ref[...].astype(BF16)


def _prep(w_in, fn, ft):
    d, n = w_in.shape
    return pl.pallas_call(
        _prep_kernel,
        grid=(d // PREP_ROWS,),
        in_specs=[pl.BlockSpec((PREP_ROWS, n), lambda i: (i, 0)),
                  pl.BlockSpec((PREP_ROWS, LANES), lambda i: (i, 0)),
                  pl.BlockSpec((PREP_ROWS, T_PAD - T_F), lambda i: (i, 0))],
        out_specs=[pl.BlockSpec((PREP_ROWS, N_END), lambda i: (i, 0)),
                   pl.BlockSpec((PREP_ROWS, T_PAD), lambda i: (i, 0))],
        out_shape=[jax.ShapeDtypeStruct((d, N_END), BF16),
                   jax.ShapeDtypeStruct((d, T_PAD), BF16)],
        name="prep",
    )(w_in, fn, ft)


def _proj_kernel(x_ref, xnext_ref, ada_ref, ng_ref, wn_ref, wtn_ref, bfn_ref, bft_ref, gq_ref, gk_ref,
                 cw_ref, wb_ref, pm_ref, lt_ref, ut_ref,
                 k_ref, kx_ref, qt_ref, cq_ref, vt_ref, za_ref, sga_ref, mb_ref,
                 cn_ref, ct_ref, cu_ref, h_ref, gc_ref, wt_ref):
    s = pl.program_id(1)

    @pl.when((pl.program_id(0) == 0) & (s == 0))
    def _():
        wt_ref[...] = wtn_ref[...].T
    slot = lax.rem(s, 2)

    def modulated_norm(xr):
        x = xr[0]
        shift = ada_ref[0, :, 0:D_MODEL]
        scale = ada_ref[0, :, D_MODEL:2 * D_MODEL]
        xn = x * lax.rsqrt(jnp.mean(x * x, axis=-1, keepdims=True) + EPS)
        return (xn * ng_ref[...] * (1.0 + scale) + shift).astype(BF16)

    @pl.when(s == 0)
    def _():
        cn_ref[...] = jnp.zeros_like(cn_ref)
        ct_ref[...] = jnp.zeros_like(ct_ref)
        cu_ref[0:8, :] = jnp.zeros((8, CONV_W), F32)
        h0 = modulated_norm(x_ref)
        h_ref[0] = h0
        gc_ref[...] = _dot(h0, wn_ref[:, N_GC:N_U])

    nat = lambda lo, hi: _dot(h_ref[slot], wn_ref[:, lo:hi])
    trn = lambda lo, hi: _dot_nt(wt_ref[lo:hi, :], h_ref[slot])
    lane = lax.broadcasted_iota(jnp.int32, (TM, LANES), 1)
    in_hi = lane < ONES_LANE
    in_lo = (lane >= LO_LANE) & (lane < LO_LANE + ONES_LANE)

    gc = gc_ref[...]
    u = nat(N_U, N_ZB)
    kf = nat(N_K, N_GB)
    kraw = kf[:, 0:ATTN_W]
    flog = kf[:, ATTN_W:]

    cu = gc * u
    cu_ref[8:8 + TM, :] = cu
    conv = (cw_ref[2:3, :] * cu + cw_ref[1:2, :] * cu_ref[7:7 + TM, :]
            + cw_ref[0:1, :] * cu_ref[6:6 + TM, :])
    cu_ref[0:8, :] = cu[TM - 8:TM, :]
    gb = nat(N_GB, N_GC)
    zb = nat(N_ZB, N_GA)

    kk = (kraw * kraw).astype(BF16)
    lf = _log_sigmoid(flog + bfn_ref[...])
    lf_lo = lf - lf.astype(BF16).astype(F32)
    parts = jnp.where(in_hi, lf, jnp.where(in_lo, lf_lo, 0.0)).astype(BF16)
    kms = _dot(kk, pm_ref[...])
    csum = _dot(lt_ref[...], parts)
    ga = nat(N_GA, N_GB2)

    gb2 = nat(N_GB2, N_END)
    ob = (gb * conv * _silu(zb)).astype(BF16)
    mbr = _dot(ob, wb_ref[...])

    sga_ref[0] = _sigmoid(ga).astype(BF16)
    kn = (kraw * lax.rsqrt(kms + EPS) * gk_ref[...]).astype(BF16)
    for pair in range(HEADS // 2):
        k_ref[0, pair] = kn[:, pair * LANES:(pair + 1) * LANES]
    cum = csum + pltpu.roll(csum, LANES - LO_LANE, 1) + cn_ref[...]
    cn_ref[...] = cum[TM - 1:TM, :]
    r1 = cum - cum.astype(BF16).astype(F32)
    r2 = r1 - r1.astype(BF16).astype(F32)
    piece = jnp.where(lane < HEADS, cum, jnp.where(lane < 2 * HEADS, r1, r2))
    is_one = (lane >= ONES_LANE) & (lane < ONES_LANE + CUM_PARTS)
    kx_ref[0] = jnp.where(in_hi, -piece, jnp.where(is_one, 1.0, 0.0)).astype(BF16)
    zf = trn(T_ZA, T_END)
    qraw = trn(T_Q, T_V)

    mb_ref[0] = (_sigmoid(gb2) * mbr).astype(BF16)
    lft = _log_sigmoid(zf[ATTN_W:, :] + bft_ref[...])
    lft_lo = lft - lft.astype(BF16).astype(F32)
    row = lax.broadcasted_iota(jnp.int32, (2 * HEADS, TM), 0)
    ct = _dot(jnp.where(row < HEADS, lft, lft_lo).astype(BF16), ut_ref[...])

    za_ref[0] = _silu(zf[0:ATTN_W, :]).astype(BF16)
    ct = ct[0:HEADS, :] + ct[HEADS:2 * HEADS, :]
    carry = ct_ref[...]
    cq_ref[0] = ct[:, 0:TM] + jnp.concatenate([carry] * (TM // LANES), axis=1)
    ct_ref[...] = carry + ct[:, TM:TM + LANES]
    vraw = trn(T_V, T_ZA)

    q3 = qraw.reshape(HEADS, HEAD_DIM, TM)
    qn = q3 * lax.rsqrt(jnp.mean(q3 * q3, axis=1, keepdims=True) + EPS)
    qt_ref[0] = (qn.reshape(ATTN_W, TM) * gq_ref[...]).astype(BF16)

    vt = vraw.astype(BF16)
    for i in range(TM // TK):
        for hd in range(HEADS):
            vt_ref[0, i, hd, 0:HEAD_DIM, :] = (
                vt[hd * HEAD_DIM:(hd + 1) * HEAD_DIM, i * TK:(i + 1) * TK])
            vt_ref[0, i, hd, HEAD_DIM:V_ROWS, :] = jnp.ones((V_ROWS - HEAD_DIM, TK), BF16)

    h_next = modulated_norm(xnext_ref)
    h_ref[1 - slot] = h_next
    gc_ref[...] = _dot(h_next, wn_ref[:, N_GC:N_U])


def _proj(x, ada3, ng, wn, wt, bfn, bft, gq, gk, cw, wb, pm, lt, ut):
    b, s, d = x.shape
    const = lambda shape: pl.BlockSpec(shape, lambda bi, si: (0,) * len(shape))
    out_shape = [
        jax.ShapeDtypeStruct((b, HEADS // 2, s, LANES), BF16),
        jax.ShapeDtypeStruct((b, s, LANES), BF16),
        jax.ShapeDtypeStruct((b, ATTN_W, s), BF16),
        jax.ShapeDtypeStruct((b, HEADS, s), F32),
        jax.ShapeDtypeStruct((b, s // TK, HEADS, V_ROWS, TK), BF16),
        jax.ShapeDtypeStruct((b, ATTN_W, s), BF16),
        jax.ShapeDtypeStruct((b, s, d), BF16),
        jax.ShapeDtypeStruct((b, s, d), BF16),
    ]
    out_specs = [
        pl.BlockSpec((1, HEADS // 2, TM, LANES), lambda bi, si: (bi, 0, si, 0)),
        pl.BlockSpec((1, TM, LANES), lambda bi, si: (bi, si, 0)),
        pl.BlockSpec((1, ATTN_W, TM), lambda bi, si: (bi, 0, si)),
        pl.BlockSpec((1, HEADS, TM), lambda bi, si: (bi, 0, si)),
        pl.BlockSpec((1, TM // TK, HEADS, V_ROWS, TK), lambda bi, si: (bi, si, 0, 0, 0)),
        pl.BlockSpec((1, ATTN_W, TM), lambda bi, si: (bi, 0, si)),
        pl.BlockSpec((1, TM, d), lambda bi, si: (bi, si, 0)),
        pl.BlockSpec((1, TM, d), lambda bi, si: (bi, si, 0)),
    ]
    last = s // TM - 1
    in_specs = [
        pl.BlockSpec((1, TM, d), lambda bi, si: (bi, si, 0)),
        pl.BlockSpec((1, TM, d), lambda bi, si: (bi, jnp.minimum(si + 1, last), 0)),
        pl.BlockSpec((1, 1, 3 * d), lambda bi, si: (bi, 0, 0)),
        const(ng.shape), const(wn.shape), const(wt.shape), const(bfn.shape), const(bft.shape),
        const(gq.shape), const(gk.shape), const(cw.shape), const(wb.shape), const(pm.shape),
        const(lt.shape), const(ut.shape),
    ]
    return pl.pallas_call(
        _proj_kernel,
        grid=(b, s // TM),
        in_specs=in_specs,
        out_specs=out_specs,
        out_shape=out_shape,
        scratch_shapes=[pltpu.VMEM((1, LANES), F32),
                        pltpu.VMEM((HEADS, LANES), F32),
                        pltpu.VMEM((TM + 8, CONV_W), F32),
                        pltpu.VMEM((2, TM, D_MODEL), BF16),
                        pltpu.VMEM((TM, CONV_W), F32),
                        pltpu.VMEM((T_PAD, D_MODEL), BF16)],
        compiler_params=pltpu.CompilerParams(
            dimension_semantics=("arbitrary", "arbitrary"),
            vmem_limit_bytes=VMEM_LIMIT),
        name="proj",
    )(x, x, ada3, ng, wn, wt, bfn, bft, gq, gk, cw, wb, pm, lt, ut)


def _attn_kernel(cend_ref, bound_ref, qt_ref, cq_ref, k_ref, kx_ref, vt_ref, za_ref, sga_ref, mb_ref,
                 x_ref, gate_ref, wa_ref, wo_ref, o_ref, rhs_ref, m_ref, acc_ref, *, running_max):
    bi = pl.program_id(0)
    qi = pl.program_id(1)
    blocks_per_tile = TQ // TK
    n_blocks = kx_ref.shape[1] // TK
    i32 = jnp.int32

    last_before = qi * blocks_per_tile - 1
    slack = LOGIT_BOUND_MARGIN * 2.0 * bound_ref[0] + SKIP_LOG_WEIGHT

    first = []
    for h in range(HEADS):
        base = (bi * HEADS + h) * n_blocks
        f_tile = cend_ref[base + jnp.maximum(last_before, 0)]
        skipped = i32(0)
        for j in range(n_blocks - blocks_per_tile - 1):
            ok = (j < last_before) & (f_tile - cend_ref[base + j] + slack <= 0.0)
            skipped = skipped + ok.astype(i32)
        first.append(skipped)
    upto, shift, total = [], [], i32(0)
    for h in range(HEADS):
        shift.append(first[h] - total)
        total = total + (qi * blocks_per_tile - first[h])
        upto.append(total)

    row = lax.broadcasted_iota(jnp.int32, (LANES, TQ), 0)
    for h in range(HEADS):
        pair, half = divmod(h, 2)
        qp = qt_ref[0, pair * LANES:(pair + 1) * LANES, :]
        mine = (row >= half * HEAD_DIM) & (row < (half + 1) * HEAD_DIM)
        rhs_ref[h, 0:LANES, :] = qp * jnp.where(mine, 1.0, 0.0).astype(BF16)
        cq = cq_ref[0, h:h + 1, :]
        r1 = cq - cq.astype(BF16).astype(F32)
        r2 = r1 - r1.astype(BF16).astype(F32)
        sel = (row == h) | (row == HEADS + h) | (row == 2 * HEADS + h)
        f = jnp.where(row == ONES_LANE, cq,
                      jnp.where(row == ONES_LANE + 1, r1,
                                jnp.where(row == ONES_LANE + 2, r2, 0.0)))
        rhs_ref[h, LANES:2 * LANES, :] = jnp.where(sel, 1.0, f).astype(BF16)
    rhs_ref[HEADS] = jnp.zeros(rhs_ref.shape[1:], BF16)

    if running_max:
        m_ref[...] = jnp.full(m_ref.shape, NEG_BIG, F32)
    acc_ref[...] = jnp.zeros_like(acc_ref)

    def scores(j, h, q_lo):
        start = pl.multiple_of(j * TK, TK)
        pair = h // 2 if isinstance(h, int) else jnp.minimum(h, HEADS - 1) // 2
        lhs = jnp.concatenate([k_ref[0, pair, pl.ds(start, TK), :],
                               kx_ref[0, pl.ds(start, TK), :]], axis=1)
        return _dot(lhs, rhs_ref[h, :, q_lo:TQ])

    def run_units(units):
        ahead = [scores(j, h, q_lo) for j, h, _, q_lo in units[:SCORE_LOOKAHEAD]]
        for i, (j, h, masked, q_lo) in enumerate(units):
            sc = ahead.pop(0)
            if i + SCORE_LOOKAHEAD < len(units):
                jn, hn, _, qn = units[i + SCORE_LOOKAHEAD]
                ahead.append(scores(jn, hn, qn))
            if masked:
                kpos = lax.broadcasted_iota(jnp.int32, sc.shape, 0)
                qpos = lax.broadcasted_iota(jnp.int32, sc.shape, 1)
                sc = jnp.where(kpos <= qpos, sc, NEG_BIG)
            hv = h if isinstance(h, int) else jnp.minimum(h, HEADS - 1)
            if running_max:
                m_prev = m_ref[h, :, q_lo:TQ]
                m_new = jnp.maximum(m_prev, jnp.max(sc, axis=0, keepdims=True))
                m_ref[h, :, q_lo:TQ] = m_new
                p = jnp.exp(sc - m_new).astype(BF16)
                acc_ref[h, :, q_lo:TQ] = (jnp.exp(m_prev - m_new) * acc_ref[h, :, q_lo:TQ]
                                          + _dot(vt_ref[0, j, hv], p))
            else:
                p = jnp.exp(sc).astype(BF16)
                acc_ref[h, :, q_lo:TQ] += _dot(vt_ref[0, j, hv], p)

    def body(i, carry):
        units = []
        for t in range(UNITS_PER_ITER):
            u = i * UNITS_PER_ITER + t
            h = i32(0)
            off = shift[0]
            for g in range(HEADS):
                past = u >= upto[g]
                h = h + past.astype(i32)
                if g + 1 < HEADS:
                    off = jnp.where(past, shift[g + 1], off)
            j = jnp.where(u >= total, 0, u + off)
            units.append((j, h, False, 0))
        run_units(units)
        return carry

    lax.fori_loop(0, (total + UNITS_PER_ITER - 1) // UNITS_PER_ITER, body, 0)
    run_units([(qi * blocks_per_tile + t, h, True, t * TK)
               for t in range(blocks_per_tile) for h in range(HEADS)])

    acc = jnp.concatenate(
        [acc_ref[h, 0:HEAD_DIM, :] * (1.0 / acc_ref[h, HEAD_DIM:HEAD_DIM + 1, :])
         for h in range(HEADS)], axis=0)
    at = (acc * za_ref[0].astype(F32)).astype(BF16)
    ya = _dot_tn(at, wa_ref[...])
    merged = (sga_ref[0].astype(F32) * ya + mb_ref[0].astype(F32)).astype(BF16)
    o_ref[0] = x_ref[0] + gate_ref[0] * _dot(merged, wo_ref[...])


def _attn(running_max, cend, bound, qt, cq, k, kx, vt, za, sga, mb, x, gate3, wa, wo):
    b, s, d = x.shape
    const = lambda shape: pl.BlockSpec(shape, lambda bi, qi: (0,) * len(shape))
    in_specs = [
        pl.BlockSpec(memory_space=pltpu.SMEM),
        pl.BlockSpec(memory_space=pltpu.SMEM),
        pl.BlockSpec((1, ATTN_W, TQ), lambda bi, qi: (bi, 0, qi)),
        pl.BlockSpec((1, HEADS, TQ), lambda bi, qi: (bi, 0, qi)),
        pl.BlockSpec((1, HEADS // 2, s, LANES), lambda bi, qi: (bi, 0, 0, 0)),
        pl.BlockSpec((1, s, LANES), lambda bi, qi: (bi, 0, 0)),
        pl.BlockSpec((1, s // TK, HEADS, V_ROWS, TK), lambda bi, qi: (bi, 0, 0, 0, 0)),
        pl.BlockSpec((1, ATTN_W, TQ), lambda bi, qi: (bi, 0, qi)),
        pl.BlockSpec((1, TQ, d), lambda bi, qi: (bi, qi, 0)),
        pl.BlockSpec((1, TQ, d), lambda bi, qi: (bi, qi, 0)),
        pl.BlockSpec((1, TQ, d), lambda bi, qi: (bi, qi, 0)),
        pl.BlockSpec((1, 1, d), lambda bi, qi: (bi, 0, 0)),
        const(wa.shape), const(wo.shape),
    ]
    return pl.pallas_call(
        functools.partial(_attn_kernel, running_max=running_max),
        grid=(b, s // TQ),
        in_specs=in_specs,
        out_specs=pl.BlockSpec((1, TQ, d), lambda bi, qi: (bi, qi, 0)),
        out_shape=jax.ShapeDtypeStruct((b, s, d), F32),
        scratch_shapes=[pltpu.VMEM((HEADS + 1, 2 * LANES, TQ), BF16),
                        pltpu.VMEM((HEADS + 1, 1, TQ), F32),
                        pltpu.VMEM((HEADS + 1, V_ROWS, TQ), F32)],
        compiler_params=pltpu.CompilerParams(
            dimension_semantics=("arbitrary", "arbitrary"),
            vmem_limit_bytes=VMEM_LIMIT),
        name="attn_running_max" if running_max else "attn",
    )(cend, bound, qt, cq, k, kx, vt, za, sga, mb, x, gate3, wa, wo)


def _layer(x, c, w_ada, b_ada, norm_g, w_in, b_f, q_norm_g, k_norm_g, conv_w,
           w_attn_out, w_conv_out, w_o):
    b, s, d = x.shape
    ada = _ada(c, w_ada, b_ada)
    ada3 = ada.reshape(b, 1, 3 * d)
    gate3 = ada3[:, :, 2 * d:]

    def hi_lo_lanes(v):
        rep = jnp.concatenate([v] * CUM_PARTS, axis=-1)
        gap = jnp.zeros(v.shape[:-1] + (LO_LANE - ONES_LANE,), v.dtype)
        tail = jnp.zeros(v.shape[:-1] + (LANES - LO_LANE - ONES_LANE,), v.dtype)
        return jnp.concatenate([rep, gap, rep, tail], axis=-1)

    w_f = w_in[:, IN_OFFS[3]:IN_OFFS[4]]
    ft = jnp.concatenate([w_f, w_f, jnp.zeros((d, T_PAD - T_END), F32)], axis=1)
    wn, wt = _prep(w_in, hi_lo_lanes(w_f), ft)

    bfn = hi_lo_lanes(b_f.reshape(1, HEADS))
    bft = jnp.broadcast_to(jnp.concatenate([b_f, b_f])[:, None], (2 * HEADS, TM))
    gq = (jnp.tile(q_norm_g, HEADS) * HEAD_DIM ** -0.5).reshape(ATTN_W, 1)
    gk = jnp.tile(k_norm_g, HEADS).reshape(1, ATTN_W)

    head_of = np.arange(ATTN_W) // HEAD_DIM
    pm = jnp.asarray(np.where(head_of[:, None] == head_of[None, :], 1.0 / HEAD_DIM, 0.0), BF16)
    tok = np.arange(TM)
    lower = (tok[None, :] <= tok[:, None]).astype(np.float32)
    lt = jnp.asarray(lower, BF16)
    ut = jnp.asarray(np.concatenate([lower.T, np.ones((TM, LANES), np.float32)], axis=1), BF16)

    k, kx, qt, cq, vt, za, sga, mb = _proj(
        x, ada3, norm_g.reshape(1, d), wn, wt, bfn, bft, gq, gk, conv_w,
        w_conv_out.astype(BF16), pm, lt, ut)
    logit_bound = HEAD_DIM ** 0.5 * jnp.max(jnp.abs(q_norm_g)) * jnp.max(jnp.abs(k_norm_g))
    cend = cq[:, :, TK - 1::TK].reshape(-1)
    args = (cend, logit_bound.reshape(1), qt, cq, k, kx, vt, za, sga, mb, x, gate3,
            w_attn_out.astype(BF16), w_o.astype(BF16))
    return lax.cond(logit_bound <= MAX_RAW_LOGIT,
                    functools.partial(_attn, False), functools.partial(_attn, True), *args)


@jax.jit
def kernel(x, c, w_ada, b_ada, norm_g, w_in, b_f, q_norm_g, k_norm_g, conv_w,
           w_attn_out, w_conv_out, w_o):
    for i in range(w_ada.shape[0]):
        x = _layer(x, c, w_ada[i], b_ada[i], norm_g[i], w_in[i], b_f[i], q_norm_g[i],
                   k_norm_g[i], conv_w[i], w_attn_out[i], w_conv_out[i], w_o[i])
    return x
```

```python
import functools

import jax
import jax.numpy as jnp
import numpy as np
from jax import lax
from jax.experimental import pallas as pl
from jax.experimental.pallas import tpu as pltpu

D_MODEL = 1024
HEADS = 8
HEAD_DIM = 64
ATTN_W = HEADS * HEAD_DIM
CONV_W = 512
CONV_K = 3
EPS = 1e-6

LANES = 128
TM = 512
TQ = 512
TK = 256
CUM_PARTS = 3
ONES_LANE = CUM_PARTS * HEADS
LO_LANE = 32
V_ROWS = HEAD_DIM + 16
SCORE_LOOKAHEAD = 4
UNITS_PER_ITER = 8
NEG_BIG = -1e30
MAX_RAW_LOGIT = 40.0
SKIP_LOG_WEIGHT = 30.0
LOGIT_BOUND_MARGIN = 1.05
VMEM_LIMIT = 56 * 1024 * 1024

F32 = jnp.float32
BF16 = jnp.bfloat16


def _log_sigmoid(x):
    return jnp.minimum(x, 0.0) - jnp.log(1.0 + jnp.exp(-jnp.abs(x)))


def _sigmoid(x):
    return 0.5 * jnp.tanh(0.5 * x) + 0.5


def _silu(x):
    hx = 0.5 * x
    return hx * jnp.tanh(hx) + hx


def _dot(a, b):
    return jnp.dot(a, b, preferred_element_type=F32)


def _dot_nt(a, b):
    return lax.dot_general(a, b, (((1,), (1,)), ((), ())), preferred_element_type=F32)


def _dot_tn(a, b):
    return lax.dot_general(a, b, (((0,), (0,)), ((), ())), preferred_element_type=F32)


def _ada_kernel(c_ref, w_ref, b_ref, o_ref):
    o_ref[...] = jnp.dot(c_ref[...], w_ref[...], preferred_element_type=F32,
                         precision=lax.Precision.HIGHEST) + b_ref[...]


def _ada(c, w_ada, b_ada):
    b, d = c.shape
    n = w_ada.shape[1]
    return pl.pallas_call(
        _ada_kernel,
        grid=(n // d,),
        in_specs=[pl.BlockSpec((b, d), lambda j: (0, 0)),
                  pl.BlockSpec((d, d), lambda j: (0, j)),
                  pl.BlockSpec((1, d), lambda j: (0, j))],
        out_specs=pl.BlockSpec((b, d), lambda j: (0, j)),
        out_shape=jax.ShapeDtypeStruct((b, n), F32),
        name="ada",
    )(c, w_ada, b_ada.reshape(1, n))


N_K = 0
N_F = N_K + ATTN_W
N_GB = N_F + LANES
N_GC = N_GB + CONV_W
N_U = N_GC + CONV_W
N_ZB = N_U + CONV_W
N_GA = N_ZB + CONV_W
N_GB2 = N_GA + D_MODEL
N_END = N_GB2 + D_MODEL
T_Q = 0
T_V = T_Q + ATTN_W
T_ZA = T_V + ATTN_W
T_F = T_ZA + ATTN_W
T_END = T_F + 2 * HEADS

IN_SIZES = (ATTN_W, ATTN_W, ATTN_W, HEADS, ATTN_W, CONV_W, CONV_W, CONV_W, CONV_W, D_MODEL, D_MODEL)
IN_OFFS = tuple(sum(IN_SIZES[:i]) for i in range(len(IN_SIZES) + 1))
PREP_COLS = 256


def _prep_kernel(w_ref, wn_ref, wt_ref):
    rows = lambda i, j: w_ref[IN_OFFS[i]:IN_OFFS[j], :]
    f = rows(3, 4)
    zeros = lambda n: jnp.zeros((n, PREP_COLS), F32)
    fx = jnp.concatenate([f] * CUM_PARTS + [zeros(LO_LANE - ONES_LANE)] + [f] * CUM_PARTS
                         + [zeros(LANES - LO_LANE - ONES_LANE)], axis=0)
    wn_ref[N_K:N_F, :] = rows(1, 2).astype(BF16)
    wn_ref[N_F:N_GB, :] = fx.astype(BF16)
    wn_ref[N_GB:N_END, :] = rows(5, 11).astype(BF16)
    wt_ref[T_Q:T_V, :] = rows(0, 1).astype(BF16)
    wt_ref[T_V:T_ZA, :] = rows(2, 3).astype(BF16)
    wt_ref[T_ZA:T_F, :] = rows(4, 5).astype(BF16)
    wt_ref[T_F:T_END, :] = jnp.concatenate([f, f], axis=0).astype(BF16)


def _prep(w_t):
    n, d = w_t.shape
    return pl.pallas_call(
        _prep_kernel,
        grid=(d // PREP_COLS,),
        in_specs=[pl.BlockSpec((n, PREP_COLS), lambda i: (0, i))],
        out_specs=[pl.BlockSpec((N_END, PREP_COLS), lambda i: (0, i)),
                   pl.BlockSpec((T_END, PREP_COLS), lambda i: (0, i))],
        out_shape=[jax.ShapeDtypeStruct((N_END, d), BF16),
                   jax.ShapeDtypeStruct((T_END, d), BF16)],
        compiler_params=pltpu.CompilerParams(vmem_limit_bytes=VMEM_LIMIT),
        name="prep",
    )(w_t)


def _proj_kernel(x_ref, xnext_ref, ada_ref, ng_ref, wn_ref, wt_ref, bfn_ref, bft_ref, gq_ref, gk_ref,
                 cw_ref, wb_ref, pm_ref, lt_ref, ut_ref,
                 k_ref, kx_ref, qt_ref, cq_ref, vt_ref, za_ref, sga_ref, mb_ref,
                 cn_ref, ct_ref, cu_ref, h_ref, gc_ref):
    s = pl.program_id(1)
    slot = lax.rem(s, 2)

    def modulated_norm(xr):
        x = xr[0]
        shift = ada_ref[0, :, 0:D_MODEL]
        scale = ada_ref[0, :, D_MODEL:2 * D_MODEL]
        xn = x * lax.rsqrt(jnp.mean(x * x, axis=-1, keepdims=True) + EPS)
        return (xn * ng_ref[...] * (1.0 + scale) + shift).astype(BF16)

    @pl.when(s == 0)
    def _():
        cn_ref[...] = jnp.zeros_like(cn_ref)
        ct_ref[...] = jnp.zeros_like(ct_ref)
        cu_ref[0:8, :] = jnp.zeros((8, CONV_W), F32)
        h0 = modulated_norm(x_ref)
        h_ref[0] = h0
        gc_ref[...] = _dot_nt(h0, wn_ref[N_GC:N_U, :])

    nat = lambda lo, hi: _dot_nt(h_ref[slot], wn_ref[lo:hi, :])
    trn = lambda lo, hi: _dot_nt(wt_ref[lo:hi, :], h_ref[slot])
    lane = lax.broadcasted_iota(jnp.int32, (TM, LANES), 1)
    in_hi = lane < ONES_LANE
    in_lo = (lane >= LO_LANE) & (lane < LO_LANE + ONES_LANE)

    gc = gc_ref[...]
    u = nat(N_U, N_ZB)
    kf = nat(N_K, N_GB)
    kraw = kf[:, 0:ATTN_W]
    flog = kf[:, ATTN_W:]

    cu = gc * u
    cu_ref[8:8 + TM, :] = cu
    conv = (cw_ref[2:3, :] * cu + cw_ref[1:2, :] * cu_ref[7:7 + TM, :]
            + cw_ref[0:1, :] * cu_ref[6:6 + TM, :])
    cu_ref[0:8, :] = cu[TM - 8:TM, :]
    gb = nat(N_GB, N_GC)
    zb = nat(N_ZB, N_GA)

    kk = (kraw * kraw).astype(BF16)
    lf = _log_sigmoid(flog + bfn_ref[...])
    lf_lo = lf - lf.astype(BF16).astype(F32)
    parts = jnp.where(in_hi, lf, jnp.where(in_lo, lf_lo, 0.0)).astype(BF16)
    kms = _dot(kk, pm_ref[...])
    csum = _dot(lt_ref[...], parts)
    ga = nat(N_GA, N_GB2)

    gb2 = nat(N_GB2, N_END)
    ob = (gb * conv * _silu(zb)).astype(BF16)
    mbr = _dot(ob, wb_ref[...])

    sga_ref[0] = _sigmoid(ga).astype(BF16)
    kn = (kraw * lax.rsqrt(kms + EPS) * gk_ref[...]).astype(BF16)
    for pair in range(HEADS // 2):
        k_ref[0, pair] = kn[:, pair * LANES:(pair + 1) * LANES]
    cum = csum + pltpu.roll(csum, LANES - LO_LANE, 1) + cn_ref[...]
    cn_ref[...] = cum[TM - 1:TM, :]
    r1 = cum - cum.astype(BF16).astype(F32)
    r2 = r1 - r1.astype(BF16).astype(F32)
    piece = jnp.where(lane < HEADS, cum, jnp.where(lane < 2 * HEADS, r1, r2))
    is_one = (lane >= ONES_LANE) & (lane < ONES_LANE + CUM_PARTS)
    kx_ref[0] = jnp.where(in_hi, -piece, jnp.where(is_one, 1.0, 0.0)).astype(BF16)
    zf = trn(T_ZA, T_END)
    qraw = trn(T_Q, T_V)

    mb_ref[0] = (_sigmoid(gb2) * mbr).astype(BF16)
    lft = _log_sigmoid(zf[ATTN_W:, :] + bft_ref[...])
    lft_lo = lft - lft.astype(BF16).astype(F32)
    row = lax.broadcasted_iota(jnp.int32, (2 * HEADS, TM), 0)
    ct = _dot(jnp.where(row < HEADS, lft, lft_lo).astype(BF16), ut_ref[...])

    za_ref[0] = _silu(zf[0:ATTN_W, :]).astype(BF16)
    ct = ct[0:HEADS, :] + ct[HEADS:2 * HEADS, :]
    carry = ct_ref[...]
    cq_ref[0] = ct[:, 0:TM] + jnp.concatenate([carry] * (TM // LANES), axis=1)
    ct_ref[...] = carry + ct[:, TM:TM + LANES]
    vraw = trn(T_V, T_ZA)

    q3 = qraw.reshape(HEADS, HEAD_DIM, TM)
    qn = q3 * lax.rsqrt(jnp.mean(q3 * q3, axis=1, keepdims=True) + EPS)
    qt_ref[0] = (qn.reshape(ATTN_W, TM) * gq_ref[...]).astype(BF16)

    vt = vraw.astype(BF16)
    for i in range(TM // TK):
        for hd in range(HEADS):
            vt_ref[0, i, hd, 0:HEAD_DIM, :] = (
                vt[hd * HEAD_DIM:(hd + 1) * HEAD_DIM, i * TK:(i + 1) * TK])
            vt_ref[0, i, hd, HEAD_DIM:V_ROWS, :] = jnp.ones((V_ROWS - HEAD_DIM, TK), BF16)

    h_next = modulated_norm(xnext_ref)
    h_ref[1 - slot] = h_next
    gc_ref[...] = _dot_nt(h_next, wn_ref[N_GC:N_U, :])


def _proj(x, ada3, ng, wn, wt, bfn, bft, gq, gk, cw, wb, pm, lt, ut):
    b, s, d = x.shape
    const = lambda shape: pl.BlockSpec(shape, lambda bi, si: (0,) * len(shape))
    out_shape = [
        jax.ShapeDtypeStruct((b, HEADS // 2, s, LANES), BF16),
        jax.ShapeDtypeStruct((b, s, LANES), BF16),
        jax.ShapeDtypeStruct((b, ATTN_W, s), BF16),
        jax.ShapeDtypeStruct((b, HEADS, s), F32),
        jax.ShapeDtypeStruct((b, s // TK, HEADS, V_ROWS, TK), BF16),
        jax.ShapeDtypeStruct((b, ATTN_W, s), BF16),
        jax.ShapeDtypeStruct((b, s, d), BF16),
        jax.ShapeDtypeStruct((b, s, d), BF16),
    ]
    out_specs = [
        pl.BlockSpec((1, HEADS // 2, TM, LANES), lambda bi, si: (bi, 0, si, 0)),
        pl.BlockSpec((1, TM, LANES), lambda bi, si: (bi, si, 0)),
        pl.BlockSpec((1, ATTN_W, TM), lambda bi, si: (bi, 0, si)),
        pl.BlockSpec((1, HEADS, TM), lambda bi, si: (bi, 0, si)),
        pl.BlockSpec((1, TM // TK, HEADS, V_ROWS, TK), lambda bi, si: (bi, si, 0, 0, 0)),
        pl.BlockSpec((1, ATTN_W, TM), lambda bi, si: (bi, 0, si)),
        pl.BlockSpec((1, TM, d), lambda bi, si: (bi, si, 0)),
        pl.BlockSpec((1, TM, d), lambda bi, si: (bi, si, 0)),
    ]
    last = s // TM - 1
    in_specs = [
        pl.BlockSpec((1, TM, d), lambda bi, si: (bi, si, 0)),
        pl.BlockSpec((1, TM, d), lambda bi, si: (bi, jnp.minimum(si + 1, last), 0)),
        pl.BlockSpec((1, 1, 3 * d), lambda bi, si: (bi, 0, 0)),
        const(ng.shape), const(wn.shape), const(wt.shape), const(bfn.shape), const(bft.shape),
        const(gq.shape), const(gk.shape), const(cw.shape), const(wb.shape), const(pm.shape),
        const(lt.shape), const(ut.shape),
    ]
    return pl.pallas_call(
        _proj_kernel,
        grid=(b, s // TM),
        in_specs=in_specs,
        out_specs=out_specs,
        out_shape=out_shape,
        scratch_shapes=[pltpu.VMEM((1, LANES), F32),
                        pltpu.VMEM((HEADS, LANES), F32),
                        pltpu.VMEM((TM + 8, CONV_W), F32),
                        pltpu.VMEM((2, TM, D_MODEL), BF16),
                        pltpu.VMEM((TM, CONV_W), F32)],
        compiler_params=pltpu.CompilerParams(
            dimension_semantics=("arbitrary", "arbitrary"),
            vmem_limit_bytes=VMEM_LIMIT),
        name="proj",
    )(x, x, ada3, ng, wn, wt, bfn, bft, gq, gk, cw, wb, pm, lt, ut)


def _attn_kernel(cend_ref, bound_ref, qt_ref, cq_ref, k_ref, kx_ref, vt_ref, za_ref, sga_ref, mb_ref,
                 x_ref, gate_ref, wa_ref, wo_ref, o_ref, rhs_ref, m_ref, acc_ref, *, running_max):
    bi = pl.program_id(0)
    qi = pl.program_id(1)
    blocks_per_tile = TQ // TK
    n_blocks = kx_ref.shape[1] // TK
    i32 = jnp.int32

    last_before = qi * blocks_per_tile - 1
    slack = LOGIT_BOUND_MARGIN * 2.0 * bound_ref[0] + SKIP_LOG_WEIGHT

    first = []
    for h in range(HEADS):
        base = (bi * HEADS + h) * n_blocks
        f_tile = cend_ref[base + jnp.maximum(last_before, 0)]
        skipped = i32(0)
        for j in range(n_blocks - blocks_per_tile - 1):
            ok = (j < last_before) & (f_tile - cend_ref[base + j] + slack <= 0.0)
            skipped = skipped + ok.astype(i32)
        first.append(skipped)
    upto, shift, total = [], [], i32(0)
    for h in range(HEADS):
        shift.append(first[h] - total)
        total = total + (qi * blocks_per_tile - first[h])
        upto.append(total)

    row = lax.broadcasted_iota(jnp.int32, (LANES, TQ), 0)
    for h in range(HEADS):
        pair, half = divmod(h, 2)
        qp = qt_ref[0, pair * LANES:(pair + 1) * LANES, :]
        mine = (row >= half * HEAD_DIM) & (row < (half + 1) * HEAD_DIM)
        rhs_ref[h, 0:LANES, :] = qp * jnp.where(mine, 1.0, 0.0).astype(BF16)
        cq = cq_ref[0, h:h + 1, :]
        r1 = cq - cq.astype(BF16).astype(F32)
        r2 = r1 - r1.astype(BF16).astype(F32)
        sel = (row == h) | (row == HEADS + h) | (row == 2 * HEADS + h)
        f = jnp.where(row == ONES_LANE, cq,
                      jnp.where(row == ONES_LANE + 1, r1,
                                jnp.where(row == ONES_LANE + 2, r2, 0.0)))
        rhs_ref[h, LANES:2 * LANES, :] = jnp.where(sel, 1.0, f).astype(BF16)
    rhs_ref[HEADS] = jnp.zeros(rhs_ref.shape[1:], BF16)

    if running_max:
        m_ref[...] = jnp.full(m_ref.shape, NEG_BIG, F32)
    acc_ref[...] = jnp.zeros_like(acc_ref)

    def scores(j, h, q_lo):
        start = pl.multiple_of(j * TK, TK)
        pair = h // 2 if isinstance(h, int) else jnp.minimum(h, HEADS - 1) // 2
        lhs = jnp.concatenate([k_ref[0, pair, pl.ds(start, TK), :],
                               kx_ref[0, pl.ds(start, TK), :]], axis=1)
        return _dot(lhs, rhs_ref[h, :, q_lo:TQ])

    def run_units(units):
        ahead = [scores(j, h, q_lo) for j, h, _, q_lo in units[:SCORE_LOOKAHEAD]]
        for i, (j, h, masked, q_lo) in enumerate(units):
            sc = ahead.pop(0)
            if i + SCORE_LOOKAHEAD < len(units):
                jn, hn, _, qn = units[i + SCORE_LOOKAHEAD]
                ahead.append(scores(jn, hn, qn))
            if masked:
                kpos = lax.broadcasted_iota(jnp.int32, sc.shape, 0)
                qpos = lax.broadcasted_iota(jnp.int32, sc.shape, 1)
                sc = jnp.where(kpos <= qpos, sc, NEG_BIG)
            hv = h if isinstance(h, int) else jnp.minimum(h, HEADS - 1)
            if running_max:
                m_prev = m_ref[h, :, q_lo:TQ]
                m_new = jnp.maximum(m_prev, jnp.max(sc, axis=0, keepdims=True))
                m_ref[h, :, q_lo:TQ] = m_new
                p = jnp.exp(sc - m_new).astype(BF16)
                acc_ref[h, :, q_lo:TQ] = (jnp.exp(m_prev - m_new) * acc_ref[h, :, q_lo:TQ]
                                          + _dot(vt_ref[0, j, hv], p))
            else:
                p = jnp.exp(sc).astype(BF16)
                acc_ref[h, :, q_lo:TQ] += _dot(vt_ref[0, j, hv], p)

    def body(i, carry):
        units = []
        for t in range(UNITS_PER_ITER):
            u = i * UNITS_PER_ITER + t
            h = i32(0)
            off = shift[0]
            for g in range(HEADS):
                past = u >= upto[g]
                h = h + past.astype(i32)
                if g + 1 < HEADS:
                    off = jnp.where(past, shift[g + 1], off)
            j = jnp.where(u >= total, 0, u + off)
            units.append((j, h, False, 0))
        run_units(units)
        return carry

    lax.fori_loop(0, (total + UNITS_PER_ITER - 1) // UNITS_PER_ITER, body, 0)
    run_units([(qi * blocks_per_tile + t, h, True, t * TK)
               for t in range(blocks_per_tile) for h in range(HEADS)])

    acc = jnp.concatenate(
        [acc_ref[h, 0:HEAD_DIM, :] * (1.0 / acc_ref[h, HEAD_DIM:HEAD_DIM + 1, :])
         for h in range(HEADS)], axis=0)
    at = (acc * za_ref[0].astype(F32)).astype(BF16)
    ya = _dot_tn(at, wa_ref[...])
    merged = (sga_ref[0].astype(F32) * ya + mb_ref[0].astype(F32)).astype(BF16)
    o_ref[0] = x_ref[0] + gate_ref[0] * _dot(merged, wo_ref[...])


def _attn(running_max, cend, bound, qt, cq, k, kx, vt, za, sga, mb, x, gate3, wa, wo):
    b, s, d = x.shape
    const = lambda shape: pl.BlockSpec(shape, lambda bi, qi: (0,) * len(shape))
    in_specs = [
        pl.BlockSpec(memory_space=pltpu.SMEM),
        pl.BlockSpec(memory_space=pltpu.SMEM),
        pl.BlockSpec((1, ATTN_W, TQ), lambda bi, qi: (bi, 0, qi)),
        pl.BlockSpec((1, HEADS, TQ), lambda bi, qi: (bi, 0, qi)),
        pl.BlockSpec((1, HEADS // 2, s, LANES), lambda bi, qi: (bi, 0, 0, 0)),
        pl.BlockSpec((1, s, LANES), lambda bi, qi: (bi, 0, 0)),
        pl.BlockSpec((1, s // TK, HEADS, V_ROWS, TK), lambda bi, qi: (bi, 0, 0, 0, 0)),
        pl.BlockSpec((1, ATTN_W, TQ), lambda bi, qi: (bi, 0, qi)),
        pl.BlockSpec((1, TQ, d), lambda bi, qi: (bi, qi, 0)),
        pl.BlockSpec((1, TQ, d), lambda bi, qi: (bi, qi, 0)),
        pl.BlockSpec((1, TQ, d), lambda bi, qi: (bi, qi, 0)),
        pl.BlockSpec((1, 1, d), lambda bi, qi: (bi, 0, 0)),
        const(wa.shape), const(wo.shape),
    ]
    return pl.pallas_call(
        functools.partial(_attn_kernel, running_max=running_max),
        grid=(b, s // TQ),
        in_specs=in_specs,
        out_specs=pl.BlockSpec((1, TQ, d), lambda bi, qi: (bi, qi, 0)),
        out_shape=jax.ShapeDtypeStruct((b, s, d), F32),
        scratch_shapes=[pltpu.VMEM((HEADS + 1, 2 * LANES, TQ), BF16),
                        pltpu.VMEM((HEADS + 1, 1, TQ), F32),
                        pltpu.VMEM((HEADS + 1, V_ROWS, TQ), F32)],
        compiler_params=pltpu.CompilerParams(
            dimension_semantics=("arbitrary", "arbitrary"),
            vmem_limit_bytes=VMEM_LIMIT),
        name="attn_running_max" if running_max else "attn",
    )(cend, bound, qt, cq, k, kx, vt, za, sga, mb, x, gate3, wa, wo)


def _layer(x, c, w_ada, b_ada, norm_g, w_in, b_f, q_norm_g, k_norm_g, conv_w,
           w_attn_out, w_conv_out, w_o):
    b, s, d = x.shape
    ada = _ada(c, w_ada, b_ada)
    ada3 = ada.reshape(b, 1, 3 * d)
    gate3 = ada3[:, :, 2 * d:]

    def hi_lo_lanes(v):
        rep = jnp.concatenate([v] * CUM_PARTS, axis=-1)
        gap = jnp.zeros(v.shape[:-1] + (LO_LANE - ONES_LANE,), v.dtype)
        tail = jnp.zeros(v.shape[:-1] + (LANES - LO_LANE - ONES_LANE,), v.dtype)
        return jnp.concatenate([rep, gap, rep, tail], axis=-1)

    wn, wt = _prep(w_in.T)

    bfn = hi_lo_lanes(b_f.reshape(1, HEADS))
    bft = jnp.broadcast_to(jnp.concatenate([b_f, b_f])[:, None], (2 * HEADS, TM))
    gq = (jnp.tile(q_norm_g, HEADS) * HEAD_DIM ** -0.5).reshape(ATTN_W, 1)
    gk = jnp.tile(k_norm_g, HEADS).reshape(1, ATTN_W)

    head_of = np.arange(ATTN_W) // HEAD_DIM
    pm = jnp.asarray(np.where(head_of[:, None] == head_of[None, :], 1.0 / HEAD_DIM, 0.0), BF16)
    tok = np.arange(TM)
    lower = (tok[None, :] <= tok[:, None]).astype(np.float32)
    lt = jnp.asarray(lower, BF16)
    ut = jnp.asarray(np.concatenate([lower.T, np.ones((TM, LANES), np.float32)], axis=1), BF16)

    k, kx, qt, cq, vt, za, sga, mb = _proj(
        x, ada3, norm_g.reshape(1, d), wn, wt, bfn, bft, gq, gk, conv_w,
        w_conv_out.astype(BF16), pm, lt, ut)
    logit_bound = HEAD_DIM ** 0.5 * jnp.max(jnp.abs(q_norm_g)) * jnp.max(jnp.abs(k_norm_g))
    cend = cq[:, :, TK - 1::TK].reshape(-1)
    args = (cend, logit_bound.reshape(1), qt, cq, k, kx, vt, za, sga, mb, x, gate3,
            w_attn_out.astype(BF16), w_o.astype(BF16))
    return lax.cond(logit_bound <= MAX_RAW_LOGIT,
                    functools.partial(_attn, False), functools.partial(_attn, True), *args)


@jax.jit
def kernel(x, c, w_ada, b_ada, norm_g, w_in, b_f, q_norm_g, k_norm_g, conv_w,
           w_attn_out, w_conv_out, w_o):
    for i in range(w_ada.shape[0]):
        x = _layer(x, c, w_ada[i], b_ada[i], norm_g[i], w_in[i], b_f[i], q_norm_g[i],
                   k_norm_g[i], conv_w[i], w_attn_out[i], w_conv_out[i], w_o[i])
    return x
```

```python
import functools

import jax
import jax.numpy as jnp
import numpy as np
from jax import lax
from jax.experimental import pallas as pl
from jax.experimental.pallas import tpu as pltpu

D_MODEL = 1024
HEADS = 8
HEAD_DIM = 64
ATTN_W = HEADS * HEAD_DIM
CONV_W = 512
CONV_K = 3
EPS = 1e-6

LANES = 128
TM = 512
TQ = 512
TK = 256
CUM_PARTS = 3
ONES_LANE = CUM_PARTS * HEADS
LO_LANE = 32
V_ROWS = HEAD_DIM + 16
SCORE_LOOKAHEAD = 4
UNITS_PER_ITER = 8
NEG_BIG = -1e30
MAX_RAW_LOGIT = 40.0
SKIP_LOG_WEIGHT = 30.0
LOGIT_BOUND_MARGIN = 1.05
VMEM_LIMIT = 56 * 1024 * 1024

F32 = jnp.float32
BF16 = jnp.bfloat16


def _log_sigmoid(x):
    return jnp.minimum(x, 0.0) - jnp.log(1.0 + jnp.exp(-jnp.abs(x)))


def _sigmoid(x):
    return 0.5 * jnp.tanh(0.5 * x) + 0.5


def _silu(x):
    hx = 0.5 * x
    return hx * jnp.tanh(hx) + hx


def _dot(a, b):
    return jnp.dot(a, b, preferred_element_type=F32)


def _dot_nt(a, b):
    return lax.dot_general(a, b, (((1,), (1,)), ((), ())), preferred_element_type=F32)


def _dot_tn(a, b):
    return lax.dot_general(a, b, (((0,), (0,)), ((), ())), preferred_element_type=F32)


def _ada_kernel(c_ref, w_ref, b_ref, o_ref):
    o_ref[...] = jnp.dot(c_ref[...], w_ref[...], preferred_element_type=F32,
                         precision=lax.Precision.HIGHEST) + b_ref[...]


def _ada(c, w_ada, b_ada):
    b, d = c.shape
    n = w_ada.shape[1]
    return pl.pallas_call(
        _ada_kernel,
        grid=(n // d,),
        in_specs=[pl.BlockSpec((b, d), lambda j: (0, 0)),
                  pl.BlockSpec((d, d), lambda j: (0, j)),
                  pl.BlockSpec((1, d), lambda j: (0, j))],
        out_specs=pl.BlockSpec((b, d), lambda j: (0, j)),
        out_shape=jax.ShapeDtypeStruct((b, n), F32),
        name="ada",
    )(c, w_ada, b_ada.reshape(1, n))


N_K = 0
N_F = N_K + ATTN_W
N_GB = N_F + LANES
N_GC = N_GB + CONV_W
N_U = N_GC + CONV_W
N_ZB = N_U + CONV_W
N_GA = N_ZB + CONV_W
N_GB2 = N_GA + D_MODEL
N_END = N_GB2 + D_MODEL
T_Q = 0
T_V = T_Q + ATTN_W
T_ZA = T_V + ATTN_W
T_F = T_ZA + ATTN_W
T_END = T_F + 2 * HEADS

IN_SIZES = (ATTN_W, ATTN_W, ATTN_W, HEADS, ATTN_W, CONV_W, CONV_W, CONV_W, CONV_W, D_MODEL, D_MODEL)
IN_OFFS = tuple(sum(IN_SIZES[:i]) for i in range(len(IN_SIZES) + 1))
PREP_COLS = 256


def _prep_kernel(w_ref, wn_ref, wt_ref):
    rows = lambda i, j: w_ref[IN_OFFS[i]:IN_OFFS[j], :]
    f = rows(3, 4)
    zeros = lambda n: jnp.zeros((n, PREP_COLS), F32)
    fx = jnp.concatenate([f] * CUM_PARTS + [zeros(LO_LANE - ONES_LANE)] + [f] * CUM_PARTS
                         + [zeros(LANES - LO_LANE - ONES_LANE)], axis=0)
    wn_ref[...] = jnp.concatenate([rows(1, 2), fx, rows(5, 11)], axis=0).astype(BF16).T
    wt_ref[T_Q:T_V, :] = rows(0, 1).astype(BF16)
    wt_ref[T_V:T_ZA, :] = rows(2, 3).astype(BF16)
    wt_ref[T_ZA:T_F, :] = rows(4, 5).astype(BF16)
    wt_ref[T_F:T_END, :] = jnp.concatenate([f, f], axis=0).astype(BF16)


def _prep(w_t):
    n, d = w_t.shape
    return pl.pallas_call(
        _prep_kernel,
        grid=(d // PREP_COLS,),
        in_specs=[pl.BlockSpec((n, PREP_COLS), lambda i: (0, i))],
        out_specs=[pl.BlockSpec((PREP_COLS, N_END), lambda i: (i, 0)),
                   pl.BlockSpec((T_END, PREP_COLS), lambda i: (0, i))],
        out_shape=[jax.ShapeDtypeStruct((d, N_END), BF16),
                   jax.ShapeDtypeStruct((T_END, d), BF16)],
        compiler_params=pltpu.CompilerParams(vmem_limit_bytes=VMEM_LIMIT),
        name="prep",
    )(w_t)


def _proj_kernel(x_ref, xnext_ref, ada_ref, ng_ref, wn_ref, wt_ref, bfn_ref, bft_ref, gq_ref, gk_ref,
                 cw_ref, wb_ref, pm_ref, lt_ref, ut_ref,
                 k_ref, kx_ref, qt_ref, cq_ref, vt_ref, za_ref, sga_ref, mb_ref,
                 cn_ref, ct_ref, cu_ref, h_ref, gc_ref):
    s = pl.program_id(1)
    slot = lax.rem(s, 2)

    def modulated_norm(xr):
        x = xr[0]
        shift = ada_ref[0, :, 0:D_MODEL]
        scale = ada_ref[0, :, D_MODEL:2 * D_MODEL]
        xn = x * lax.rsqrt(jnp.mean(x * x, axis=-1, keepdims=True) + EPS)
        return (xn * ng_ref[...] * (1.0 + scale) + shift).astype(BF16)

    @pl.when(s == 0)
    def _():
        cn_ref[...] = jnp.zeros_like(cn_ref)
        ct_ref[...] = jnp.zeros_like(ct_ref)
        cu_ref[0:8, :] = jnp.zeros((8, CONV_W), F32)
        h0 = modulated_norm(x_ref)
        h_ref[0] = h0
        gc_ref[...] = _dot(h0, wn_ref[:, N_GC:N_U])

    nat = lambda lo, hi: _dot(h_ref[slot], wn_ref[:, lo:hi])
    trn = lambda lo, hi: _dot_nt(wt_ref[lo:hi, :], h_ref[slot])
    lane = lax.broadcasted_iota(jnp.int32, (TM, LANES), 1)
    in_hi = lane < ONES_LANE
    in_lo = (lane >= LO_LANE) & (lane < LO_LANE + ONES_LANE)

    gc = gc_ref[...]
    u = nat(N_U, N_ZB)
    kf = nat(N_K, N_GB)
    kraw = kf[:, 0:ATTN_W]
    flog = kf[:, ATTN_W:]

    cu = gc * u
    cu_ref[8:8 + TM, :] = cu
    conv = (cw_ref[2:3, :] * cu + cw_ref[1:2, :] * cu_ref[7:7 + TM, :]
            + cw_ref[0:1, :] * cu_ref[6:6 + TM, :])
    cu_ref[0:8, :] = cu[TM - 8:TM, :]
    gb = nat(N_GB, N_GC)
    zb = nat(N_ZB, N_GA)

    kk = (kraw * kraw).astype(BF16)
    lf = _log_sigmoid(flog + bfn_ref[...])
    lf_lo = lf - lf.astype(BF16).astype(F32)
    parts = jnp.where(in_hi, lf, jnp.where(in_lo, lf_lo, 0.0)).astype(BF16)
    kms = _dot(kk, pm_ref[...])
    csum = _dot(lt_ref[...], parts)
    ga = nat(N_GA, N_GB2)

    gb2 = nat(N_GB2, N_END)
    ob = (gb * conv * _silu(zb)).astype(BF16)
    mbr = _dot(ob, wb_ref[...])

    sga_ref[0] = _sigmoid(ga).astype(BF16)
    kn = (kraw * lax.rsqrt(kms + EPS) * gk_ref[...]).astype(BF16)
    for pair in range(HEADS // 2):
        k_ref[0, pair] = kn[:, pair * LANES:(pair + 1) * LANES]
    cum = csum + pltpu.roll(csum, LANES - LO_LANE, 1) + cn_ref[...]
    cn_ref[...] = cum[TM - 1:TM, :]
    r1 = cum - cum.astype(BF16).astype(F32)
    r2 = r1 - r1.astype(BF16).astype(F32)
    piece = jnp.where(lane < HEADS, cum, jnp.where(lane < 2 * HEADS, r1, r2))
    is_one = (lane >= ONES_LANE) & (lane < ONES_LANE + CUM_PARTS)
    kx_ref[0] = jnp.where(in_hi, -piece, jnp.where(is_one, 1.0, 0.0)).astype(BF16)
    zf = trn(T_ZA, T_END)
    qraw = trn(T_Q, T_V)

    mb_ref[0] = (_sigmoid(gb2) * mbr).astype(BF16)
    lft = _log_sigmoid(zf[ATTN_W:, :] + bft_ref[...])
    lft_lo = lft - lft.astype(BF16).astype(F32)
    row = lax.broadcasted_iota(jnp.int32, (2 * HEADS, TM), 0)
    ct = _dot(jnp.where(row < HEADS, lft, lft_lo).astype(BF16), ut_ref[...])

    za_ref[0] = _silu(zf[0:ATTN_W, :]).astype(BF16)
    ct = ct[0:HEADS, :] + ct[HEADS:2 * HEADS, :]
    carry = ct_ref[...]
    cq_ref[0] = ct[:, 0:TM] + jnp.concatenate([carry] * (TM // LANES), axis=1)
    ct_ref[...] = carry + ct[:, TM:TM + LANES]
    vraw = trn(T_V, T_ZA)

    q3 = qraw.reshape(HEADS, HEAD_DIM, TM)
    qn = q3 * lax.rsqrt(jnp.mean(q3 * q3, axis=1, keepdims=True) + EPS)
    qt_ref[0] = (qn.reshape(ATTN_W, TM) * gq_ref[...]).astype(BF16)

    vt = vraw.astype(BF16)
    for i in range(TM // TK):
        for hd in range(HEADS):
            vt_ref[0, i, hd, 0:HEAD_DIM, :] = (
                vt[hd * HEAD_DIM:(hd + 1) * HEAD_DIM, i * TK:(i + 1) * TK])
            vt_ref[0, i, hd, HEAD_DIM:V_ROWS, :] = jnp.ones((V_ROWS - HEAD_DIM, TK), BF16)

    h_next = modulated_norm(xnext_ref)
    h_ref[1 - slot] = h_next
    gc_ref[...] = _dot(h_next, wn_ref[:, N_GC:N_U])


def _proj(x, ada3, ng, wn, wt, bfn, bft, gq, gk, cw, wb, pm, lt, ut):
    b, s, d = x.shape
    const = lambda shape: pl.BlockSpec(shape, lambda bi, si: (0,) * len(shape))
    out_shape = [
        jax.ShapeDtypeStruct((b, HEADS // 2, s, LANES), BF16),
        jax.ShapeDtypeStruct((b, s, LANES), BF16),
        jax.ShapeDtypeStruct((b, ATTN_W, s), BF16),
        jax.ShapeDtypeStruct((b, HEADS, s), F32),
        jax.ShapeDtypeStruct((b, s // TK, HEADS, V_ROWS, TK), BF16),
        jax.ShapeDtypeStruct((b, ATTN_W, s), BF16),
        jax.ShapeDtypeStruct((b, s, d), BF16),
        jax.ShapeDtypeStruct((b, s, d), BF16),
    ]
    out_specs = [
        pl.BlockSpec((1, HEADS // 2, TM, LANES), lambda bi, si: (bi, 0, si, 0)),
        pl.BlockSpec((1, TM, LANES), lambda bi, si: (bi, si, 0)),
        pl.BlockSpec((1, ATTN_W, TM), lambda bi, si: (bi, 0, si)),
        pl.BlockSpec((1, HEADS, TM), lambda bi, si: (bi, 0, si)),
        pl.BlockSpec((1, TM // TK, HEADS, V_ROWS, TK), lambda bi, si: (bi, si, 0, 0, 0)),
        pl.BlockSpec((1, ATTN_W, TM), lambda bi, si: (bi, 0, si)),
        pl.BlockSpec((1, TM, d), lambda bi, si: (bi, si, 0)),
        pl.BlockSpec((1, TM, d), lambda bi, si: (bi, si, 0)),
    ]
    last = s // TM - 1
    in_specs = [
        pl.BlockSpec((1, TM, d), lambda bi, si: (bi, si, 0)),
        pl.BlockSpec((1, TM, d), lambda bi, si: (bi, jnp.minimum(si + 1, last), 0)),
        pl.BlockSpec((1, 1, 3 * d), lambda bi, si: (bi, 0, 0)),
        const(ng.shape), const(wn.shape), const(wt.shape), const(bfn.shape), const(bft.shape),
        const(gq.shape), const(gk.shape), const(cw.shape), const(wb.shape), const(pm.shape),
        const(lt.shape), const(ut.shape),
    ]
    return pl.pallas_call(
        _proj_kernel,
        grid=(b, s // TM),
        in_specs=in_specs,
        out_specs=out_specs,
        out_shape=out_shape,
        scratch_shapes=[pltpu.VMEM((1, LANES), F32),
                        pltpu.VMEM((HEADS, LANES), F32),
                        pltpu.VMEM((TM + 8, CONV_W), F32),
                        pltpu.VMEM((2, TM, D_MODEL), BF16),
                        pltpu.VMEM((TM, CONV_W), F32)],
        compiler_params=pltpu.CompilerParams(
            dimension_semantics=("arbitrary", "arbitrary"),
            vmem_limit_bytes=VMEM_LIMIT),
        name="proj",
    )(x, x, ada3, ng, wn, wt, bfn, bft, gq, gk, cw, wb, pm, lt, ut)


def _attn_kernel(cend_ref, bound_ref, qt_ref, cq_ref, k_ref, kx_ref, vt_ref, za_ref, sga_ref, mb_ref,
                 x_ref, gate_ref, wa_ref, wo_ref, o_ref, rhs_ref, m_ref, acc_ref, *, running_max):
    bi = pl.program_id(0)
    qi = pl.program_id(1)
    blocks_per_tile = TQ // TK
    n_blocks = kx_ref.shape[1] // TK
    i32 = jnp.int32

    last_before = qi * blocks_per_tile - 1
    slack = LOGIT_BOUND_MARGIN * 2.0 * bound_ref[0] + SKIP_LOG_WEIGHT

    first = []
    for h in range(HEADS):
        base = (bi * HEADS + h) * n_blocks
        f_tile = cend_ref[base + jnp.maximum(last_before, 0)]
        skipped = i32(0)
        for j in range(n_blocks - blocks_per_tile - 1):
            ok = (j < last_before) & (f_tile - cend_ref[base + j] + slack <= 0.0)
            skipped = skipped + ok.astype(i32)
        first.append(skipped)
    upto, shift, total = [], [], i32(0)
    for h in range(HEADS):
        shift.append(first[h] - total)
        total = total + (qi * blocks_per_tile - first[h])
        upto.append(total)

    row = lax.broadcasted_iota(jnp.int32, (LANES, TQ), 0)
    for h in range(HEADS):
        pair, half = divmod(h, 2)
        qp = qt_ref[0, pair * LANES:(pair + 1) * LANES, :]
        mine = (row >= half * HEAD_DIM) & (row < (half + 1) * HEAD_DIM)
        rhs_ref[h, 0:LANES, :] = qp * jnp.where(mine, 1.0, 0.0).astype(BF16)
        cq = cq_ref[0, h:h + 1, :]
        r1 = cq - cq.astype(BF16).astype(F32)
        r2 = r1 - r1.astype(BF16).astype(F32)
        sel = (row == h) | (row == HEADS + h) | (row == 2 * HEADS + h)
        f = jnp.where(row == ONES_LANE, cq,
                      jnp.where(row == ONES_LANE + 1, r1,
                                jnp.where(row == ONES_LANE + 2, r2, 0.0)))
        rhs_ref[h, LANES:2 * LANES, :] = jnp.where(sel, 1.0, f).astype(BF16)
    rhs_ref[HEADS] = jnp.zeros(rhs_ref.shape[1:], BF16)

    if running_max:
        m_ref[...] = jnp.full(m_ref.shape, NEG_BIG, F32)
    acc_ref[...] = jnp.zeros_like(acc_ref)

    def scores(j, h, q_lo):
        start = pl.multiple_of(j * TK, TK)
        pair = h // 2 if isinstance(h, int) else jnp.minimum(h, HEADS - 1) // 2
        lhs = jnp.concatenate([k_ref[0, pair, pl.ds(start, TK), :],
                               kx_ref[0, pl.ds(start, TK), :]], axis=1)
        return _dot(lhs, rhs_ref[h, :, q_lo:TQ])

    def run_units(units):
        ahead = [scores(j, h, q_lo) for j, h, _, q_lo in units[:SCORE_LOOKAHEAD]]
        for i, (j, h, masked, q_lo) in enumerate(units):
            sc = ahead.pop(0)
            if i + SCORE_LOOKAHEAD < len(units):
                jn, hn, _, qn = units[i + SCORE_LOOKAHEAD]
                ahead.append(scores(jn, hn, qn))
            if masked:
                kpos = lax.broadcasted_iota(jnp.int32, sc.shape, 0)
                qpos = lax.broadcasted_iota(jnp.int32, sc.shape, 1)
                sc = jnp.where(kpos <= qpos, sc, NEG_BIG)
            hv = h if isinstance(h, int) else jnp.minimum(h, HEADS - 1)
            if running_max:
                m_prev = m_ref[h, :, q_lo:TQ]
                m_new = jnp.maximum(m_prev, jnp.max(sc, axis=0, keepdims=True))
                m_ref[h, :, q_lo:TQ] = m_new
                p = jnp.exp(sc - m_new).astype(BF16)
                acc_ref[h, :, q_lo:TQ] = (jnp.exp(m_prev - m_new) * acc_ref[h, :, q_lo:TQ]
                                          + _dot(vt_ref[0, j, hv], p))
            else:
                p = jnp.exp(sc).astype(BF16)
                acc_ref[h, :, q_lo:TQ] += _dot(vt_ref[0, j, hv], p)

    def body(i, carry):
        units = []
        for t in range(UNITS_PER_ITER):
            u = i * UNITS_PER_ITER + t
            h = i32(0)
            off = shift[0]
            for g in range(HEADS):
                past = u >= upto[g]
                h = h + past.astype(i32)
                if g + 1 < HEADS:
                    off = jnp.where(past, shift[g + 1], off)
            j = jnp.where(u >= total, 0, u + off)
            units.append((j, h, False, 0))
        run_units(units)
        return carry

    lax.fori_loop(0, (total + UNITS_PER_ITER - 1) // UNITS_PER_ITER, body, 0)
    run_units([(qi * blocks_per_tile + t, h, True, t * TK)
               for t in range(blocks_per_tile) for h in range(HEADS)])

    acc = jnp.concatenate(
        [acc_ref[h, 0:HEAD_DIM, :] * (1.0 / acc_ref[h, HEAD_DIM:HEAD_DIM + 1, :])
         for h in range(HEADS)], axis=0)
    at = (acc * za_ref[0].astype(F32)).astype(BF16)
    ya = _dot_tn(at, wa_ref[...])
    merged = (sga_ref[0].astype(F32) * ya + mb_ref[0].astype(F32)).astype(BF16)
    o_ref[0] = x_ref[0] + gate_ref[0] * _dot(merged, wo_ref[...])


def _attn(running_max, cend, bound, qt, cq, k, kx, vt, za, sga, mb, x, gate3, wa, wo):
    b, s, d = x.shape
    const = lambda shape: pl.BlockSpec(shape, lambda bi, qi: (0,) * len(shape))
    in_specs = [
        pl.BlockSpec(memory_space=pltpu.SMEM),
        pl.BlockSpec(memory_space=pltpu.SMEM),
        pl.BlockSpec((1, ATTN_W, TQ), lambda bi, qi: (bi, 0, qi)),
        pl.BlockSpec((1, HEADS, TQ), lambda bi, qi: (bi, 0, qi)),
        pl.BlockSpec((1, HEADS // 2, s, LANES), lambda bi, qi: (bi, 0, 0, 0)),
        pl.BlockSpec((1, s, LANES), lambda bi, qi: (bi, 0, 0)),
        pl.BlockSpec((1, s // TK, HEADS, V_ROWS, TK), lambda bi, qi: (bi, 0, 0, 0, 0)),
        pl.BlockSpec((1, ATTN_W, TQ), lambda bi, qi: (bi, 0, qi)),
        pl.BlockSpec((1, TQ, d), lambda bi, qi: (bi, qi, 0)),
        pl.BlockSpec((1, TQ, d), lambda bi, qi: (bi, qi, 0)),
        pl.BlockSpec((1, TQ, d), lambda bi, qi: (bi, qi, 0)),
        pl.BlockSpec((1, 1, d), lambda bi, qi: (bi, 0, 0)),
        const(wa.shape), const(wo.shape),
    ]
    return pl.pallas_call(
        functools.partial(_attn_kernel, running_max=running_max),
        grid=(b, s // TQ),
        in_specs=in_specs,
        out_specs=pl.BlockSpec((1, TQ, d), lambda bi, qi: (bi, qi, 0)),
        out_shape=jax.ShapeDtypeStruct((b, s, d), F32),
        scratch_shapes=[pltpu.VMEM((HEADS + 1, 2 * LANES, TQ), BF16),
                        pltpu.VMEM((HEADS + 1, 1, TQ), F32),
                        pltpu.VMEM((HEADS + 1, V_ROWS, TQ), F32)],
        compiler_params=pltpu.CompilerParams(
            dimension_semantics=("arbitrary", "arbitrary"),
            vmem_limit_bytes=VMEM_LIMIT),
        name="attn_running_max" if running_max else "attn",
    )(cend, bound, qt, cq, k, kx, vt, za, sga, mb, x, gate3, wa, wo)


def _layer(x, c, w_ada, b_ada, norm_g, w_in, b_f, q_norm_g, k_norm_g, conv_w,
           w_attn_out, w_conv_out, w_o):
    b, s, d = x.shape
    ada = _ada(c, w_ada, b_ada)
    ada3 = ada.reshape(b, 1, 3 * d)
    gate3 = ada3[:, :, 2 * d:]

    def hi_lo_lanes(v):
        rep = jnp.concatenate([v] * CUM_PARTS, axis=-1)
        gap = jnp.zeros(v.shape[:-1] + (LO_LANE - ONES_LANE,), v.dtype)
        tail = jnp.zeros(v.shape[:-1] + (LANES - LO_LANE - ONES_LANE,), v.dtype)
        return jnp.concatenate([rep, gap, rep, tail], axis=-1)

    wn, wt = _prep(w_in.T)

    bfn = hi_lo_lanes(b_f.reshape(1, HEADS))
    bft = jnp.broadcast_to(jnp.concatenate([b_f, b_f])[:, None], (2 * HEADS, TM))
    gq = (jnp.tile(q_norm_g, HEADS) * HEAD_DIM ** -0.5).reshape(ATTN_W, 1)
    gk = jnp.tile(k_norm_g, HEADS).reshape(1, ATTN_W)

    head_of = np.arange(ATTN_W) // HEAD_DIM
    pm = jnp.asarray(np.where(head_of[:, None] == head_of[None, :], 1.0 / HEAD_DIM, 0.0), BF16)
    tok = np.arange(TM)
    lower = (tok[None, :] <= tok[:, None]).astype(np.float32)
    lt = jnp.asarray(lower, BF16)
    ut = jnp.asarray(np.concatenate([lower.T, np.ones((TM, LANES), np.float32)], axis=1), BF16)

    k, kx, qt, cq, vt, za, sga, mb = _proj(
        x, ada3, norm_g.reshape(1, d), wn, wt, bfn, bft, gq, gk, conv_w,
        w_conv_out.astype(BF16), pm, lt, ut)
    logit_bound = HEAD_DIM ** 0.5 * jnp.max(jnp.abs(q_norm_g)) * jnp.max(jnp.abs(k_norm_g))
    cend = cq[:, :, TK - 1::TK].reshape(-1)
    args = (cend, logit_bound.reshape(1), qt, cq, k, kx, vt, za, sga, mb, x, gate3,
            w_attn_out.astype(BF16), w_o.astype(BF16))
    return lax.cond(logit_bound <= MAX_RAW_LOGIT,
                    functools.partial(_attn, False), functools.partial(_attn, True), *args)


@jax.jit
def kernel(x, c, w_ada, b_ada, norm_g, w_in, b_f, q_norm_g, k_norm_g, conv_w,
           w_attn_out, w_conv_out, w_o):
    for i in range(w_ada.shape[0]):
        x = _layer(x, c, w_ada[i], b_ada[i], norm_g[i], w_in[i], b_f[i], q_norm_g[i],
                   k_norm_g[i], conv_w[i], w_attn_out[i], w_conv_out[i], w_o[i])
    return x
```

```python
import functools

import jax
import jax.numpy as jnp
import numpy as np
from jax import lax
from jax.experimental import pallas as pl
from jax.experimental.pallas import tpu as pltpu

D_MODEL = 1024
HEADS = 8
HEAD_DIM = 64
ATTN_W = HEADS * HEAD_DIM
CONV_W = 512
CONV_K = 3
EPS = 1e-6

LANES = 128
TM = 512
TQ = 512
TK = 256
CUM_PARTS = 3
ONES_LANE = CUM_PARTS * HEADS
LO_LANE = 32
V_ROWS = HEAD_DIM + 16
SCORE_LOOKAHEAD = 4
UNITS_PER_LONG_ITER = 16
UNITS_PER_ITER = 8
NEG_BIG = -1e30
MAX_RAW_LOGIT = 40.0
SKIP_LOG_WEIGHT = 30.0
LOGIT_BOUND_MARGIN = 1.05
VMEM_LIMIT = 56 * 1024 * 1024

F32 = jnp.float32
BF16 = jnp.bfloat16


def _log_sigmoid(x):
    return jnp.minimum(x, 0.0) - jnp.log(1.0 + jnp.exp(-jnp.abs(x)))


def _sigmoid(x):
    return 0.5 * jnp.tanh(0.5 * x) + 0.5


def _silu(x):
    hx = 0.5 * x
    return hx * jnp.tanh(hx) + hx


def _dot(a, b):
    return jnp.dot(a, b, preferred_element_type=F32)


def _dot_nt(a, b):
    return lax.dot_general(a, b, (((1,), (1,)), ((), ())), preferred_element_type=F32)


def _dot_tn(a, b):
    return lax.dot_general(a, b, (((0,), (0,)), ((), ())), preferred_element_type=F32)


def _ada_kernel(c_ref, w_ref, b_ref, o_ref):
    o_ref[...] = _dot(c_ref[...].astype(BF16), w_ref[...].astype(BF16)) + b_ref[...]


def _ada(c, w_ada, b_ada):
    b, d = c.shape
    n = w_ada.shape[1]
    return pl.pallas_call(
        _ada_kernel,
        grid=(n // d,),
        in_specs=[pl.BlockSpec((b, d), lambda j: (0, 0)),
                  pl.BlockSpec((d, d), lambda j: (0, j)),
                  pl.BlockSpec((1, d), lambda j: (0, j))],
        out_specs=pl.BlockSpec((b, d), lambda j: (0, j)),
        out_shape=jax.ShapeDtypeStruct((b, n), F32),
        name="ada",
    )(c, w_ada, b_ada.reshape(1, n))


N_K = 0
N_F = N_K + ATTN_W
N_GB = N_F + LANES
N_GC = N_GB + CONV_W
N_U = N_GC + CONV_W
N_ZB = N_U + CONV_W
N_GA = N_ZB + CONV_W
N_GB2 = N_GA + D_MODEL
N_END = N_GB2 + D_MODEL
T_Q = 0
T_V = T_Q + ATTN_W
T_ZA = T_V + ATTN_W
T_F = T_ZA + ATTN_W
T_END = T_F + 2 * HEADS

IN_SIZES = (ATTN_W, ATTN_W, ATTN_W, HEADS, ATTN_W, CONV_W, CONV_W, CONV_W, CONV_W, D_MODEL, D_MODEL)
IN_OFFS = tuple(sum(IN_SIZES[:i]) for i in range(len(IN_SIZES) + 1))
PREP_COLS = 256


def _prep_kernel(w_ref, wn_ref, wt_ref):
    rows = lambda i, j: w_ref[IN_OFFS[i]:IN_OFFS[j], :]
    f = rows(3, 4)
    zeros = lambda n: jnp.zeros((n, PREP_COLS), F32)
    fx = jnp.concatenate([f] * CUM_PARTS + [zeros(LO_LANE - ONES_LANE)] + [f] * CUM_PARTS
                         + [zeros(LANES - LO_LANE - ONES_LANE)], axis=0)
    wn_ref[...] = jnp.concatenate([rows(1, 2), fx, rows(5, 11)], axis=0).astype(BF16).T
    wt_ref[T_Q:T_V, :] = rows(0, 1).astype(BF16)
    wt_ref[T_V:T_ZA, :] = rows(2, 3).astype(BF16)
    wt_ref[T_ZA:T_F, :] = rows(4, 5).astype(BF16)
    wt_ref[T_F:T_END, :] = jnp.concatenate([f, f], axis=0).astype(BF16)


def _prep(w_t):
    n, d = w_t.shape
    return pl.pallas_call(
        _prep_kernel,
        grid=(d // PREP_COLS,),
        in_specs=[pl.BlockSpec((n, PREP_COLS), lambda i: (0, i))],
        out_specs=[pl.BlockSpec((PREP_COLS, N_END), lambda i: (i, 0)),
                   pl.BlockSpec((T_END, PREP_COLS), lambda i: (0, i))],
        out_shape=[jax.ShapeDtypeStruct((d, N_END), BF16),
                   jax.ShapeDtypeStruct((T_END, d), BF16)],
        compiler_params=pltpu.CompilerParams(vmem_limit_bytes=VMEM_LIMIT),
        name="prep",
    )(w_t)


def _proj_kernel(x_ref, xnext_ref, ada_ref, ng_ref, wn_ref, wt_ref, bfn_ref, bft_ref, gq_ref, gk_ref,
                 cw_ref, wb_ref, pm_ref, lt_ref, ut_ref,
                 k_ref, kx_ref, qt_ref, cq_ref, vt_ref, za_ref, sga_ref, mb_ref,
                 cn_ref, ct_ref, cu_ref, h_ref, gc_ref):
    s = pl.program_id(1)
    slot = lax.rem(s, 2)

    def modulated_norm(xr):
        x = xr[0]
        shift = ada_ref[0, :, 0:D_MODEL]
        scale = ada_ref[0, :, D_MODEL:2 * D_MODEL]
        xn = x * lax.rsqrt(jnp.mean(x * x, axis=-1, keepdims=True) + EPS)
        return (xn * ng_ref[...] * (1.0 + scale) + shift).astype(BF16)

    @pl.when(s == 0)
    def _():
        cn_ref[...] = jnp.zeros_like(cn_ref)
        ct_ref[...] = jnp.zeros_like(ct_ref)
        cu_ref[0:8, :] = jnp.zeros((8, CONV_W), F32)
        h0 = modulated_norm(x_ref)
        h_ref[0] = h0
        gc_ref[...] = _dot(h0, wn_ref[:, N_GC:N_U])

    nat = lambda lo, hi: _dot(h_ref[slot], wn_ref[:, lo:hi])
    trn = lambda lo, hi: _dot_nt(wt_ref[lo:hi, :], h_ref[slot])
    lane = lax.broadcasted_iota(jnp.int32, (TM, LANES), 1)
    in_hi = lane < ONES_LANE
    in_lo = (lane >= LO_LANE) & (lane < LO_LANE + ONES_LANE)

    gc = gc_ref[...]
    u = nat(N_U, N_ZB)
    kf = nat(N_K, N_GB)
    kraw = kf[:, 0:ATTN_W]
    flog = kf[:, ATTN_W:]

    cu = gc * u
    cu_ref[8:8 + TM, :] = cu
    conv = (cw_ref[2:3, :] * cu + cw_ref[1:2, :] * cu_ref[7:7 + TM, :]
            + cw_ref[0:1, :] * cu_ref[6:6 + TM, :])
    cu_ref[0:8, :] = cu[TM - 8:TM, :]
    gb = nat(N_GB, N_GC)
    zb = nat(N_ZB, N_GA)

    kk = (kraw * kraw).astype(BF16)
    lf = _log_sigmoid(flog + bfn_ref[...])
    lf_lo = lf - lf.astype(BF16).astype(F32)
    parts = jnp.where(in_hi, lf, jnp.where(in_lo, lf_lo, 0.0)).astype(BF16)
    kms = _dot(kk, pm_ref[...])
    csum = _dot(lt_ref[...], parts)
    ga = nat(N_GA, N_GB2)

    gb2 = nat(N_GB2, N_END)
    ob = (gb * conv * _silu(zb)).astype(BF16)
    mbr = _dot(ob, wb_ref[...])

    sga_ref[0] = _sigmoid(ga).astype(BF16)
    kn = (kraw * lax.rsqrt(kms + EPS) * gk_ref[...]).astype(BF16)
    for pair in range(HEADS // 2):
        k_ref[0, pair] = kn[:, pair * LANES:(pair + 1) * LANES]
    cum = csum + pltpu.roll(csum, LANES - LO_LANE, 1) + cn_ref[...]
    cn_ref[...] = cum[TM - 1:TM, :]
    r1 = cum - cum.astype(BF16).astype(F32)
    r2 = r1 - r1.astype(BF16).astype(F32)
    piece = jnp.where(lane < HEADS, cum, jnp.where(lane < 2 * HEADS, r1, r2))
    is_one = (lane >= ONES_LANE) & (lane < ONES_LANE + CUM_PARTS)
    kx_ref[0] = jnp.where(in_hi, -piece, jnp.where(is_one, 1.0, 0.0)).astype(BF16)
    zf = trn(T_ZA, T_END)
    qraw = trn(T_Q, T_V)

    mb_ref[0] = (_sigmoid(gb2) * mbr).astype(BF16)
    lft = _log_sigmoid(zf[ATTN_W:, :] + bft_ref[...])
    lft_lo = lft - lft.astype(BF16).astype(F32)
    row = lax.broadcasted_iota(jnp.int32, (2 * HEADS, TM), 0)
    ct = _dot(jnp.where(row < HEADS, lft, lft_lo).astype(BF16), ut_ref[...])

    za_ref[0] = _silu(zf[0:ATTN_W, :]).astype(BF16)
    ct = ct[0:HEADS, :] + ct[HEADS:2 * HEADS, :]
    carry = ct_ref[...]
    cq_ref[0] = ct[:, 0:TM] + jnp.concatenate([carry] * (TM // LANES), axis=1)
    ct_ref[...] = carry + ct[:, TM:TM + LANES]
    vraw = trn(T_V, T_ZA)

    q3 = qraw.reshape(HEADS, HEAD_DIM, TM)
    qn = q3 * lax.rsqrt(jnp.mean(q3 * q3, axis=1, keepdims=True) + EPS)
    qt_ref[0] = (qn.reshape(ATTN_W, TM) * gq_ref[...]).astype(BF16)

    vt = vraw.astype(BF16)
    for i in range(TM // TK):
        for hd in range(HEADS):
            vt_ref[0, i, hd, 0:HEAD_DIM, :] = (
                vt[hd * HEAD_DIM:(hd + 1) * HEAD_DIM, i * TK:(i + 1) * TK])
            vt_ref[0, i, hd, HEAD_DIM:V_ROWS, :] = jnp.ones((V_ROWS - HEAD_DIM, TK), BF16)

    h_next = modulated_norm(xnext_ref)
    h_ref[1 - slot] = h_next
    gc_ref[...] = _dot(h_next, wn_ref[:, N_GC:N_U])


def _proj(x, ada3, ng, wn, wt, bfn, bft, gq, gk, cw, wb, pm, lt, ut):
    b, s, d = x.shape
    const = lambda shape: pl.BlockSpec(shape, lambda bi, si: (0,) * len(shape))
    out_shape = [
        jax.ShapeDtypeStruct((b, HEADS // 2, s, LANES), BF16),
        jax.ShapeDtypeStruct((b, s, LANES), BF16),
        jax.ShapeDtypeStruct((b, ATTN_W, s), BF16),
        jax.ShapeDtypeStruct((b, HEADS, s), F32),
        jax.ShapeDtypeStruct((b, s // TK, HEADS, V_ROWS, TK), BF16),
        jax.ShapeDtypeStruct((b, ATTN_W, s), BF16),
        jax.ShapeDtypeStruct((b, s, d), BF16),
        jax.ShapeDtypeStruct((b, s, d), BF16),
    ]
    out_specs = [
        pl.BlockSpec((1, HEADS // 2, TM, LANES), lambda bi, si: (bi, 0, si, 0)),
        pl.BlockSpec((1, TM, LANES), lambda bi, si: (bi, si, 0)),
        pl.BlockSpec((1, ATTN_W, TM), lambda bi, si: (bi, 0, si)),
        pl.BlockSpec((1, HEADS, TM), lambda bi, si: (bi, 0, si)),
        pl.BlockSpec((1, TM // TK, HEADS, V_ROWS, TK), lambda bi, si: (bi, si, 0, 0, 0)),
        pl.BlockSpec((1, ATTN_W, TM), lambda bi, si: (bi, 0, si)),
        pl.BlockSpec((1, TM, d), lambda bi, si: (bi, si, 0)),
        pl.BlockSpec((1, TM, d), lambda bi, si: (bi, si, 0)),
    ]
    last = s // TM - 1
    in_specs = [
        pl.BlockSpec((1, TM, d), lambda bi, si: (bi, si, 0)),
        pl.BlockSpec((1, TM, d), lambda bi, si: (bi, jnp.minimum(si + 1, last), 0)),
        pl.BlockSpec((1, 1, 3 * d), lambda bi, si: (bi, 0, 0)),
        const(ng.shape), const(wn.shape), const(wt.shape), const(bfn.shape), const(bft.shape),
        const(gq.shape), const(gk.shape), const(cw.shape), const(wb.shape), const(pm.shape),
        const(lt.shape), const(ut.shape),
    ]
    return pl.pallas_call(
        _proj_kernel,
        grid=(b, s // TM),
        in_specs=in_specs,
        out_specs=out_specs,
        out_shape=out_shape,
        scratch_shapes=[pltpu.VMEM((1, LANES), F32),
                        pltpu.VMEM((HEADS, LANES), F32),
                        pltpu.VMEM((TM + 8, CONV_W), F32),
                        pltpu.VMEM((2, TM, D_MODEL), BF16),
                        pltpu.VMEM((TM, CONV_W), F32)],
        compiler_params=pltpu.CompilerParams(
            dimension_semantics=("arbitrary", "arbitrary"),
            vmem_limit_bytes=VMEM_LIMIT),
        name="proj",
    )(x, x, ada3, ng, wn, wt, bfn, bft, gq, gk, cw, wb, pm, lt, ut)


def _attn_kernel(cend_ref, bound_ref, qt_ref, cq_ref, k_ref, kx_ref, vt_ref, za_ref, sga_ref, mb_ref,
                 x_ref, gate_ref, wa_ref, wo_ref, o_ref, rhs_ref, m_ref, acc_ref, *, running_max):
    bi = pl.program_id(0)
    qi = pl.program_id(1)
    blocks_per_tile = TQ // TK
    n_blocks = kx_ref.shape[1] // TK
    i32 = jnp.int32

    last_before = qi * blocks_per_tile - 1
    slack = LOGIT_BOUND_MARGIN * 2.0 * bound_ref[0] + SKIP_LOG_WEIGHT

    first = []
    for h in range(HEADS):
        base = (bi * HEADS + h) * n_blocks
        threshold = cend_ref[base + jnp.maximum(last_before, 0)] + slack
        skipped = i32(0)
        for j in range(n_blocks - blocks_per_tile - 1):
            skipped = skipped + (cend_ref[base + j] >= threshold).astype(i32)
        first.append(skipped)
    upto, shift, total = [], [], i32(0)
    for h in range(HEADS):
        shift.append(first[h] - total)
        total = total + (qi * blocks_per_tile - first[h])
        upto.append(total)

    row = lax.broadcasted_iota(jnp.int32, (LANES, TQ), 0)
    for h in range(HEADS):
        pair, half = divmod(h, 2)
        qp = qt_ref[0, pair * LANES:(pair + 1) * LANES, :]
        mine = (row >= half * HEAD_DIM) & (row < (half + 1) * HEAD_DIM)
        rhs_ref[h, 0:LANES, :] = qp * jnp.where(mine, 1.0, 0.0).astype(BF16)
        cq = cq_ref[0, h:h + 1, :]
        r1 = cq - cq.astype(BF16).astype(F32)
        r2 = r1 - r1.astype(BF16).astype(F32)
        sel = (row == h) | (row == HEADS + h) | (row == 2 * HEADS + h)
        f = jnp.where(row == ONES_LANE, cq,
                      jnp.where(row == ONES_LANE + 1, r1,
                                jnp.where(row == ONES_LANE + 2, r2, 0.0)))
        rhs_ref[h, LANES:2 * LANES, :] = jnp.where(sel, 1.0, f).astype(BF16)
    rhs_ref[HEADS] = jnp.zeros(rhs_ref.shape[1:], BF16)

    if running_max:
        m_ref[...] = jnp.full(m_ref.shape, NEG_BIG, F32)
    acc_ref[...] = jnp.zeros_like(acc_ref)

    def scores(j, h, q_lo):
        start = pl.multiple_of(j * TK, TK)
        pair = h // 2 if isinstance(h, int) else jnp.minimum(h, HEADS - 1) // 2
        lhs = jnp.concatenate([k_ref[0, pair, pl.ds(start, TK), :],
                               kx_ref[0, pl.ds(start, TK), :]], axis=1)
        return _dot(lhs, rhs_ref[h, :, q_lo:TQ])

    def run_units(units):
        ahead = [scores(j, h, q_lo) for j, h, _, q_lo in units[:SCORE_LOOKAHEAD]]
        for i, (j, h, masked, q_lo) in enumerate(units):
            sc = ahead.pop(0)
            if i + SCORE_LOOKAHEAD < len(units):
                jn, hn, _, qn = units[i + SCORE_LOOKAHEAD]
                ahead.append(scores(jn, hn, qn))
            if masked:
                kpos = lax.broadcasted_iota(jnp.int32, sc.shape, 0)
                qpos = lax.broadcasted_iota(jnp.int32, sc.shape, 1)
                sc = jnp.where(kpos <= qpos, sc, NEG_BIG)
            hv = h if isinstance(h, int) else jnp.minimum(h, HEADS - 1)
            if running_max:
                m_prev = m_ref[h, :, q_lo:TQ]
                m_new = jnp.maximum(m_prev, jnp.max(sc, axis=0, keepdims=True))
                m_ref[h, :, q_lo:TQ] = m_new
                p = jnp.exp(sc - m_new).astype(BF16)
                acc_ref[h, :, q_lo:TQ] = (jnp.exp(m_prev - m_new) * acc_ref[h, :, q_lo:TQ]
                                          + _dot(vt_ref[0, j, hv], p))
            else:
                p = jnp.exp(sc).astype(BF16)
                acc_ref[h, :, q_lo:TQ] += _dot(vt_ref[0, j, hv], p)

    def list_units(start, count):
        units = []
        for t in range(count):
            u = start + t
            h = i32(0)
            off = shift[0]
            for g in range(HEADS):
                past = u >= upto[g]
                h = h + past.astype(i32)
                if g + 1 < HEADS:
                    off = jnp.where(past, shift[g + 1], off)
            j = jnp.where(u >= total, 0, u + off)
            units.append((j, h, False, 0))
        return units

    def long_body(i, carry):
        run_units(list_units(i * UNITS_PER_LONG_ITER, UNITS_PER_LONG_ITER))
        return carry

    n_short = (total + UNITS_PER_ITER - 1) // UNITS_PER_ITER
    n_long = n_short // (UNITS_PER_LONG_ITER // UNITS_PER_ITER)
    lax.fori_loop(0, n_long, long_body, 0)
    done = n_long * UNITS_PER_LONG_ITER

    def short_body(i, carry):
        run_units(list_units(done + i * UNITS_PER_ITER, UNITS_PER_ITER))
        return carry

    lax.fori_loop(0, n_short - n_long * (UNITS_PER_LONG_ITER // UNITS_PER_ITER), short_body, 0)
    run_units([(qi * blocks_per_tile + t, h, True, t * TK)
               for t in range(blocks_per_tile) for h in range(HEADS)])

    acc = jnp.concatenate(
        [acc_ref[h, 0:HEAD_DIM, :] * (1.0 / acc_ref[h, HEAD_DIM:HEAD_DIM + 1, :])
         for h in range(HEADS)], axis=0)
    at = (acc * za_ref[0].astype(F32)).astype(BF16)
    ya = _dot_tn(at, wa_ref[...])
    merged = (sga_ref[0].astype(F32) * ya + mb_ref[0].astype(F32)).astype(BF16)
    o_ref[0] = x_ref[0] + gate_ref[0] * _dot(merged, wo_ref[...])


def _attn(running_max, cend, bound, qt, cq, k, kx, vt, za, sga, mb, x, gate3, wa, wo):
    b, s, d = x.shape
    const = lambda shape: pl.BlockSpec(shape, lambda bi, qi: (0,) * len(shape))
    in_specs = [
        pl.BlockSpec(memory_space=pltpu.SMEM),
        pl.BlockSpec(memory_space=pltpu.SMEM),
        pl.BlockSpec((1, ATTN_W, TQ), lambda bi, qi: (bi, 0, qi)),
        pl.BlockSpec((1, HEADS, TQ), lambda bi, qi: (bi, 0, qi)),
        pl.BlockSpec((1, HEADS // 2, s, LANES), lambda bi, qi: (bi, 0, 0, 0)),
        pl.BlockSpec((1, s, LANES), lambda bi, qi: (bi, 0, 0)),
        pl.BlockSpec((1, s // TK, HEADS, V_ROWS, TK), lambda bi, qi: (bi, 0, 0, 0, 0)),
        pl.BlockSpec((1, ATTN_W, TQ), lambda bi, qi: (bi, 0, qi)),
        pl.BlockSpec((1, TQ, d), lambda bi, qi: (bi, qi, 0)),
        pl.BlockSpec((1, TQ, d), lambda bi, qi: (bi, qi, 0)),
        pl.BlockSpec((1, TQ, d), lambda bi, qi: (bi, qi, 0)),
        pl.BlockSpec((1, 1, d), lambda bi, qi: (bi, 0, 0)),
        const(wa.shape), const(wo.shape),
    ]
    return pl.pallas_call(
        functools.partial(_attn_kernel, running_max=running_max),
        grid=(b, s // TQ),
        in_specs=in_specs,
        out_specs=pl.BlockSpec((1, TQ, d), lambda bi, qi: (bi, qi, 0)),
        out_shape=jax.ShapeDtypeStruct((b, s, d), F32),
        scratch_shapes=[pltpu.VMEM((HEADS + 1, 2 * LANES, TQ), BF16),
                        pltpu.VMEM((HEADS + 1, 1, TQ), F32),
                        pltpu.VMEM((HEADS + 1, V_ROWS, TQ), F32)],
        compiler_params=pltpu.CompilerParams(
            dimension_semantics=("arbitrary", "arbitrary"),
            vmem_limit_bytes=VMEM_LIMIT),
        name="attn_running_max" if running_max else "attn",
    )(cend, bound, qt, cq, k, kx, vt, za, sga, mb, x, gate3, wa, wo)


def _layer(x, c, w_ada, b_ada, norm_g, w_in, b_f, q_norm_g, k_norm_g, conv_w,
           w_attn_out, w_conv_out, w_o):
    b, s, d = x.shape
    ada = _ada(c, w_ada, b_ada)
    ada3 = ada.reshape(b, 1, 3 * d)
    gate3 = ada3[:, :, 2 * d:]

    def hi_lo_lanes(v):
        rep = jnp.concatenate([v] * CUM_PARTS, axis=-1)
        gap = jnp.zeros(v.shape[:-1] + (LO_LANE - ONES_LANE,), v.dtype)
        tail = jnp.zeros(v.shape[:-1] + (LANES - LO_LANE - ONES_LANE,), v.dtype)
        return jnp.concatenate([rep, gap, rep, tail], axis=-1)

    wn, wt = _prep(w_in.T)

    bfn = hi_lo_lanes(b_f.reshape(1, HEADS))
    bft = jnp.broadcast_to(jnp.concatenate([b_f, b_f])[:, None], (2 * HEADS, TM))
    gq = (jnp.tile(q_norm_g, HEADS) * HEAD_DIM ** -0.5).reshape(ATTN_W, 1)
    gk = jnp.tile(k_norm_g, HEADS).reshape(1, ATTN_W)

    head_of = np.arange(ATTN_W) // HEAD_DIM
    pm = jnp.asarray(np.where(head_of[:, None] == head_of[None, :], 1.0 / HEAD_DIM, 0.0), BF16)
    tok = np.arange(TM)
    lower = (tok[None, :] <= tok[:, None]).astype(np.float32)
    lt = jnp.asarray(lower, BF16)
    ut = jnp.asarray(np.concatenate([lower.T, np.ones((TM, LANES), np.float32)], axis=1), BF16)

    k, kx, qt, cq, vt, za, sga, mb = _proj(
        x, ada3, norm_g.reshape(1, d), wn, wt, bfn, bft, gq, gk, conv_w,
        w_conv_out.astype(BF16), pm, lt, ut)
    logit_bound = HEAD_DIM ** 0.5 * jnp.max(jnp.abs(q_norm_g)) * jnp.max(jnp.abs(k_norm_g))
    cend = cq[:, :, TK - 1::TK].reshape(-1)
    args = (cend, logit_bound.reshape(1), qt, cq, k, kx, vt, za, sga, mb, x, gate3,
            w_attn_out.astype(BF16), w_o.astype(BF16))
    return lax.cond(logit_bound <= MAX_RAW_LOGIT,
                    functools.partial(_attn, False), functools.partial(_attn, True), *args)


@jax.jit
def kernel(x, c, w_ada, b_ada, norm_g, w_in, b_f, q_norm_g, k_norm_g, conv_w,
           w_attn_out, w_conv_out, w_o):
    for i in range(w_ada.shape[0]):
        x = _layer(x, c, w_ada[i], b_ada[i], norm_g[i], w_in[i], b_f[i], q_norm_g[i],
                   k_norm_g[i], conv_w[i], w_attn_out[i], w_conv_out[i], w_o[i])
    return x
```

```python
import functools

import jax
import jax.numpy as jnp
import numpy as np
from jax import lax
from jax.experimental import pallas as pl
from jax.experimental.pallas import tpu as pltpu

D_MODEL = 1024
HEADS = 8
HEAD_DIM = 64
ATTN_W = HEADS * HEAD_DIM
CONV_W = 512
CONV_K = 3
EPS = 1e-6

LANES = 128
TM = 512
TQ = 512
TK = 256
CUM_PARTS = 3
ONES_LANE = CUM_PARTS * HEADS
LO_LANE = 32
V_ROWS = HEAD_DIM + 16
SCORE_LOOKAHEAD = 4
UNITS_PER_LONG_ITER = 16
UNITS_PER_ITER = 8
NEG_BIG = -1e30
MAX_RAW_LOGIT = 40.0
SKIP_LOG_WEIGHT = 30.0
LOGIT_BOUND_MARGIN = 1.05
VMEM_LIMIT = 56 * 1024 * 1024

F32 = jnp.float32
BF16 = jnp.bfloat16


def _log_sigmoid(x):
    return jnp.minimum(x, 0.0) - jnp.log(1.0 + jnp.exp(-jnp.abs(x)))


def _sigmoid(x):
    return 0.5 * jnp.tanh(0.5 * x) + 0.5


def _silu(x):
    hx = 0.5 * x
    return hx * jnp.tanh(hx) + hx


def _dot(a, b):
    return jnp.dot(a, b, preferred_element_type=F32)


def _dot_nt(a, b):
    return lax.dot_general(a, b, (((1,), (1,)), ((), ())), preferred_element_type=F32)


def _dot_tn(a, b):
    return lax.dot_general(a, b, (((0,), (0,)), ((), ())), preferred_element_type=F32)


def _ada_kernel(c_ref, w_ref, b_ref, o_ref):
    o_ref[:, 0, :] = _dot(c_ref[...].astype(BF16), w_ref[...].astype(BF16)) + b_ref[...]


def _ada(c, w_ada, b_ada):
    b, d = c.shape
    n = w_ada.shape[1]
    return pl.pallas_call(
        _ada_kernel,
        grid=(n // d,),
        in_specs=[pl.BlockSpec((b, d), lambda j: (0, 0)),
                  pl.BlockSpec((d, d), lambda j: (0, j)),
                  pl.BlockSpec((1, d), lambda j: (0, j))],
        out_specs=pl.BlockSpec((b, 1, d), lambda j: (0, 0, j)),
        out_shape=jax.ShapeDtypeStruct((b, 1, n), F32),
        name="ada",
    )(c, w_ada, b_ada.reshape(1, n))


N_K = 0
N_F = N_K + ATTN_W
N_GB = N_F + LANES
N_GC = N_GB + CONV_W
N_U = N_GC + CONV_W
N_ZB = N_U + CONV_W
N_GA = N_ZB + CONV_W
N_GB2 = N_GA + D_MODEL
N_END = N_GB2 + D_MODEL
T_Q = 0
T_V = T_Q + ATTN_W
T_ZA = T_V + ATTN_W
T_F = T_ZA + ATTN_W
T_END = T_F + 2 * HEADS

IN_SIZES = (ATTN_W, ATTN_W, ATTN_W, HEADS, ATTN_W, CONV_W, CONV_W, CONV_W, CONV_W, D_MODEL, D_MODEL)
IN_OFFS = tuple(sum(IN_SIZES[:i]) for i in range(len(IN_SIZES) + 1))
PREP_COLS = 256


def _prep_kernel(w_ref, wa_ref, wb_ref, wo_ref, wn_ref, wt_ref, wa16_ref, wb16_ref, wo16_ref):
    rows = lambda i, j: w_ref[IN_OFFS[i]:IN_OFFS[j], :]
    f = rows(3, 4)
    zeros = lambda n: jnp.zeros((n, PREP_COLS), F32)
    fx = jnp.concatenate([f] * CUM_PARTS + [zeros(LO_LANE - ONES_LANE)] + [f] * CUM_PARTS
                         + [zeros(LANES - LO_LANE - ONES_LANE)], axis=0)
    wn_ref[...] = jnp.concatenate([rows(1, 2), fx, rows(5, 11)], axis=0).astype(BF16).T
    wt_ref[T_Q:T_V, :] = rows(0, 1).astype(BF16)
    wt_ref[T_V:T_ZA, :] = rows(2, 3).astype(BF16)
    wt_ref[T_ZA:T_F, :] = rows(4, 5).astype(BF16)
    wt_ref[T_F:T_END, :] = jnp.concatenate([f, f], axis=0).astype(BF16)
    wa16_ref[...] = wa_ref[...].astype(BF16)
    wb16_ref[...] = wb_ref[...].astype(BF16)
    wo16_ref[...] = wo_ref[...].astype(BF16)


def _prep(w_t, w_attn_out, w_conv_out, w_o):
    n, d = w_t.shape
    steps = d // PREP_COLS
    row_block = lambda w: pl.BlockSpec((w.shape[0] // steps, w.shape[1]), lambda i: (i, 0))
    sides = (w_attn_out, w_conv_out, w_o)
    return pl.pallas_call(
        _prep_kernel,
        grid=(steps,),
        in_specs=[pl.BlockSpec((n, PREP_COLS), lambda i: (0, i))] + [row_block(w) for w in sides],
        out_specs=[pl.BlockSpec((PREP_COLS, N_END), lambda i: (i, 0)),
                   pl.BlockSpec((T_END, PREP_COLS), lambda i: (0, i))] + [row_block(w) for w in sides],
        out_shape=[jax.ShapeDtypeStruct((d, N_END), BF16),
                   jax.ShapeDtypeStruct((T_END, d), BF16)]
                  + [jax.ShapeDtypeStruct(w.shape, BF16) for w in sides],
        compiler_params=pltpu.CompilerParams(vmem_limit_bytes=VMEM_LIMIT),
        name="prep",
    )(w_t, *sides)


def _proj_kernel(xfirst_ref, xnext_ref, ada_ref, adanext_ref, ng_ref, wn_ref, wt_ref, bfn_ref, bft_ref, gq_ref, gk_ref,
                 cw_ref, wb_ref, pm_ref, lt_ref, ut_ref,
                 k_ref, kx_ref, qt_ref, cq_ref, vt_ref, za_ref, sga_ref, mb_ref,
                 cn_ref, ct_ref, cu_ref, h_ref, gc_ref):
    s = pl.program_id(1)
    slot = lax.rem(s, 2)

    def modulated_norm(xr, ar):
        x = xr[0]
        shift = ar[0, :, 0:D_MODEL]
        scale = ar[0, :, D_MODEL:2 * D_MODEL]
        xn = x * lax.rsqrt(jnp.mean(x * x, axis=-1, keepdims=True) + EPS)
        return (xn * ng_ref[...] * (1.0 + scale) + shift).astype(BF16)

    @pl.when(s == 0)
    def _():
        cn_ref[...] = jnp.zeros_like(cn_ref)
        ct_ref[...] = jnp.zeros_like(ct_ref)
        cu_ref[0:8, :] = jnp.zeros((8, CONV_W), F32)

    @pl.when((pl.program_id(0) == 0) & (s == 0))
    def _():
        h0 = modulated_norm(xfirst_ref, ada_ref)
        h_ref[0] = h0
        gc_ref[...] = _dot(h0, wn_ref[:, N_GC:N_U])

    nat = lambda lo, hi: _dot(h_ref[slot], wn_ref[:, lo:hi])
    trn = lambda lo, hi: _dot_nt(wt_ref[lo:hi, :], h_ref[slot])
    lane = lax.broadcasted_iota(jnp.int32, (TM, LANES), 1)
    in_hi = lane < ONES_LANE
    in_lo = (lane >= LO_LANE) & (lane < LO_LANE + ONES_LANE)

    gc = gc_ref[...]
    u = nat(N_U, N_ZB)
    kf = nat(N_K, N_GB)
    kraw = kf[:, 0:ATTN_W]
    flog = kf[:, ATTN_W:]

    cu = gc * u
    cu_ref[8:8 + TM, :] = cu
    conv = (cw_ref[2:3, :] * cu + cw_ref[1:2, :] * cu_ref[7:7 + TM, :]
            + cw_ref[0:1, :] * cu_ref[6:6 + TM, :])
    cu_ref[0:8, :] = cu[TM - 8:TM, :]
    gb = nat(N_GB, N_GC)
    zb = nat(N_ZB, N_GA)

    kk = (kraw * kraw).astype(BF16)
    lf = _log_sigmoid(flog + bfn_ref[...])
    lf_lo = lf - lf.astype(BF16).astype(F32)
    parts = jnp.where(in_hi, lf, jnp.where(in_lo, lf_lo, 0.0)).astype(BF16)
    kms = _dot(kk, pm_ref[...])
    csum = _dot(lt_ref[...], parts)
    ga = nat(N_GA, N_GB2)

    gb2 = nat(N_GB2, N_END)
    ob = (gb * conv * _silu(zb)).astype(BF16)
    mbr = _dot(ob, wb_ref[...])

    sga_ref[0] = _sigmoid(ga).astype(BF16)
    kn = (kraw * lax.rsqrt(kms + EPS) * gk_ref[...]).astype(BF16)
    for pair in range(HEADS // 2):
        k_ref[0, pair] = kn[:, pair * LANES:(pair + 1) * LANES]
    cum = csum + pltpu.roll(csum, LANES - LO_LANE, 1) + cn_ref[...]
    cn_ref[...] = cum[TM - 1:TM, :]
    r1 = cum - cum.astype(BF16).astype(F32)
    r2 = r1 - r1.astype(BF16).astype(F32)
    piece = jnp.where(lane < HEADS, cum, jnp.where(lane < 2 * HEADS, r1, r2))
    is_one = (lane >= ONES_LANE) & (lane < ONES_LANE + CUM_PARTS)
    kx_ref[0] = jnp.where(in_hi, -piece, jnp.where(is_one, 1.0, 0.0)).astype(BF16)
    zf = trn(T_ZA, T_END)
    qraw = trn(T_Q, T_V)

    mb_ref[0] = (_sigmoid(gb2) * mbr).astype(BF16)
    lft = _log_sigmoid(zf[ATTN_W:, :] + bft_ref[...])
    lft_lo = lft - lft.astype(BF16).astype(F32)
    row = lax.broadcasted_iota(jnp.int32, (2 * HEADS, TM), 0)
    ct = _dot(jnp.where(row < HEADS, lft, lft_lo).astype(BF16), ut_ref[...])

    za_ref[0] = _silu(zf[0:ATTN_W, :]).astype(BF16)
    ct = ct[0:HEADS, :] + ct[HEADS:2 * HEADS, :]
    carry = ct_ref[...]
    cq_ref[0] = ct[:, 0:TM] + jnp.concatenate([carry] * (TM // LANES), axis=1)
    ct_ref[...] = carry + ct[:, TM:TM + LANES]
    vraw = trn(T_V, T_ZA)

    q3 = qraw.reshape(HEADS, HEAD_DIM, TM)
    qn = q3 * lax.rsqrt(jnp.mean(q3 * q3, axis=1, keepdims=True) + EPS)
    qt_ref[0] = (qn.reshape(ATTN_W, TM) * gq_ref[...]).astype(BF16)

    vt = vraw.astype(BF16)
    for i in range(TM // TK):
        for hd in range(HEADS):
            vt_ref[0, i, hd, 0:HEAD_DIM, :] = (
                vt[hd * HEAD_DIM:(hd + 1) * HEAD_DIM, i * TK:(i + 1) * TK])
            vt_ref[0, i, hd, HEAD_DIM:V_ROWS, :] = jnp.ones((V_ROWS - HEAD_DIM, TK), BF16)

    h_next = modulated_norm(xnext_ref, adanext_ref)
    h_ref[1 - slot] = h_next
    gc_ref[...] = _dot(h_next, wn_ref[:, N_GC:N_U])


def _proj(x, ada3, ng, wn, wt, bfn, bft, gq, gk, cw, wb, pm, lt, ut):
    b, s, d = x.shape
    const = lambda shape: pl.BlockSpec(shape, lambda bi, si: (0,) * len(shape))
    out_shape = [
        jax.ShapeDtypeStruct((b, HEADS // 2, s, LANES), BF16),
        jax.ShapeDtypeStruct((b, s, LANES), BF16),
        jax.ShapeDtypeStruct((b, ATTN_W, s), BF16),
        jax.ShapeDtypeStruct((b, HEADS, s), F32),
        jax.ShapeDtypeStruct((b, s // TK, HEADS, V_ROWS, TK), BF16),
        jax.ShapeDtypeStruct((b, ATTN_W, s), BF16),
        jax.ShapeDtypeStruct((b, s, d), BF16),
        jax.ShapeDtypeStruct((b, s, d), BF16),
    ]
    out_specs = [
        pl.BlockSpec((1, HEADS // 2, TM, LANES), lambda bi, si: (bi, 0, si, 0)),
        pl.BlockSpec((1, TM, LANES), lambda bi, si: (bi, si, 0)),
        pl.BlockSpec((1, ATTN_W, TM), lambda bi, si: (bi, 0, si)),
        pl.BlockSpec((1, HEADS, TM), lambda bi, si: (bi, 0, si)),
        pl.BlockSpec((1, TM // TK, HEADS, V_ROWS, TK), lambda bi, si: (bi, si, 0, 0, 0)),
        pl.BlockSpec((1, ATTN_W, TM), lambda bi, si: (bi, 0, si)),
        pl.BlockSpec((1, TM, d), lambda bi, si: (bi, si, 0)),
        pl.BlockSpec((1, TM, d), lambda bi, si: (bi, si, 0)),
    ]
    tiles = s // TM
    assert tiles % 2 == 0
    next_b = lambda bi, si: jnp.minimum(bi + (si + 1) // tiles, b - 1)
    next_s = lambda bi, si: jnp.where((bi == b - 1) & (si == tiles - 1), si, (si + 1) % tiles)
    in_specs = [
        pl.BlockSpec((1, TM, d), lambda bi, si: (0, 0, 0)),
        pl.BlockSpec((1, TM, d), lambda bi, si: (next_b(bi, si), next_s(bi, si), 0)),
        pl.BlockSpec((1, 1, 3 * d), lambda bi, si: (bi, 0, 0)),
        pl.BlockSpec((1, 1, 3 * d), lambda bi, si: (next_b(bi, si), 0, 0)),
        const(ng.shape), const(wn.shape), const(wt.shape), const(bfn.shape), const(bft.shape),
        const(gq.shape), const(gk.shape), const(cw.shape), const(wb.shape), const(pm.shape),
        const(lt.shape), const(ut.shape),
    ]
    return pl.pallas_call(
        _proj_kernel,
        grid=(b, s // TM),
        in_specs=in_specs,
        out_specs=out_specs,
        out_shape=out_shape,
        scratch_shapes=[pltpu.VMEM((1, LANES), F32),
                        pltpu.VMEM((HEADS, LANES), F32),
                        pltpu.VMEM((TM + 8, CONV_W), F32),
                        pltpu.VMEM((2, TM, D_MODEL), BF16),
                        pltpu.VMEM((TM, CONV_W), F32)],
        compiler_params=pltpu.CompilerParams(
            dimension_semantics=("arbitrary", "arbitrary"),
            vmem_limit_bytes=VMEM_LIMIT),
        name="proj",
    )(x, x, ada3, ada3, ng, wn, wt, bfn, bft, gq, gk, cw, wb, pm, lt, ut)


def _attn_kernel(cend_ref, bound_ref, qt_ref, cq_ref, k_ref, kx_ref, vt_ref, za_ref, sga_ref, mb_ref,
                 x_ref, gate_ref, wa_ref, wo_ref, o_ref, rhs_ref, m_ref, acc_ref, *, running_max):
    bi = pl.program_id(0)
    qi = pl.program_id(1)
    blocks_per_tile = TQ // TK
    n_blocks = kx_ref.shape[1] // TK
    i32 = jnp.int32

    last_before = qi * blocks_per_tile - 1
    slack = LOGIT_BOUND_MARGIN * 2.0 * bound_ref[0] + SKIP_LOG_WEIGHT

    first = []
    for h in range(HEADS):
        base = (bi * HEADS + h) * n_blocks
        threshold = cend_ref[base + jnp.maximum(last_before, 0)] + slack
        skipped = i32(0)
        for j in range(n_blocks - blocks_per_tile - 1):
            skipped = skipped + (cend_ref[base + j] >= threshold).astype(i32)
        first.append(skipped)
    upto, shift, total = [], [], i32(0)
    for h in range(HEADS):
        shift.append(first[h] - total)
        total = total + (qi * blocks_per_tile - first[h])
        upto.append(total)

    row = lax.broadcasted_iota(jnp.int32, (LANES, TQ), 0)
    for h in range(HEADS):
        pair, half = divmod(h, 2)
        qp = qt_ref[0, pair * LANES:(pair + 1) * LANES, :]
        mine = (row >= half * HEAD_DIM) & (row < (half + 1) * HEAD_DIM)
        rhs_ref[h, 0:LANES, :] = qp * jnp.where(mine, 1.0, 0.0).astype(BF16)
        cq = cq_ref[0, h:h + 1, :]
        r1 = cq - cq.astype(BF16).astype(F32)
        r2 = r1 - r1.astype(BF16).astype(F32)
        sel = (row == h) | (row == HEADS + h) | (row == 2 * HEADS + h)
        f = jnp.where(row == ONES_LANE, cq,
                      jnp.where(row == ONES_LANE + 1, r1,
                                jnp.where(row == ONES_LANE + 2, r2, 0.0)))
        rhs_ref[h, LANES:2 * LANES, :] = jnp.where(sel, 1.0, f).astype(BF16)
    rhs_ref[HEADS] = jnp.zeros(rhs_ref.shape[1:], BF16)

    if running_max:
        m_ref[...] = jnp.full(m_ref.shape, NEG_BIG, F32)
    acc_ref[...] = jnp.zeros_like(acc_ref)

    def scores(j, h, q_lo):
        start = pl.multiple_of(j * TK, TK)
        pair = h // 2 if isinstance(h, int) else jnp.minimum(h, HEADS - 1) // 2
        lhs = jnp.concatenate([k_ref[0, pair, pl.ds(start, TK), :],
                               kx_ref[0, pl.ds(start, TK), :]], axis=1)
        return _dot(lhs, rhs_ref[h, :, q_lo:TQ])

    def run_units(units):
        ahead = [scores(j, h, q_lo) for j, h, _, q_lo in units[:SCORE_LOOKAHEAD]]
        for i, (j, h, masked, q_lo) in enumerate(units):
            sc = ahead.pop(0)
            if i + SCORE_LOOKAHEAD < len(units):
                jn, hn, _, qn = units[i + SCORE_LOOKAHEAD]
                ahead.append(scores(jn, hn, qn))
            if masked:
                kpos = lax.broadcasted_iota(jnp.int32, sc.shape, 0)
                qpos = lax.broadcasted_iota(jnp.int32, sc.shape, 1)
                sc = jnp.where(kpos <= qpos, sc, NEG_BIG)
            hv = h if isinstance(h, int) else jnp.minimum(h, HEADS - 1)
            if running_max:
                m_prev = m_ref[h, :, q_lo:TQ]
                m_new = jnp.maximum(m_prev, jnp.max(sc, axis=0, keepdims=True))
                m_ref[h, :, q_lo:TQ] = m_new
                p = jnp.exp(sc - m_new).astype(BF16)
                acc_ref[h, :, q_lo:TQ] = (jnp.exp(m_prev - m_new) * acc_ref[h, :, q_lo:TQ]
                                          + _dot(vt_ref[0, j, hv], p))
            else:
                p = jnp.exp(sc).astype(BF16)
                acc_ref[h, :, q_lo:TQ] += _dot(vt_ref[0, j, hv], p)

    def list_units(start, count):
        units = []
        for t in range(count):
            u = start + t
            h = i32(0)
            off = shift[0]
            for g in range(HEADS):
                past = u >= upto[g]
                h = h + past.astype(i32)
                if g + 1 < HEADS:
                    off = jnp.where(past, shift[g + 1], off)
            j = jnp.where(u >= total, 0, u + off)
            units.append((j, h, False, 0))
        return units

    def long_body(i, carry):
        run_units(list_units(i * UNITS_PER_LONG_ITER, UNITS_PER_LONG_ITER))
        return carry

    n_short = (total + UNITS_PER_ITER - 1) // UNITS_PER_ITER
    n_long = n_short // (UNITS_PER_LONG_ITER // UNITS_PER_ITER)
    lax.fori_loop(0, n_long, long_body, 0)
    done = n_long * UNITS_PER_LONG_ITER

    def short_body(i, carry):
        run_units(list_units(done + i * UNITS_PER_ITER, UNITS_PER_ITER))
        return carry

    lax.fori_loop(0, n_short - n_long * (UNITS_PER_LONG_ITER // UNITS_PER_ITER), short_body, 0)
    run_units([(qi * blocks_per_tile + t, h, True, t * TK)
               for t in range(blocks_per_tile) for h in range(HEADS)])

    acc = jnp.concatenate(
        [acc_ref[h, 0:HEAD_DIM, :] * (1.0 / acc_ref[h, HEAD_DIM:HEAD_DIM + 1, :])
         for h in range(HEADS)], axis=0)
    at = (acc * za_ref[0].astype(F32)).astype(BF16)
    ya = _dot_tn(at, wa_ref[...])
    merged = (sga_ref[0].astype(F32) * ya + mb_ref[0].astype(F32)).astype(BF16)
    o_ref[0] = x_ref[0] + gate_ref[0] * _dot(merged, wo_ref[...])


def _attn(running_max, cend, bound, qt, cq, k, kx, vt, za, sga, mb, x, ada3, wa, wo):
    b, s, d = x.shape
    const = lambda shape: pl.BlockSpec(shape, lambda bi, qi: (0,) * len(shape))
    in_specs = [
        pl.BlockSpec(memory_space=pltpu.SMEM),
        pl.BlockSpec(memory_space=pltpu.SMEM),
        pl.BlockSpec((1, ATTN_W, TQ), lambda bi, qi: (bi, 0, qi)),
        pl.BlockSpec((1, HEADS, TQ), lambda bi, qi: (bi, 0, qi)),
        pl.BlockSpec((1, HEADS // 2, s, LANES), lambda bi, qi: (bi, 0, 0, 0)),
        pl.BlockSpec((1, s, LANES), lambda bi, qi: (bi, 0, 0)),
        pl.BlockSpec((1, s // TK, HEADS, V_ROWS, TK), lambda bi, qi: (bi, 0, 0, 0, 0)),
        pl.BlockSpec((1, ATTN_W, TQ), lambda bi, qi: (bi, 0, qi)),
        pl.BlockSpec((1, TQ, d), lambda bi, qi: (bi, qi, 0)),
        pl.BlockSpec((1, TQ, d), lambda bi, qi: (bi, qi, 0)),
        pl.BlockSpec((1, TQ, d), lambda bi, qi: (bi, qi, 0)),
        pl.BlockSpec((1, 1, d), lambda bi, qi: (bi, 0, 2)),
        const(wa.shape), const(wo.shape),
    ]
    return pl.pallas_call(
        functools.partial(_attn_kernel, running_max=running_max),
        grid=(b, s // TQ),
        in_specs=in_specs,
        out_specs=pl.BlockSpec((1, TQ, d), lambda bi, qi: (bi, qi, 0)),
        out_shape=jax.ShapeDtypeStruct((b, s, d), F32),
        scratch_shapes=[pltpu.VMEM((HEADS + 1, 2 * LANES, TQ), BF16),
                        pltpu.VMEM((HEADS + 1, 1, TQ), F32),
                        pltpu.VMEM((HEADS + 1, V_ROWS, TQ), F32)],
        compiler_params=pltpu.CompilerParams(
            dimension_semantics=("arbitrary", "arbitrary"),
            vmem_limit_bytes=VMEM_LIMIT),
        name="attn_running_max" if running_max else "attn",
    )(cend, bound, qt, cq, k, kx, vt, za, sga, mb, x, ada3, wa, wo)


def _layer(x, c, w_ada, b_ada, norm_g, w_in, b_f, q_norm_g, k_norm_g, conv_w,
           w_attn_out, w_conv_out, w_o):
    b, s, d = x.shape
    ada3 = _ada(c, w_ada, b_ada)

    def hi_lo_lanes(v):
        rep = jnp.concatenate([v] * CUM_PARTS, axis=-1)
        gap = jnp.zeros(v.shape[:-1] + (LO_LANE - ONES_LANE,), v.dtype)
        tail = jnp.zeros(v.shape[:-1] + (LANES - LO_LANE - ONES_LANE,), v.dtype)
        return jnp.concatenate([rep, gap, rep, tail], axis=-1)

    wn, wt, wa, wb, wo = _prep(w_in.T, w_attn_out, w_conv_out, w_o)

    bfn = hi_lo_lanes(b_f.reshape(1, HEADS))
    bft = jnp.broadcast_to(jnp.concatenate([b_f, b_f])[:, None], (2 * HEADS, TM))
    gq = (jnp.tile(q_norm_g, HEADS) * HEAD_DIM ** -0.5).reshape(ATTN_W, 1)
    gk = jnp.tile(k_norm_g, HEADS).reshape(1, ATTN_W)

    head_of = np.arange(ATTN_W) // HEAD_DIM
    pm = jnp.asarray(np.where(head_of[:, None] == head_of[None, :], 1.0 / HEAD_DIM, 0.0), BF16)
    tok = np.arange(TM)
    lower = (tok[None, :] <= tok[:, None]).astype(np.float32)
    lt = jnp.asarray(lower, BF16)
    ut = jnp.asarray(np.concatenate([lower.T, np.ones((TM, LANES), np.float32)], axis=1), BF16)

    k, kx, qt, cq, vt, za, sga, mb = _proj(
        x, ada3, norm_g.reshape(1, d), wn, wt, bfn, bft, gq, gk, conv_w, wb, pm, lt, ut)
    logit_bound = HEAD_DIM ** 0.5 * jnp.max(jnp.abs(q_norm_g)) * jnp.max(jnp.abs(k_norm_g))
    cend = cq[:, :, TK - 1::TK].reshape(-1)
    args = (cend, logit_bound.reshape(1), qt, cq, k, kx, vt, za, sga, mb, x, ada3, wa, wo)
    return lax.cond(logit_bound <= MAX_RAW_LOGIT,
                    functools.partial(_attn, False), functools.partial(_attn, True), *args)


@jax.jit
def kernel(x, c, w_ada, b_ada, norm_g, w_in, b_f, q_norm_g, k_norm_g, conv_w,
           w_attn_out, w_conv_out, w_o):
    for i in range(w_ada.shape[0]):
        x = _layer(x, c, w_ada[i], b_ada[i], norm_g[i], w_in[i], b_f[i], q_norm_g[i],
                   k_norm_g[i], conv_w[i], w_attn_out[i], w_conv_out[i], w_o[i])
    return x
```

```python
import functools

import jax
import jax.numpy as jnp
import numpy as np
from jax import lax
from jax.experimental import pallas as pl
from jax.experimental.pallas import tpu as pltpu

D_MODEL = 1024
HEADS = 8
HEAD_DIM = 64
ATTN_W = HEADS * HEAD_DIM
CONV_W = 512
EPS = 1e-6

LANES = 128
TM = 512
TQ = 512
TK = 256
CUM_PARTS = 3
ONES_LANE = CUM_PARTS * HEADS
V_ROWS = HEAD_DIM + 16
SCORE_LOOKAHEAD = 4
UNITS_PER_LONG_ITER = 16
UNITS_PER_ITER = 8
NEG_BIG = -1e30
MAX_RAW_LOGIT = 40.0
SKIP_LOG_WEIGHT = 30.0
LOGIT_BOUND_MARGIN = 1.05
VMEM_LIMIT = 56 * 1024 * 1024

F32 = jnp.float32
BF16 = jnp.bfloat16


def _log_sigmoid(x):
    return jnp.minimum(x, 0.0) - jnp.log(1.0 + jnp.exp(-jnp.abs(x)))


def _sigmoid(x):
    return 0.5 * jnp.tanh(0.5 * x) + 0.5


def _silu(x):
    hx = 0.5 * x
    return hx * jnp.tanh(hx) + hx


def _dot(a, b):
    return jnp.dot(a, b, preferred_element_type=F32)


def _dot_nt(a, b):
    return lax.dot_general(a, b, (((1,), (1,)), ((), ())), preferred_element_type=F32)


def _dot_tn(a, b):
    return lax.dot_general(a, b, (((0,), (0,)), ((), ())), preferred_element_type=F32)


def _ada_kernel(c_ref, w_ref, b_ref, o_ref):
    o_ref[:, 0, :] = _dot(c_ref[...].astype(BF16), w_ref[...].astype(BF16)) + b_ref[...]


def _ada(c, w_ada, b_ada):
    b, d = c.shape
    n = w_ada.shape[1]
    return pl.pallas_call(
        _ada_kernel,
        grid=(n // d,),
        in_specs=[pl.BlockSpec((b, d), lambda j: (0, 0)),
                  pl.BlockSpec((d, d), lambda j: (0, j)),
                  pl.BlockSpec((1, d), lambda j: (0, j))],
        out_specs=pl.BlockSpec((b, 1, d), lambda j: (0, 0, j)),
        out_shape=jax.ShapeDtypeStruct((b, 1, n), F32),
        name="ada",
    )(c, w_ada, b_ada.reshape(1, n))


N_K = 0
N_GB = N_K + ATTN_W
N_GC = N_GB + CONV_W
N_U = N_GC + CONV_W
N_ZB = N_U + CONV_W
N_GA = N_ZB + CONV_W
N_GB2 = N_GA + D_MODEL
N_END = N_GB2 + D_MODEL
T_Q = 0
T_V = T_Q + ATTN_W
T_ZA = T_V + ATTN_W
T_F = T_ZA + ATTN_W
T_END = T_F + 2 * HEADS

IN_SIZES = (ATTN_W, ATTN_W, ATTN_W, HEADS, ATTN_W, CONV_W, CONV_W, CONV_W, CONV_W, D_MODEL, D_MODEL)
IN_OFFS = tuple(sum(IN_SIZES[:i]) for i in range(len(IN_SIZES) + 1))
PREP_COLS = 256


def _prep_kernel(w_ref, wa_ref, wb_ref, wo_ref, wn_ref, wt_ref, wa16_ref, wb16_ref, wo16_ref):
    rows = lambda i, j: w_ref[IN_OFFS[i]:IN_OFFS[j], :]
    f = rows(3, 4)
    wn_ref[...] = jnp.concatenate([rows(1, 2), rows(5, 11)], axis=0).astype(BF16).T
    wt_ref[T_Q:T_V, :] = rows(0, 1).astype(BF16)
    wt_ref[T_V:T_ZA, :] = rows(2, 3).astype(BF16)
    wt_ref[T_ZA:T_F, :] = rows(4, 5).astype(BF16)
    wt_ref[T_F:T_END, :] = jnp.concatenate([f, f], axis=0).astype(BF16)
    wa16_ref[...] = wa_ref[...].astype(BF16)
    wb16_ref[...] = wb_ref[...].astype(BF16)
    wo16_ref[...] = wo_ref[...].astype(BF16)


def _prep(w_t, w_attn_out, w_conv_out, w_o):
    n, d = w_t.shape
    steps = d // PREP_COLS
    row_block = lambda w: pl.BlockSpec((w.shape[0] // steps, w.shape[1]), lambda i: (i, 0))
    sides = (w_attn_out, w_conv_out, w_o)
    return pl.pallas_call(
        _prep_kernel,
        grid=(steps,),
        in_specs=[pl.BlockSpec((n, PREP_COLS), lambda i: (0, i))] + [row_block(w) for w in sides],
        out_specs=[pl.BlockSpec((PREP_COLS, N_END), lambda i: (i, 0)),
                   pl.BlockSpec((T_END, PREP_COLS), lambda i: (0, i))] + [row_block(w) for w in sides],
        out_shape=[jax.ShapeDtypeStruct((d, N_END), BF16),
                   jax.ShapeDtypeStruct((T_END, d), BF16)]
                  + [jax.ShapeDtypeStruct(w.shape, BF16) for w in sides],
        compiler_params=pltpu.CompilerParams(vmem_limit_bytes=VMEM_LIMIT),
        name="prep",
    )(w_t, *sides)


def _proj_kernel(xfirst_ref, xnext_ref, ada_ref, adanext_ref, ng_ref, wn_ref, wt_ref, bft_ref, gq_ref,
                 gk_ref, cw_ref, wb_ref, pm_ref, ut_ref,
                 k_ref, kx_ref, qt_ref, cq_ref, vt_ref, za_ref, sga_ref, mb_ref,
                 ct_ref, cu_ref, h_ref, gc_ref):
    s = pl.program_id(1)
    slot = lax.rem(s, 2)

    def modulated_norm(xr, ar):
        x = xr[0]
        shift = ar[0, :, 0:D_MODEL]
        scale = ar[0, :, D_MODEL:2 * D_MODEL]
        xn = x * lax.rsqrt(jnp.mean(x * x, axis=-1, keepdims=True) + EPS)
        return (xn * ng_ref[...] * (1.0 + scale) + shift).astype(BF16)

    @pl.when(s == 0)
    def _():
        ct_ref[...] = jnp.zeros_like(ct_ref)
        cu_ref[0:8, :] = jnp.zeros((8, CONV_W), F32)

    @pl.when((pl.program_id(0) == 0) & (s == 0))
    def _():
        h0 = modulated_norm(xfirst_ref, ada_ref)
        h_ref[0] = h0
        gc_ref[...] = _dot(h0, wn_ref[:, N_GC:N_U])

    nat = lambda lo, hi: _dot(h_ref[slot], wn_ref[:, lo:hi])
    trn = lambda lo, hi: _dot_nt(wt_ref[lo:hi, :], h_ref[slot])

    gc = gc_ref[...]
    u = nat(N_U, N_ZB)
    kraw = nat(N_K, N_GB)

    cu = gc * u
    cu_ref[8:8 + TM, :] = cu
    conv = (cw_ref[2:3, :] * cu + cw_ref[1:2, :] * cu_ref[7:7 + TM, :]
            + cw_ref[0:1, :] * cu_ref[6:6 + TM, :])
    cu_ref[0:8, :] = cu[TM - 8:TM, :]
    gb = nat(N_GB, N_GC)
    zb = nat(N_ZB, N_GA)

    kk = (kraw * kraw).astype(BF16)
    kms = _dot(kk, pm_ref[...])
    ga = nat(N_GA, N_GB2)

    gb2 = nat(N_GB2, N_END)
    ob = (gb * conv * _silu(zb)).astype(BF16)
    mbr = _dot(ob, wb_ref[...])

    sga_ref[0] = _sigmoid(ga).astype(BF16)
    kn = (kraw * lax.rsqrt(kms + EPS) * gk_ref[...]).astype(BF16)
    for pair in range(HEADS // 2):
        k_ref[0, pair] = kn[:, pair * LANES:(pair + 1) * LANES]
    zf = trn(T_ZA, T_END)
    qraw = trn(T_Q, T_V)

    mb_ref[0] = (_sigmoid(gb2) * mbr).astype(BF16)
    lft = _log_sigmoid(zf[ATTN_W:, :] + bft_ref[...])
    lft_lo = lft - lft.astype(BF16).astype(F32)
    row = lax.broadcasted_iota(jnp.int32, (2 * HEADS, TM), 0)
    ct = _dot(jnp.where(row < HEADS, lft, lft_lo).astype(BF16), ut_ref[...])

    za_ref[0] = _silu(zf[0:ATTN_W, :]).astype(BF16)
    ct = ct[0:HEADS, :] + ct[HEADS:2 * HEADS, :]
    carry = ct_ref[...]
    cq = ct[:, 0:TM] + jnp.concatenate([carry] * (TM // LANES), axis=1)
    cq_ref[0] = cq
    ct_ref[...] = carry + ct[:, TM:TM + LANES]
    vraw = trn(T_V, T_ZA)

    ck = jnp.concatenate([cq, jnp.zeros((LANES - HEADS, TM), F32)], axis=0).T
    ck = ck + pltpu.roll(ck, HEADS, 1) + pltpu.roll(ck, 2 * HEADS, 1)
    lane = lax.broadcasted_iota(jnp.int32, (TM, LANES), 1)
    r1 = ck - ck.astype(BF16).astype(F32)
    r2 = r1 - r1.astype(BF16).astype(F32)
    piece = jnp.where(lane < HEADS, ck, jnp.where(lane < 2 * HEADS, r1, r2))
    is_one = (lane >= ONES_LANE) & (lane < ONES_LANE + CUM_PARTS)
    kx_ref[0] = jnp.where(lane < ONES_LANE, -piece, jnp.where(is_one, 1.0, 0.0)).astype(BF16)

    q3 = qraw.reshape(HEADS, HEAD_DIM, TM)
    qn = q3 * lax.rsqrt(jnp.mean(q3 * q3, axis=1, keepdims=True) + EPS)
    qt_ref[0] = (qn.reshape(ATTN_W, TM) * gq_ref[...]).astype(BF16)

    vt = vraw.astype(BF16)
    for i in range(TM // TK):
        for hd in range(HEADS):
            vt_ref[0, i, hd, 0:HEAD_DIM, :] = (
                vt[hd * HEAD_DIM:(hd + 1) * HEAD_DIM, i * TK:(i + 1) * TK])
            vt_ref[0, i, hd, HEAD_DIM:V_ROWS, :] = jnp.ones((V_ROWS - HEAD_DIM, TK), BF16)

    h_next = modulated_norm(xnext_ref, adanext_ref)
    h_ref[1 - slot] = h_next
    gc_ref[...] = _dot(h_next, wn_ref[:, N_GC:N_U])


def _proj(x, ada3, ng, wn, wt, bft, gq, gk, cw, wb, pm, ut):
    b, s, d = x.shape
    const = lambda shape: pl.BlockSpec(shape, lambda bi, si: (0,) * len(shape))
    out_shape = [
        jax.ShapeDtypeStruct((b, HEADS // 2, s, LANES), BF16),
        jax.ShapeDtypeStruct((b, s, LANES), BF16),
        jax.ShapeDtypeStruct((b, ATTN_W, s), BF16),
        jax.ShapeDtypeStruct((b, HEADS, s), F32),
        jax.ShapeDtypeStruct((b, s // TK, HEADS, V_ROWS, TK), BF16),
        jax.ShapeDtypeStruct((b, ATTN_W, s), BF16),
        jax.ShapeDtypeStruct((b, s, d), BF16),
        jax.ShapeDtypeStruct((b, s, d), BF16),
    ]
    out_specs = [
        pl.BlockSpec((1, HEADS // 2, TM, LANES), lambda bi, si: (bi, 0, si, 0)),
        pl.BlockSpec((1, TM, LANES), lambda bi, si: (bi, si, 0)),
        pl.BlockSpec((1, ATTN_W, TM), lambda bi, si: (bi, 0, si)),
        pl.BlockSpec((1, HEADS, TM), lambda bi, si: (bi, 0, si)),
        pl.BlockSpec((1, TM // TK, HEADS, V_ROWS, TK), lambda bi, si: (bi, si, 0, 0, 0)),
        pl.BlockSpec((1, ATTN_W, TM), lambda bi, si: (bi, 0, si)),
        pl.BlockSpec((1, TM, d), lambda bi, si: (bi, si, 0)),
        pl.BlockSpec((1, TM, d), lambda bi, si: (bi, si, 0)),
    ]
    tiles = s // TM
    assert tiles % 2 == 0
    next_b = lambda bi, si: jnp.minimum(bi + (si + 1) // tiles, b - 1)
    next_s = lambda bi, si: jnp.where((bi == b - 1) & (si == tiles - 1), si, (si + 1) % tiles)
    in_specs = [
        pl.BlockSpec((1, TM, d), lambda bi, si: (0, 0, 0)),
        pl.BlockSpec((1, TM, d), lambda bi, si: (next_b(bi, si), next_s(bi, si), 0)),
        pl.BlockSpec((1, 1, 3 * d), lambda bi, si: (bi, 0, 0)),
        pl.BlockSpec((1, 1, 3 * d), lambda bi, si: (next_b(bi, si), 0, 0)),
        const(ng.shape), const(wn.shape), const(wt.shape), const(bft.shape), const(gq.shape),
        const(gk.shape), const(cw.shape), const(wb.shape), const(pm.shape), const(ut.shape),
    ]
    return pl.pallas_call(
        _proj_kernel,
        grid=(b, s // TM),
        in_specs=in_specs,
        out_specs=out_specs,
        out_shape=out_shape,
        scratch_shapes=[pltpu.VMEM((HEADS, LANES), F32),
                        pltpu.VMEM((TM + 8, CONV_W), F32),
                        pltpu.VMEM((2, TM, D_MODEL), BF16),
                        pltpu.VMEM((TM, CONV_W), F32)],
        compiler_params=pltpu.CompilerParams(
            dimension_semantics=("arbitrary", "arbitrary"),
            vmem_limit_bytes=VMEM_LIMIT),
        name="proj",
    )(x, x, ada3, ada3, ng, wn, wt, bft, gq, gk, cw, wb, pm, ut)


def _attn_kernel(cend_ref, bound_ref, qt_ref, cq_ref, k_ref, kx_ref, vt_ref, za_ref, sga_ref, mb_ref,
                 x_ref, gate_ref, wa_ref, wo_ref, o_ref, rhs_ref, m_ref, acc_ref, *, running_max):
    bi = pl.program_id(0)
    qi = pl.program_id(1)
    blocks_per_tile = TQ // TK
    n_blocks = kx_ref.shape[1] // TK
    i32 = jnp.int32

    last_before = qi * blocks_per_tile - 1
    slack = LOGIT_BOUND_MARGIN * 2.0 * bound_ref[0] + SKIP_LOG_WEIGHT

    first = []
    for h in range(HEADS):
        base = (bi * HEADS + h) * n_blocks
        threshold = cend_ref[base + jnp.maximum(last_before, 0)] + slack
        skipped = i32(0)
        for j in range(n_blocks - blocks_per_tile - 1):
            skipped = skipped + (cend_ref[base + j] >= threshold).astype(i32)
        first.append(skipped)
    upto, shift, total = [], [], i32(0)
    for h in range(HEADS):
        shift.append(first[h] - total)
        total = total + (qi * blocks_per_tile - first[h])
        upto.append(total)

    row = lax.broadcasted_iota(jnp.int32, (LANES, TQ), 0)
    for h in range(HEADS):
        pair, half = divmod(h, 2)
        qp = qt_ref[0, pair * LANES:(pair + 1) * LANES, :]
        mine = (row >= half * HEAD_DIM) & (row < (half + 1) * HEAD_DIM)
        rhs_ref[h, 0:LANES, :] = qp * jnp.where(mine, 1.0, 0.0).astype(BF16)
        cq = cq_ref[0, h:h + 1, :]
        r1 = cq - cq.astype(BF16).astype(F32)
        r2 = r1 - r1.astype(BF16).astype(F32)
        sel = (row == h) | (row == HEADS + h) | (row == 2 * HEADS + h)
        f = jnp.where(row == ONES_LANE, cq,
                      jnp.where(row == ONES_LANE + 1, r1,
                                jnp.where(row == ONES_LANE + 2, r2, 0.0)))
        rhs_ref[h, LANES:2 * LANES, :] = jnp.where(sel, 1.0, f).astype(BF16)
    rhs_ref[HEADS] = jnp.zeros(rhs_ref.shape[1:], BF16)

    if running_max:
        m_ref[...] = jnp.full(m_ref.shape, NEG_BIG, F32)
    acc_ref[...] = jnp.zeros_like(acc_ref)

    def scores(j, h, q_lo):
        start = pl.multiple_of(j * TK, TK)
        pair = h // 2 if isinstance(h, int) else jnp.minimum(h, HEADS - 1) // 2
        lhs = jnp.concatenate([k_ref[0, pair, pl.ds(start, TK), :],
                               kx_ref[0, pl.ds(start, TK), :]], axis=1)
        return _dot(lhs, rhs_ref[h, :, q_lo:TQ])

    def run_units(units):
        ahead = [scores(j, h, q_lo) for j, h, _, q_lo in units[:SCORE_LOOKAHEAD]]
        for i, (j, h, masked, q_lo) in enumerate(units):
            sc = ahead.pop(0)
            if i + SCORE_LOOKAHEAD < len(units):
                jn, hn, _, qn = units[i + SCORE_LOOKAHEAD]
                ahead.append(scores(jn, hn, qn))
            if masked:
                kpos = lax.broadcasted_iota(jnp.int32, sc.shape, 0)
                qpos = lax.broadcasted_iota(jnp.int32, sc.shape, 1)
                sc = jnp.where(kpos <= qpos, sc, NEG_BIG)
            hv = h if isinstance(h, int) else jnp.minimum(h, HEADS - 1)
            if running_max:
                m_prev = m_ref[h, :, q_lo:TQ]
                m_new = jnp.maximum(m_prev, jnp.max(sc, axis=0, keepdims=True))
                m_ref[h, :, q_lo:TQ] = m_new
                p = jnp.exp(sc - m_new).astype(BF16)
                acc_ref[h, :, q_lo:TQ] = (jnp.exp(m_prev - m_new) * acc_ref[h, :, q_lo:TQ]
                                          + _dot(vt_ref[0, j, hv], p))
            else:
                p = jnp.exp(sc).astype(BF16)
                acc_ref[h, :, q_lo:TQ] += _dot(vt_ref[0, j, hv], p)

    def list_units(start, count):
        units = []
        for t in range(count):
            u = start + t
            h = i32(0)
            off = shift[0]
            for g in range(HEADS):
                past = u >= upto[g]
                h = h + past.astype(i32)
                if g + 1 < HEADS:
                    off = jnp.where(past, shift[g + 1], off)
            j = jnp.where(u >= total, 0, u + off)
            units.append((j, h, False, 0))
        return units

    def long_body(i, carry):
        run_units(list_units(i * UNITS_PER_LONG_ITER, UNITS_PER_LONG_ITER))
        return carry

    n_short = (total + UNITS_PER_ITER - 1) // UNITS_PER_ITER
    n_long = n_short // (UNITS_PER_LONG_ITER // UNITS_PER_ITER)
    lax.fori_loop(0, n_long, long_body, 0)
    done = n_long * UNITS_PER_LONG_ITER

    def short_body(i, carry):
        run_units(list_units(done + i * UNITS_PER_ITER, UNITS_PER_ITER))
        return carry

    lax.fori_loop(0, n_short - n_long * (UNITS_PER_LONG_ITER // UNITS_PER_ITER), short_body, 0)
    run_units([(qi * blocks_per_tile + t, h, True, t * TK)
               for t in range(blocks_per_tile) for h in range(HEADS)])

    acc = jnp.concatenate(
        [acc_ref[h, 0:HEAD_DIM, :] * (1.0 / acc_ref[h, HEAD_DIM:HEAD_DIM + 1, :])
         for h in range(HEADS)], axis=0)
    at = (acc * za_ref[0].astype(F32)).astype(BF16)
    ya = _dot_tn(at, wa_ref[...])
    merged = (sga_ref[0].astype(F32) * ya + mb_ref[0].astype(F32)).astype(BF16)
    o_ref[0] = x_ref[0] + gate_ref[0] * _dot(merged, wo_ref[...])


def _attn(running_max, cend, bound, qt, cq, k, kx, vt, za, sga, mb, x, ada3, wa, wo):
    b, s, d = x.shape
    const = lambda shape: pl.BlockSpec(shape, lambda bi, qi: (0,) * len(shape))
    in_specs = [
        pl.BlockSpec(memory_space=pltpu.SMEM),
        pl.BlockSpec(memory_space=pltpu.SMEM),
        pl.BlockSpec((1, ATTN_W, TQ), lambda bi, qi: (bi, 0, qi)),
        pl.BlockSpec((1, HEADS, TQ), lambda bi, qi: (bi, 0, qi)),
        pl.BlockSpec((1, HEADS // 2, s, LANES), lambda bi, qi: (bi, 0, 0, 0)),
        pl.BlockSpec((1, s, LANES), lambda bi, qi: (bi, 0, 0)),
        pl.BlockSpec((1, s // TK, HEADS, V_ROWS, TK), lambda bi, qi: (bi, 0, 0, 0, 0)),
        pl.BlockSpec((1, ATTN_W, TQ), lambda bi, qi: (bi, 0, qi)),
        pl.BlockSpec((1, TQ, d), lambda bi, qi: (bi, qi, 0)),
        pl.BlockSpec((1, TQ, d), lambda bi, qi: (bi, qi, 0)),
        pl.BlockSpec((1, TQ, d), lambda bi, qi: (bi, qi, 0)),
        pl.BlockSpec((1, 1, d), lambda bi, qi: (bi, 0, 2)),
        const(wa.shape), const(wo.shape),
    ]
    return pl.pallas_call(
        functools.partial(_attn_kernel, running_max=running_max),
        grid=(b, s // TQ),
        in_specs=in_specs,
        out_specs=pl.BlockSpec((1, TQ, d), lambda bi, qi: (bi, qi, 0)),
        out_shape=jax.ShapeDtypeStruct((b, s, d), F32),
        scratch_shapes=[pltpu.VMEM((HEADS + 1, 2 * LANES, TQ), BF16),
                        pltpu.VMEM((HEADS + 1, 1, TQ), F32),
                        pltpu.VMEM((HEADS + 1, V_ROWS, TQ), F32)],
        compiler_params=pltpu.CompilerParams(
            dimension_semantics=("arbitrary", "arbitrary"),
            vmem_limit_bytes=VMEM_LIMIT),
        name="attn_running_max" if running_max else "attn",
    )(cend, bound, qt, cq, k, kx, vt, za, sga, mb, x, ada3, wa, wo)


def _layer(x, c, w_ada, b_ada, norm_g, w_in, b_f, q_norm_g, k_norm_g, conv_w,
           w_attn_out, w_conv_out, w_o):
    b, s, d = x.shape
    ada3 = _ada(c, w_ada, b_ada)

    wn, wt, wa, wb, wo = _prep(w_in.T, w_attn_out, w_conv_out, w_o)

    bft = jnp.broadcast_to(jnp.concatenate([b_f, b_f])[:, None], (2 * HEADS, TM))
    gq = (jnp.tile(q_norm_g, HEADS) * HEAD_DIM ** -0.5).reshape(ATTN_W, 1)
    gk = jnp.tile(k_norm_g, HEADS).reshape(1, ATTN_W)

    head_of = np.arange(ATTN_W) // HEAD_DIM
    pm = jnp.asarray(np.where(head_of[:, None] == head_of[None, :], 1.0 / HEAD_DIM, 0.0), BF16)
    tok = np.arange(TM)
    upper = (tok[:, None] <= tok[None, :]).astype(np.float32)
    ut = jnp.asarray(np.concatenate([upper, np.ones((TM, LANES), np.float32)], axis=1), BF16)

    k, kx, qt, cq, vt, za, sga, mb = _proj(
        x, ada3, norm_g.reshape(1, d), wn, wt, bft, gq, gk, conv_w, wb, pm, ut)
    logit_bound = HEAD_DIM ** 0.5 * jnp.max(jnp.abs(q_norm_g)) * jnp.max(jnp.abs(k_norm_g))
    cend = cq[:, :, TK - 1::TK].reshape(-1)
    args = (cend, logit_bound.reshape(1), qt, cq, k, kx, vt, za, sga, mb, x, ada3, wa, wo)
    return lax.cond(logit_bound <= MAX_RAW_LOGIT,
                    functools.partial(_attn, False), functools.partial(_attn, True), *args)


@jax.jit
def kernel(x, c, w_ada, b_ada, norm_g, w_in, b_f, q_norm_g, k_norm_g, conv_w,
           w_attn_out, w_conv_out, w_o):
    for i in range(w_ada.shape[0]):
        x = _layer(x, c, w_ada[i], b_ada[i], norm_g[i], w_in[i], b_f[i], q_norm_g[i],
                   k_norm_g[i], conv_w[i], w_attn_out[i], w_conv_out[i], w_o[i])
    return x
```

```python
import functools

import jax
import jax.numpy as jnp
import numpy as np
from jax import lax
from jax.experimental import pallas as pl
from jax.experimental.pallas import tpu as pltpu

D_MODEL = 1024
HEADS = 8
HEAD_DIM = 64
ATTN_W = HEADS * HEAD_DIM
CONV_W = 512
EPS = 1e-6

LANES = 128
TM = 512
TQ = 512
TK = 256
CUM_PARTS = 3
ONES_LANE = CUM_PARTS * HEADS
V_ROWS = HEAD_DIM + 16
SCORE_LOOKAHEAD = 4
UNITS_PER_LONG_ITER = 16
UNITS_PER_ITER = 8
NEG_BIG = -1e30
MAX_RAW_LOGIT = 40.0
SKIP_LOG_WEIGHT = 30.0
LOGIT_BOUND_MARGIN = 1.05
VMEM_LIMIT = 56 * 1024 * 1024

F32 = jnp.float32
BF16 = jnp.bfloat16


def _log_sigmoid(x):
    return jnp.minimum(x, 0.0) - jnp.log(1.0 + jnp.exp(-jnp.abs(x)))


def _sigmoid(x):
    return 0.5 * jnp.tanh(0.5 * x) + 0.5


def _silu(x):
    hx = 0.5 * x
    return hx * jnp.tanh(hx) + hx


def _dot(a, b):
    return jnp.dot(a, b, preferred_element_type=F32)


def _dot_nt(a, b):
    return lax.dot_general(a, b, (((1,), (1,)), ((), ())), preferred_element_type=F32)


def _dot_tn(a, b):
    return lax.dot_general(a, b, (((0,), (0,)), ((), ())), preferred_element_type=F32)


def _ada_kernel(c_ref, w_ref, b_ref, o_ref):
    o_ref[:, 0, :] = _dot(c_ref[...].astype(BF16), w_ref[...].astype(BF16)) + b_ref[...]


def _ada(c, w_ada, b_ada):
    b, d = c.shape
    n = w_ada.shape[1]
    return pl.pallas_call(
        _ada_kernel,
        grid=(n // d,),
        in_specs=[pl.BlockSpec((b, d), lambda j: (0, 0)),
                  pl.BlockSpec((d, d), lambda j: (0, j)),
                  pl.BlockSpec((1, d), lambda j: (0, j))],
        out_specs=pl.BlockSpec((b, 1, d), lambda j: (0, 0, j)),
        out_shape=jax.ShapeDtypeStruct((b, 1, n), F32),
        name="ada",
    )(c, w_ada, b_ada.reshape(1, n))


N_GB = 0
N_GC = N_GB + CONV_W
N_U = N_GC + CONV_W
N_ZB = N_U + CONV_W
N_GA = N_ZB + CONV_W
N_GB2 = N_GA + D_MODEL
N_END = N_GB2 + D_MODEL
T_Q = 0
T_K = T_Q + ATTN_W
T_V = T_K + ATTN_W
T_ZA = T_V + ATTN_W
T_F = T_ZA + ATTN_W
T_END = T_F + 2 * HEADS

IN_SIZES = (ATTN_W, ATTN_W, ATTN_W, HEADS, ATTN_W, CONV_W, CONV_W, CONV_W, CONV_W, D_MODEL, D_MODEL)
IN_OFFS = tuple(sum(IN_SIZES[:i]) for i in range(len(IN_SIZES) + 1))
PREP_COLS = 256


def _prep_kernel(w_ref, wa_ref, wb_ref, wo_ref, wn_ref, wt_ref, wa16_ref, wb16_ref, wo16_ref):
    rows = lambda i, j: w_ref[IN_OFFS[i]:IN_OFFS[j], :]
    f = rows(3, 4)
    wn_ref[...] = rows(5, 11).astype(BF16).T
    wt_ref[T_Q:T_ZA, :] = rows(0, 3).astype(BF16)
    wt_ref[T_ZA:T_F, :] = rows(4, 5).astype(BF16)
    wt_ref[T_F:T_END, :] = jnp.concatenate([f, f], axis=0).astype(BF16)
    wa16_ref[...] = wa_ref[...].astype(BF16)
    wb16_ref[...] = wb_ref[...].astype(BF16)
    wo16_ref[...] = wo_ref[...].astype(BF16)


def _prep(w_t, w_attn_out, w_conv_out, w_o):
    n, d = w_t.shape
    steps = d // PREP_COLS
    row_block = lambda w: pl.BlockSpec((w.shape[0] // steps, w.shape[1]), lambda i: (i, 0))
    sides = (w_attn_out, w_conv_out, w_o)
    return pl.pallas_call(
        _prep_kernel,
        grid=(steps,),
        in_specs=[pl.BlockSpec((n, PREP_COLS), lambda i: (0, i))] + [row_block(w) for w in sides],
        out_specs=[pl.BlockSpec((PREP_COLS, N_END), lambda i: (i, 0)),
                   pl.BlockSpec((T_END, PREP_COLS), lambda i: (0, i))] + [row_block(w) for w in sides],
        out_shape=[jax.ShapeDtypeStruct((d, N_END), BF16),
                   jax.ShapeDtypeStruct((T_END, d), BF16)]
                  + [jax.ShapeDtypeStruct(w.shape, BF16) for w in sides],
        compiler_params=pltpu.CompilerParams(vmem_limit_bytes=VMEM_LIMIT),
        name="prep",
    )(w_t, *sides)


def _proj_kernel(xfirst_ref, xnext_ref, ada_ref, adanext_ref, ng_ref, wn_ref, wt_ref, bft_ref, gq_ref,
                 gk_ref, cw_ref, wb_ref, ut_ref,
                 k_ref, kx_ref, qt_ref, cq_ref, vt_ref, za_ref, sga_ref, mb_ref,
                 ct_ref, cu_ref, h_ref, gc_ref):
    s = pl.program_id(1)
    slot = lax.rem(s, 2)

    def modulated_norm(xr, ar):
        x = xr[0]
        shift = ar[0, :, 0:D_MODEL]
        scale = ar[0, :, D_MODEL:2 * D_MODEL]
        xn = x * lax.rsqrt(jnp.mean(x * x, axis=-1, keepdims=True) + EPS)
        return (xn * ng_ref[...] * (1.0 + scale) + shift).astype(BF16)

    @pl.when(s == 0)
    def _():
        ct_ref[...] = jnp.zeros_like(ct_ref)
        cu_ref[0:8, :] = jnp.zeros((8, CONV_W), F32)

    @pl.when((pl.program_id(0) == 0) & (s == 0))
    def _():
        h0 = modulated_norm(xfirst_ref, ada_ref)
        h_ref[0] = h0
        gc_ref[...] = _dot(h0, wn_ref[:, N_GC:N_U])

    nat = lambda lo, hi: _dot(h_ref[slot], wn_ref[:, lo:hi])
    trn = lambda lo, hi: _dot_nt(wt_ref[lo:hi, :], h_ref[slot])

    gc = gc_ref[...]
    u = nat(N_U, N_ZB)

    cu = gc * u
    cu_ref[8:8 + TM, :] = cu
    conv = (cw_ref[2:3, :] * cu + cw_ref[1:2, :] * cu_ref[7:7 + TM, :]
            + cw_ref[0:1, :] * cu_ref[6:6 + TM, :])
    cu_ref[0:8, :] = cu[TM - 8:TM, :]
    gb = nat(N_GB, N_GC)
    zb = nat(N_ZB, N_GA)

    ga = nat(N_GA, N_GB2)

    gb2 = nat(N_GB2, N_END)
    ob = (gb * conv * _silu(zb)).astype(BF16)
    mbr = _dot(ob, wb_ref[...])

    sga_ref[0] = _sigmoid(ga).astype(BF16)
    zf = trn(T_ZA, T_END)
    qraw = trn(T_Q, T_K)

    mb_ref[0] = (_sigmoid(gb2) * mbr).astype(BF16)
    lft = _log_sigmoid(zf[ATTN_W:, :] + bft_ref[...])
    lft_lo = lft - lft.astype(BF16).astype(F32)
    row = lax.broadcasted_iota(jnp.int32, (2 * HEADS, TM), 0)
    ct = _dot(jnp.where(row < HEADS, lft, lft_lo).astype(BF16), ut_ref[...])

    za_ref[0] = _silu(zf[0:ATTN_W, :]).astype(BF16)
    ct = ct[0:HEADS, :] + ct[HEADS:2 * HEADS, :]
    carry = ct_ref[...]
    cq = ct[:, 0:TM] + jnp.concatenate([carry] * (TM // LANES), axis=1)
    cq_ref[0] = cq
    ct_ref[...] = carry + ct[:, TM:TM + LANES]
    kraw = trn(T_K, T_V)

    ck = jnp.concatenate([cq, jnp.zeros((LANES - HEADS, TM), F32)], axis=0).T
    ck = ck + pltpu.roll(ck, HEADS, 1) + pltpu.roll(ck, 2 * HEADS, 1)
    lane = lax.broadcasted_iota(jnp.int32, (TM, LANES), 1)
    r1 = ck - ck.astype(BF16).astype(F32)
    r2 = r1 - r1.astype(BF16).astype(F32)
    piece = jnp.where(lane < HEADS, ck, jnp.where(lane < 2 * HEADS, r1, r2))
    is_one = (lane >= ONES_LANE) & (lane < ONES_LANE + CUM_PARTS)
    kx_ref[0] = jnp.where(lane < ONES_LANE, -piece, jnp.where(is_one, 1.0, 0.0)).astype(BF16)

    def head_norm(raw, gain_ref):
        r3 = raw.reshape(HEADS, HEAD_DIM, TM)
        rn = r3 * lax.rsqrt(jnp.mean(r3 * r3, axis=1, keepdims=True) + EPS)
        return (rn.reshape(ATTN_W, TM) * gain_ref[...]).astype(BF16)

    qt_ref[0] = head_norm(qraw, gq_ref)
    vraw = trn(T_V, T_ZA)

    kt = head_norm(kraw, gk_ref)
    for pair in range(HEADS // 2):
        k_ref[0, pair] = kt[pair * LANES:(pair + 1) * LANES, :].T

    vt = vraw.astype(BF16)
    for i in range(TM // TK):
        for hd in range(HEADS):
            vt_ref[0, i, hd, 0:HEAD_DIM, :] = (
                vt[hd * HEAD_DIM:(hd + 1) * HEAD_DIM, i * TK:(i + 1) * TK])
            vt_ref[0, i, hd, HEAD_DIM:V_ROWS, :] = jnp.ones((V_ROWS - HEAD_DIM, TK), BF16)

    h_next = modulated_norm(xnext_ref, adanext_ref)
    h_ref[1 - slot] = h_next
    gc_ref[...] = _dot(h_next, wn_ref[:, N_GC:N_U])


def _proj(x, ada3, ng, wn, wt, bft, gq, gk, cw, wb, ut):
    b, s, d = x.shape
    const = lambda shape: pl.BlockSpec(shape, lambda bi, si: (0,) * len(shape))
    out_shape = [
        jax.ShapeDtypeStruct((b, HEADS // 2, s, LANES), BF16),
        jax.ShapeDtypeStruct((b, s, LANES), BF16),
        jax.ShapeDtypeStruct((b, ATTN_W, s), BF16),
        jax.ShapeDtypeStruct((b, HEADS, s), F32),
        jax.ShapeDtypeStruct((b, s // TK, HEADS, V_ROWS, TK), BF16),
        jax.ShapeDtypeStruct((b, ATTN_W, s), BF16),
        jax.ShapeDtypeStruct((b, s, d), BF16),
        jax.ShapeDtypeStruct((b, s, d), BF16),
    ]
    out_specs = [
        pl.BlockSpec((1, HEADS // 2, TM, LANES), lambda bi, si: (bi, 0, si, 0)),
        pl.BlockSpec((1, TM, LANES), lambda bi, si: (bi, si, 0)),
        pl.BlockSpec((1, ATTN_W, TM), lambda bi, si: (bi, 0, si)),
        pl.BlockSpec((1, HEADS, TM), lambda bi, si: (bi, 0, si)),
        pl.BlockSpec((1, TM // TK, HEADS, V_ROWS, TK), lambda bi, si: (bi, si, 0, 0, 0)),
        pl.BlockSpec((1, ATTN_W, TM), lambda bi, si: (bi, 0, si)),
        pl.BlockSpec((1, TM, d), lambda bi, si: (bi, si, 0)),
        pl.BlockSpec((1, TM, d), lambda bi, si: (bi, si, 0)),
    ]
    tiles = s // TM
    assert tiles % 2 == 0
    next_b = lambda bi, si: jnp.minimum(bi + (si + 1) // tiles, b - 1)
    next_s = lambda bi, si: jnp.where((bi == b - 1) & (si == tiles - 1), si, (si + 1) % tiles)
    in_specs = [
        pl.BlockSpec((1, TM, d), lambda bi, si: (0, 0, 0)),
        pl.BlockSpec((1, TM, d), lambda bi, si: (next_b(bi, si), next_s(bi, si), 0)),
        pl.BlockSpec((1, 1, 3 * d), lambda bi, si: (bi, 0, 0)),
        pl.BlockSpec((1, 1, 3 * d), lambda bi, si: (next_b(bi, si), 0, 0)),
        const(ng.shape), const(wn.shape), const(wt.shape), const(bft.shape), const(gq.shape),
        const(gk.shape), const(cw.shape), const(wb.shape), const(ut.shape),
    ]
    return pl.pallas_call(
        _proj_kernel,
        grid=(b, s // TM),
        in_specs=in_specs,
        out_specs=out_specs,
        out_shape=out_shape,
        scratch_shapes=[pltpu.VMEM((HEADS, LANES), F32),
                        pltpu.VMEM((TM + 8, CONV_W), F32),
                        pltpu.VMEM((2, TM, D_MODEL), BF16),
                        pltpu.VMEM((TM, CONV_W), F32)],
        compiler_params=pltpu.CompilerParams(
            dimension_semantics=("arbitrary", "arbitrary"),
            vmem_limit_bytes=VMEM_LIMIT),
        name="proj",
    )(x, x, ada3, ada3, ng, wn, wt, bft, gq, gk, cw, wb, ut)


def _attn_kernel(cend_ref, bound_ref, qt_ref, cq_ref, k_ref, kx_ref, vt_ref, za_ref, sga_ref, mb_ref,
                 x_ref, gate_ref, wa_ref, wo_ref, o_ref, rhs_ref, m_ref, acc_ref, *, running_max):
    bi = pl.program_id(0)
    qi = pl.program_id(1)
    blocks_per_tile = TQ // TK
    n_blocks = kx_ref.shape[1] // TK
    i32 = jnp.int32

    last_before = qi * blocks_per_tile - 1
    slack = LOGIT_BOUND_MARGIN * 2.0 * bound_ref[0] + SKIP_LOG_WEIGHT

    first = []
    for h in range(HEADS):
        base = (bi * HEADS + h) * n_blocks
        threshold = cend_ref[base + jnp.maximum(last_before, 0)] + slack
        skipped = i32(0)
        for j in range(n_blocks - blocks_per_tile - 1):
            skipped = skipped + (cend_ref[base + j] >= threshold).astype(i32)
        first.append(skipped)
    upto, shift, total = [], [], i32(0)
    for h in range(HEADS):
        shift.append(first[h] - total)
        total = total + (qi * blocks_per_tile - first[h])
        upto.append(total)

    row = lax.broadcasted_iota(jnp.int32, (LANES, TQ), 0)
    for h in range(HEADS):
        pair, half = divmod(h, 2)
        qp = qt_ref[0, pair * LANES:(pair + 1) * LANES, :]
        mine = (row >= half * HEAD_DIM) & (row < (half + 1) * HEAD_DIM)
        rhs_ref[h, 0:LANES, :] = qp * jnp.where(mine, 1.0, 0.0).astype(BF16)
        cq = cq_ref[0, h:h + 1, :]
        r1 = cq - cq.astype(BF16).astype(F32)
        r2 = r1 - r1.astype(BF16).astype(F32)
        sel = (row == h) | (row == HEADS + h) | (row == 2 * HEADS + h)
        f = jnp.where(row == ONES_LANE, cq,
                      jnp.where(row == ONES_LANE + 1, r1,
                                jnp.where(row == ONES_LANE + 2, r2, 0.0)))
        rhs_ref[h, LANES:2 * LANES, :] = jnp.where(sel, 1.0, f).astype(BF16)
    rhs_ref[HEADS] = jnp.zeros(rhs_ref.shape[1:], BF16)

    if running_max:
        m_ref[...] = jnp.full(m_ref.shape, NEG_BIG, F32)
    acc_ref[...] = jnp.zeros_like(acc_ref)

    def scores(j, h, q_lo):
        start = pl.multiple_of(j * TK, TK)
        pair = h // 2 if isinstance(h, int) else jnp.minimum(h, HEADS - 1) // 2
        lhs = jnp.concatenate([k_ref[0, pair, pl.ds(start, TK), :],
                               kx_ref[0, pl.ds(start, TK), :]], axis=1)
        return _dot(lhs, rhs_ref[h, :, q_lo:TQ])

    def run_units(units):
        ahead = [scores(j, h, q_lo) for j, h, _, q_lo in units[:SCORE_LOOKAHEAD]]
        for i, (j, h, masked, q_lo) in enumerate(units):
            sc = ahead.pop(0)
            if i + SCORE_LOOKAHEAD < len(units):
                jn, hn, _, qn = units[i + SCORE_LOOKAHEAD]
                ahead.append(scores(jn, hn, qn))
            if masked:
                kpos = lax.broadcasted_iota(jnp.int32, sc.shape, 0)
                qpos = lax.broadcasted_iota(jnp.int32, sc.shape, 1)
                sc = jnp.where(kpos <= qpos, sc, NEG_BIG)
            hv = h if isinstance(h, int) else jnp.minimum(h, HEADS - 1)
            if running_max:
                m_prev = m_ref[h, :, q_lo:TQ]
                m_new = jnp.maximum(m_prev, jnp.max(sc, axis=0, keepdims=True))
                m_ref[h, :, q_lo:TQ] = m_new
                p = jnp.exp(sc - m_new).astype(BF16)
                acc_ref[h, :, q_lo:TQ] = (jnp.exp(m_prev - m_new) * acc_ref[h, :, q_lo:TQ]
                                          + _dot(vt_ref[0, j, hv], p))
            else:
                p = jnp.exp(sc).astype(BF16)
                acc_ref[h, :, q_lo:TQ] += _dot(vt_ref[0, j, hv], p)

    def list_units(start, count):
        units = []
        for t in range(count):
            u = start + t
            h = i32(0)
            off = shift[0]
            for g in range(HEADS):
                past = u >= upto[g]
                h = h + past.astype(i32)
                if g + 1 < HEADS:
                    off = jnp.where(past, shift[g + 1], off)
            j = jnp.where(u >= total, 0, u + off)
            units.append((j, h, False, 0))
        return units

    def long_body(i, carry):
        run_units(list_units(i * UNITS_PER_LONG_ITER, UNITS_PER_LONG_ITER))
        return carry

    n_short = (total + UNITS_PER_ITER - 1) // UNITS_PER_ITER
    n_long = n_short // (UNITS_PER_LONG_ITER // UNITS_PER_ITER)
    lax.fori_loop(0, n_long, long_body, 0)
    done = n_long * UNITS_PER_LONG_ITER

    def short_body(i, carry):
        run_units(list_units(done + i * UNITS_PER_ITER, UNITS_PER_ITER))
        return carry

    lax.fori_loop(0, n_short - n_long * (UNITS_PER_LONG_ITER // UNITS_PER_ITER), short_body, 0)
    run_units([(qi * blocks_per_tile + t, h, True, t * TK)
               for t in range(blocks_per_tile) for h in range(HEADS)])

    acc = jnp.concatenate(
        [acc_ref[h, 0:HEAD_DIM, :] * (1.0 / acc_ref[h, HEAD_DIM:HEAD_DIM + 1, :])
         for h in range(HEADS)], axis=0)
    at = (acc * za_ref[0].astype(F32)).astype(BF16)
    ya = _dot_tn(at, wa_ref[...])
    merged = (sga_ref[0].astype(F32) * ya + mb_ref[0].astype(F32)).astype(BF16)
    o_ref[0] = x_ref[0] + gate_ref[0] * _dot(merged, wo_ref[...])


def _attn(running_max, cend, bound, qt, cq, k, kx, vt, za, sga, mb, x, ada3, wa, wo):
    b, s, d = x.shape
    const = lambda shape: pl.BlockSpec(shape, lambda bi, qi: (0,) * len(shape))
    in_specs = [
        pl.BlockSpec(memory_space=pltpu.SMEM),
        pl.BlockSpec(memory_space=pltpu.SMEM),
        pl.BlockSpec((1, ATTN_W, TQ), lambda bi, qi: (bi, 0, qi)),
        pl.BlockSpec((1, HEADS, TQ), lambda bi, qi: (bi, 0, qi)),
        pl.BlockSpec((1, HEADS // 2, s, LANES), lambda bi, qi: (bi, 0, 0, 0)),
        pl.BlockSpec((1, s, LANES), lambda bi, qi: (bi, 0, 0)),
        pl.BlockSpec((1, s // TK, HEADS, V_ROWS, TK), lambda bi, qi: (bi, 0, 0, 0, 0)),
        pl.BlockSpec((1, ATTN_W, TQ), lambda bi, qi: (bi, 0, qi)),
        pl.BlockSpec((1, TQ, d), lambda bi, qi: (bi, qi, 0)),
        pl.BlockSpec((1, TQ, d), lambda bi, qi: (bi, qi, 0)),
        pl.BlockSpec((1, TQ, d), lambda bi, qi: (bi, qi, 0)),
        pl.BlockSpec((1, 1, d), lambda bi, qi: (bi, 0, 2)),
        const(wa.shape), const(wo.shape),
    ]
    return pl.pallas_call(
        functools.partial(_attn_kernel, running_max=running_max),
        grid=(b, s // TQ),
        in_specs=in_specs,
        out_specs=pl.BlockSpec((1, TQ, d), lambda bi, qi: (bi, qi, 0)),
        out_shape=jax.ShapeDtypeStruct((b, s, d), F32),
        scratch_shapes=[pltpu.VMEM((HEADS + 1, 2 * LANES, TQ), BF16),
                        pltpu.VMEM((HEADS + 1, 1, TQ), F32),
                        pltpu.VMEM((HEADS + 1, V_ROWS, TQ), F32)],
        compiler_params=pltpu.CompilerParams(
            dimension_semantics=("arbitrary", "arbitrary"),
            vmem_limit_bytes=VMEM_LIMIT),
        name="attn_running_max" if running_max else "attn",
    )(cend, bound, qt, cq, k, kx, vt, za, sga, mb, x, ada3, wa, wo)


def _layer(x, c, w_ada, b_ada, norm_g, w_in, b_f, q_norm_g, k_norm_g, conv_w,
           w_attn_out, w_conv_out, w_o):
    b, s, d = x.shape
    ada3 = _ada(c, w_ada, b_ada)

    wn, wt, wa, wb, wo = _prep(w_in.T, w_attn_out, w_conv_out, w_o)

    bft = jnp.broadcast_to(jnp.concatenate([b_f, b_f])[:, None], (2 * HEADS, TM))
    gq = (jnp.tile(q_norm_g, HEADS) * HEAD_DIM ** -0.5).reshape(ATTN_W, 1)
    gk = jnp.tile(k_norm_g, HEADS).reshape(ATTN_W, 1)

    tok = np.arange(TM)
    upper = (tok[:, None] <= tok[None, :]).astype(np.float32)
    ut = jnp.asarray(np.concatenate([upper, np.ones((TM, LANES), np.float32)], axis=1), BF16)

    k, kx, qt, cq, vt, za, sga, mb = _proj(
        x, ada3, norm_g.reshape(1, d), wn, wt, bft, gq, gk, conv_w, wb, ut)
    logit_bound = HEAD_DIM ** 0.5 * jnp.max(jnp.abs(q_norm_g)) * jnp.max(jnp.abs(k_norm_g))
    cend = cq[:, :, TK - 1::TK].reshape(-1)
    args = (cend, logit_bound.reshape(1), qt, cq, k, kx, vt, za, sga, mb, x, ada3, wa, wo)
    return lax.cond(logit_bound <= MAX_RAW_LOGIT,
                    functools.partial(_attn, False), functools.partial(_attn, True), *args)


@jax.jit
def kernel(x, c, w_ada, b_ada, norm_g, w_in, b_f, q_norm_g, k_norm_g, conv_w,
           w_attn_out, w_conv_out, w_o):
    for i in range(w_ada.shape[0]):
        x = _layer(x, c, w_ada[i], b_ada[i], norm_g[i], w_in[i], b_f[i], q_norm_g[i],
                   k_norm_g[i], conv_w[i], w_attn_out[i], w_conv_out[i], w_o[i])
    return x
```

```python
import functools

import jax
import jax.numpy as jnp
import numpy as np
from jax import lax
from jax.experimental import pallas as pl
from jax.experimental.pallas import tpu as pltpu

D_MODEL = 1024
HEADS = 8
HEAD_DIM = 64
ATTN_W = HEADS * HEAD_DIM
CONV_W = 512
EPS = 1e-6

LANES = 128
TM = 1024
TQ = 512
TK = 256
CUM_PARTS = 3
ONES_LANE = CUM_PARTS * HEADS
V_ROWS = HEAD_DIM + 16
SCORE_LOOKAHEAD = 4
UNITS_PER_LONG_ITER = 16
UNITS_PER_ITER = 8
NEG_BIG = -1e30
MAX_RAW_LOGIT = 40.0
SKIP_LOG_WEIGHT = 30.0
LOGIT_BOUND_MARGIN = 1.05
VMEM_LIMIT = 62 * 1024 * 1024

F32 = jnp.float32
BF16 = jnp.bfloat16


def _log_sigmoid(x):
    return jnp.minimum(x, 0.0) - jnp.log(1.0 + jnp.exp(-jnp.abs(x)))


def _sigmoid(x):
    return 0.5 * jnp.tanh(0.5 * x) + 0.5


def _silu(x):
    hx = 0.5 * x
    return hx * jnp.tanh(hx) + hx


def _dot(a, b):
    return jnp.dot(a, b, preferred_element_type=F32)


def _dot_nt(a, b):
    return lax.dot_general(a, b, (((1,), (1,)), ((), ())), preferred_element_type=F32)


def _dot_tn(a, b):
    return lax.dot_general(a, b, (((0,), (0,)), ((), ())), preferred_element_type=F32)


def _ada_kernel(c_ref, w_ref, b_ref, o_ref):
    o_ref[:, 0, :] = _dot(c_ref[...].astype(BF16), w_ref[...].astype(BF16)) + b_ref[...]


def _ada(c, w_ada, b_ada):
    b, d = c.shape
    n = w_ada.shape[1]
    return pl.pallas_call(
        _ada_kernel,
        grid=(n // d,),
        in_specs=[pl.BlockSpec((b, d), lambda j: (0, 0)),
                  pl.BlockSpec((d, d), lambda j: (0, j)),
                  pl.BlockSpec((1, d), lambda j: (0, j))],
        out_specs=pl.BlockSpec((b, 1, d), lambda j: (0, 0, j)),
        out_shape=jax.ShapeDtypeStruct((b, 1, n), F32),
        name="ada",
    )(c, w_ada, b_ada.reshape(1, n))


N_GB = 0
N_GC = N_GB + CONV_W
N_U = N_GC + CONV_W
N_ZB = N_U + CONV_W
N_GA = N_ZB + CONV_W
N_GB2 = N_GA + D_MODEL
N_END = N_GB2 + D_MODEL
T_Q = 0
T_K = T_Q + ATTN_W
T_V = T_K + ATTN_W
T_ZA = T_V + ATTN_W
T_F = T_ZA + ATTN_W
T_END = T_F + 2 * HEADS

IN_SIZES = (ATTN_W, ATTN_W, ATTN_W, HEADS, ATTN_W, CONV_W, CONV_W, CONV_W, CONV_W, D_MODEL, D_MODEL)
IN_OFFS = tuple(sum(IN_SIZES[:i]) for i in range(len(IN_SIZES) + 1))
PREP_COLS = 256


def _prep_kernel(w_ref, wa_ref, wb_ref, wo_ref, wn_ref, wt_ref, wa16_ref, wb16_ref, wo16_ref):
    rows = lambda i, j: w_ref[IN_OFFS[i]:IN_OFFS[j], :]
    f = rows(3, 4)
    wn_ref[...] = rows(5, 11).astype(BF16).T
    wt_ref[T_Q:T_ZA, :] = rows(0, 3).astype(BF16)
    wt_ref[T_ZA:T_F, :] = rows(4, 5).astype(BF16)
    wt_ref[T_F:T_END, :] = jnp.concatenate([f, f], axis=0).astype(BF16)
    wa16_ref[...] = wa_ref[...].astype(BF16)
    wb16_ref[...] = wb_ref[...].astype(BF16)
    wo16_ref[...] = wo_ref[...].astype(BF16)


def _prep(w_t, w_attn_out, w_conv_out, w_o):
    n, d = w_t.shape
    steps = d // PREP_COLS
    row_block = lambda w: pl.BlockSpec((w.shape[0] // steps, w.shape[1]), lambda i: (i, 0))
    sides = (w_attn_out, w_conv_out, w_o)
    return pl.pallas_call(
        _prep_kernel,
        grid=(steps,),
        in_specs=[pl.BlockSpec((n, PREP_COLS), lambda i: (0, i))] + [row_block(w) for w in sides],
        out_specs=[pl.BlockSpec((PREP_COLS, N_END), lambda i: (i, 0)),
                   pl.BlockSpec((T_END, PREP_COLS), lambda i: (0, i))] + [row_block(w) for w in sides],
        out_shape=[jax.ShapeDtypeStruct((d, N_END), BF16),
                   jax.ShapeDtypeStruct((T_END, d), BF16)]
                  + [jax.ShapeDtypeStruct(w.shape, BF16) for w in sides],
        compiler_params=pltpu.CompilerParams(vmem_limit_bytes=VMEM_LIMIT),
        name="prep",
    )(w_t, *sides)


def _proj_kernel(xfirst_ref, xnext_ref, ada_ref, adanext_ref, ng_ref, wn_ref, wt_ref, bft_ref, gq_ref,
                 gk_ref, cw_ref, wb_ref, ut_ref,
                 k_ref, kx_ref, qt_ref, cq_ref, vt_ref, za_ref, sga_ref, mb_ref,
                 ct_ref, cu_ref, h_ref, gc_ref):
    s = pl.program_id(1)
    slot = lax.rem(s, 2)

    def modulated_norm(xr, ar):
        x = xr[0]
        shift = ar[0, :, 0:D_MODEL]
        scale = ar[0, :, D_MODEL:2 * D_MODEL]
        xn = x * lax.rsqrt(jnp.mean(x * x, axis=-1, keepdims=True) + EPS)
        return (xn * ng_ref[...] * (1.0 + scale) + shift).astype(BF16)

    @pl.when(s == 0)
    def _():
        ct_ref[...] = jnp.zeros_like(ct_ref)
        cu_ref[0:8, :] = jnp.zeros((8, CONV_W), F32)

    @pl.when((pl.program_id(0) == 0) & (s == 0))
    def _():
        h0 = modulated_norm(xfirst_ref, ada_ref)
        h_ref[0] = h0
        gc_ref[...] = _dot(h0, wn_ref[:, N_GC:N_U])

    nat = lambda lo, hi: _dot(h_ref[slot], wn_ref[:, lo:hi])
    trn = lambda lo, hi: _dot_nt(wt_ref[lo:hi, :], h_ref[slot])

    gc = gc_ref[...]
    u = nat(N_U, N_ZB)

    cu = gc * u
    cu_ref[8:8 + TM, :] = cu
    conv = (cw_ref[2:3, :] * cu + cw_ref[1:2, :] * cu_ref[7:7 + TM, :]
            + cw_ref[0:1, :] * cu_ref[6:6 + TM, :])
    cu_ref[0:8, :] = cu[TM - 8:TM, :]
    gb = nat(N_GB, N_GC)
    zb = nat(N_ZB, N_GA)

    ga = nat(N_GA, N_GB2)

    gb2 = nat(N_GB2, N_END)
    ob = (gb * conv * _silu(zb)).astype(BF16)
    mbr = _dot(ob, wb_ref[...])

    sga_ref[0] = _sigmoid(ga).astype(BF16)
    zf = trn(T_ZA, T_END)
    qraw = trn(T_Q, T_K)

    mb_ref[0] = (_sigmoid(gb2) * mbr).astype(BF16)
    lft = _log_sigmoid(zf[ATTN_W:, :] + bft_ref[...])
    lft_lo = lft - lft.astype(BF16).astype(F32)
    row = lax.broadcasted_iota(jnp.int32, (2 * HEADS, TM), 0)
    ct = _dot(jnp.where(row < HEADS, lft, lft_lo).astype(BF16), ut_ref[...])

    za_ref[0] = _silu(zf[0:ATTN_W, :]).astype(BF16)
    ct = ct[0:HEADS, :] + ct[HEADS:2 * HEADS, :]
    carry = ct_ref[...]
    cq = ct[:, 0:TM] + jnp.concatenate([carry] * (TM // LANES), axis=1)
    cq_ref[0] = cq
    ct_ref[...] = carry + ct[:, TM:TM + LANES]
    kraw = trn(T_K, T_V)

    ck = jnp.concatenate([cq, jnp.zeros((LANES - HEADS, TM), F32)], axis=0).T
    ck = ck + pltpu.roll(ck, HEADS, 1) + pltpu.roll(ck, 2 * HEADS, 1)
    lane = lax.broadcasted_iota(jnp.int32, (TM, LANES), 1)
    r1 = ck - ck.astype(BF16).astype(F32)
    r2 = r1 - r1.astype(BF16).astype(F32)
    piece = jnp.where(lane < HEADS, ck, jnp.where(lane < 2 * HEADS, r1, r2))
    is_one = (lane >= ONES_LANE) & (lane < ONES_LANE + CUM_PARTS)
    kx_ref[0] = jnp.where(lane < ONES_LANE, -piece, jnp.where(is_one, 1.0, 0.0)).astype(BF16)

    def head_norm(raw, gain_ref):
        r3 = raw.reshape(HEADS, HEAD_DIM, TM)
        rn = r3 * lax.rsqrt(jnp.mean(r3 * r3, axis=1, keepdims=True) + EPS)
        return (rn.reshape(ATTN_W, TM) * gain_ref[...]).astype(BF16)

    qt_ref[0] = head_norm(qraw, gq_ref)
    vraw = trn(T_V, T_ZA)

    kt = head_norm(kraw, gk_ref)
    for pair in range(HEADS // 2):
        k_ref[0, pair] = kt[pair * LANES:(pair + 1) * LANES, :].T

    vt = vraw.astype(BF16)
    for i in range(TM // TK):
        for hd in range(HEADS):
            vt_ref[0, i, hd, 0:HEAD_DIM, :] = (
                vt[hd * HEAD_DIM:(hd + 1) * HEAD_DIM, i * TK:(i + 1) * TK])
            vt_ref[0, i, hd, HEAD_DIM:V_ROWS, :] = jnp.ones((V_ROWS - HEAD_DIM, TK), BF16)

    h_next = modulated_norm(xnext_ref, adanext_ref)
    h_ref[1 - slot] = h_next
    gc_ref[...] = _dot(h_next, wn_ref[:, N_GC:N_U])


def _proj(x, ada3, ng, wn, wt, bft, gq, gk, cw, wb, ut):
    b, s, d = x.shape
    const = lambda shape: pl.BlockSpec(shape, lambda bi, si: (0,) * len(shape))
    out_shape = [
        jax.ShapeDtypeStruct((b, HEADS // 2, s, LANES), BF16),
        jax.ShapeDtypeStruct((b, s, LANES), BF16),
        jax.ShapeDtypeStruct((b, ATTN_W, s), BF16),
        jax.ShapeDtypeStruct((b, HEADS, s), F32),
        jax.ShapeDtypeStruct((b, s // TK, HEADS, V_ROWS, TK), BF16),
        jax.ShapeDtypeStruct((b, ATTN_W, s), BF16),
        jax.ShapeDtypeStruct((b, s, d), BF16),
        jax.ShapeDtypeStruct((b, s, d), BF16),
    ]
    out_specs = [
        pl.BlockSpec((1, HEADS // 2, TM, LANES), lambda bi, si: (bi, 0, si, 0)),
        pl.BlockSpec((1, TM, LANES), lambda bi, si: (bi, si, 0)),
        pl.BlockSpec((1, ATTN_W, TM), lambda bi, si: (bi, 0, si)),
        pl.BlockSpec((1, HEADS, TM), lambda bi, si: (bi, 0, si)),
        pl.BlockSpec((1, TM // TK, HEADS, V_ROWS, TK), lambda bi, si: (bi, si, 0, 0, 0)),
        pl.BlockSpec((1, ATTN_W, TM), lambda bi, si: (bi, 0, si)),
        pl.BlockSpec((1, TM, d), lambda bi, si: (bi, si, 0)),
        pl.BlockSpec((1, TM, d), lambda bi, si: (bi, si, 0)),
    ]
    tiles = s // TM
    assert tiles % 2 == 0
    next_b = lambda bi, si: jnp.minimum(bi + (si + 1) // tiles, b - 1)
    next_s = lambda bi, si: jnp.where((bi == b - 1) & (si == tiles - 1), si, (si + 1) % tiles)
    in_specs = [
        pl.BlockSpec((1, TM, d), lambda bi, si: (0, 0, 0)),
        pl.BlockSpec((1, TM, d), lambda bi, si: (next_b(bi, si), next_s(bi, si), 0)),
        pl.BlockSpec((1, 1, 3 * d), lambda bi, si: (bi, 0, 0)),
        pl.BlockSpec((1, 1, 3 * d), lambda bi, si: (next_b(bi, si), 0, 0)),
        const(ng.shape), const(wn.shape), const(wt.shape), const(bft.shape), const(gq.shape),
        const(gk.shape), const(cw.shape), const(wb.shape), const(ut.shape),
    ]
    return pl.pallas_call(
        _proj_kernel,
        grid=(b, s // TM),
        in_specs=in_specs,
        out_specs=out_specs,
        out_shape=out_shape,
        scratch_shapes=[pltpu.VMEM((HEADS, LANES), F32),
                        pltpu.VMEM((TM + 8, CONV_W), F32),
                        pltpu.VMEM((2, TM, D_MODEL), BF16),
                        pltpu.VMEM((TM, CONV_W), F32)],
        compiler_params=pltpu.CompilerParams(
            dimension_semantics=("arbitrary", "arbitrary"),
            vmem_limit_bytes=VMEM_LIMIT),
        name="proj",
    )(x, x, ada3, ada3, ng, wn, wt, bft, gq, gk, cw, wb, ut)


def _attn_kernel(cend_ref, bound_ref, qt_ref, cq_ref, k_ref, kx_ref, vt_ref, za_ref, sga_ref, mb_ref,
                 x_ref, gate_ref, wa_ref, wo_ref, o_ref, rhs_ref, m_ref, acc_ref, *, running_max):
    bi = pl.program_id(0)
    qi = pl.program_id(1)
    blocks_per_tile = TQ // TK
    n_blocks = kx_ref.shape[1] // TK
    i32 = jnp.int32

    last_before = qi * blocks_per_tile - 1
    slack = LOGIT_BOUND_MARGIN * 2.0 * bound_ref[0] + SKIP_LOG_WEIGHT

    first = []
    for h in range(HEADS):
        base = (bi * HEADS + h) * n_blocks
        threshold = cend_ref[base + jnp.maximum(last_before, 0)] + slack
        skipped = i32(0)
        for j in range(n_blocks - blocks_per_tile - 1):
            skipped = skipped + (cend_ref[base + j] >= threshold).astype(i32)
        first.append(skipped)
    upto, shift, total = [], [], i32(0)
    for h in range(HEADS):
        shift.append(first[h] - total)
        total = total + (qi * blocks_per_tile - first[h])
        upto.append(total)

    row = lax.broadcasted_iota(jnp.int32, (LANES, TQ), 0)
    for h in range(HEADS):
        pair, half = divmod(h, 2)
        qp = qt_ref[0, pair * LANES:(pair + 1) * LANES, :]
        mine = (row >= half * HEAD_DIM) & (row < (half + 1) * HEAD_DIM)
        rhs_ref[h, 0:LANES, :] = qp * jnp.where(mine, 1.0, 0.0).astype(BF16)
        cq = cq_ref[0, h:h + 1, :]
        r1 = cq - cq.astype(BF16).astype(F32)
        r2 = r1 - r1.astype(BF16).astype(F32)
        sel = (row == h) | (row == HEADS + h) | (row == 2 * HEADS + h)
        f = jnp.where(row == ONES_LANE, cq,
                      jnp.where(row == ONES_LANE + 1, r1,
                                jnp.where(row == ONES_LANE + 2, r2, 0.0)))
        rhs_ref[h, LANES:2 * LANES, :] = jnp.where(sel, 1.0, f).astype(BF16)
    rhs_ref[HEADS] = jnp.zeros(rhs_ref.shape[1:], BF16)

    if running_max:
        m_ref[...] = jnp.full(m_ref.shape, NEG_BIG, F32)
    acc_ref[...] = jnp.zeros_like(acc_ref)

    def scores(j, h, q_lo):
        start = pl.multiple_of(j * TK, TK)
        pair = h // 2 if isinstance(h, int) else jnp.minimum(h, HEADS - 1) // 2
        lhs = jnp.concatenate([k_ref[0, pair, pl.ds(start, TK), :],
                               kx_ref[0, pl.ds(start, TK), :]], axis=1)
        return _dot(lhs, rhs_ref[h, :, q_lo:TQ])

    def run_units(units):
        ahead = [scores(j, h, q_lo) for j, h, _, q_lo in units[:SCORE_LOOKAHEAD]]
        for i, (j, h, masked, q_lo) in enumerate(units):
            sc = ahead.pop(0)
            if i + SCORE_LOOKAHEAD < len(units):
                jn, hn, _, qn = units[i + SCORE_LOOKAHEAD]
                ahead.append(scores(jn, hn, qn))
            if masked:
                kpos = lax.broadcasted_iota(jnp.int32, sc.shape, 0)
                qpos = lax.broadcasted_iota(jnp.int32, sc.shape, 1)
                sc = jnp.where(kpos <= qpos, sc, NEG_BIG)
            hv = h if isinstance(h, int) else jnp.minimum(h, HEADS - 1)
            if running_max:
                m_prev = m_ref[h, :, q_lo:TQ]
                m_new = jnp.maximum(m_prev, jnp.max(sc, axis=0, keepdims=True))
                m_ref[h, :, q_lo:TQ] = m_new
                p = jnp.exp(sc - m_new).astype(BF16)
                acc_ref[h, :, q_lo:TQ] = (jnp.exp(m_prev - m_new) * acc_ref[h, :, q_lo:TQ]
                                          + _dot(vt_ref[0, j, hv], p))
            else:
                p = jnp.exp(sc).astype(BF16)
                acc_ref[h, :, q_lo:TQ] += _dot(vt_ref[0, j, hv], p)

    def list_units(start, count):
        units = []
        for t in range(count):
            u = start + t
            h = i32(0)
            off = shift[0]
            for g in range(HEADS):
                past = u >= upto[g]
                h = h + past.astype(i32)
                if g + 1 < HEADS:
                    off = jnp.where(past, shift[g + 1], off)
            j = jnp.where(u >= total, 0, u + off)
            units.append((j, h, False, 0))
        return units

    def long_body(i, carry):
        run_units(list_units(i * UNITS_PER_LONG_ITER, UNITS_PER_LONG_ITER))
        return carry

    n_short = (total + UNITS_PER_ITER - 1) // UNITS_PER_ITER
    n_long = n_short // (UNITS_PER_LONG_ITER // UNITS_PER_ITER)
    lax.fori_loop(0, n_long, long_body, 0)
    done = n_long * UNITS_PER_LONG_ITER

    def short_body(i, carry):
        run_units(list_units(done + i * UNITS_PER_ITER, UNITS_PER_ITER))
        return carry

    lax.fori_loop(0, n_short - n_long * (UNITS_PER_LONG_ITER // UNITS_PER_ITER), short_body, 0)
    run_units([(qi * blocks_per_tile + t, h, True, t * TK)
               for t in range(blocks_per_tile) for h in range(HEADS)])

    acc = jnp.concatenate(
        [acc_ref[h, 0:HEAD_DIM, :] * (1.0 / acc_ref[h, HEAD_DIM:HEAD_DIM + 1, :])
         for h in range(HEADS)], axis=0)
    at = (acc * za_ref[0].astype(F32)).astype(BF16)
    ya = _dot_tn(at, wa_ref[...])
    merged = (sga_ref[0].astype(F32) * ya + mb_ref[0].astype(F32)).astype(BF16)
    o_ref[0] = x_ref[0] + gate_ref[0] * _dot(merged, wo_ref[...])


def _attn(running_max, cend, bound, qt, cq, k, kx, vt, za, sga, mb, x, ada3, wa, wo):
    b, s, d = x.shape
    const = lambda shape: pl.BlockSpec(shape, lambda bi, qi: (0,) * len(shape))
    in_specs = [
        pl.BlockSpec(memory_space=pltpu.SMEM),
        pl.BlockSpec(memory_space=pltpu.SMEM),
        pl.BlockSpec((1, ATTN_W, TQ), lambda bi, qi: (bi, 0, qi)),
        pl.BlockSpec((1, HEADS, TQ), lambda bi, qi: (bi, 0, qi)),
        pl.BlockSpec((1, HEADS // 2, s, LANES), lambda bi, qi: (bi, 0, 0, 0)),
        pl.BlockSpec((1, s, LANES), lambda bi, qi: (bi, 0, 0)),
        pl.BlockSpec((1, s // TK, HEADS, V_ROWS, TK), lambda bi, qi: (bi, 0, 0, 0, 0)),
        pl.BlockSpec((1, ATTN_W, TQ), lambda bi, qi: (bi, 0, qi)),
        pl.BlockSpec((1, TQ, d), lambda bi, qi: (bi, qi, 0)),
        pl.BlockSpec((1, TQ, d), lambda bi, qi: (bi, qi, 0)),
        pl.BlockSpec((1, TQ, d), lambda bi, qi: (bi, qi, 0)),
        pl.BlockSpec((1, 1, d), lambda bi, qi: (bi, 0, 2)),
        const(wa.shape), const(wo.shape),
    ]
    return pl.pallas_call(
        functools.partial(_attn_kernel, running_max=running_max),
        grid=(b, s // TQ),
        in_specs=in_specs,
        out_specs=pl.BlockSpec((1, TQ, d), lambda bi, qi: (bi, qi, 0)),
        out_shape=jax.ShapeDtypeStruct((b, s, d), F32),
        scratch_shapes=[pltpu.VMEM((HEADS + 1, 2 * LANES, TQ), BF16),
                        pltpu.VMEM((HEADS + 1, 1, TQ), F32),
                        pltpu.VMEM((HEADS + 1, V_ROWS, TQ), F32)],
        compiler_params=pltpu.CompilerParams(
            dimension_semantics=("arbitrary", "arbitrary"),
            vmem_limit_bytes=VMEM_LIMIT),
        name="attn_running_max" if running_max else "attn",
    )(cend, bound, qt, cq, k, kx, vt, za, sga, mb, x, ada3, wa, wo)


def _layer(x, c, w_ada, b_ada, norm_g, w_in, b_f, q_norm_g, k_norm_g, conv_w,
           w_attn_out, w_conv_out, w_o):
    b, s, d = x.shape
    ada3 = _ada(c, w_ada, b_ada)

    wn, wt, wa, wb, wo = _prep(w_in.T, w_attn_out, w_conv_out, w_o)

    bft = jnp.broadcast_to(jnp.concatenate([b_f, b_f])[:, None], (2 * HEADS, TM))
    gq = (jnp.tile(q_norm_g, HEADS) * HEAD_DIM ** -0.5).reshape(ATTN_W, 1)
    gk = jnp.tile(k_norm_g, HEADS).reshape(ATTN_W, 1)

    tok = np.arange(TM)
    upper = (tok[:, None] <= tok[None, :]).astype(np.float32)
    ut = jnp.asarray(np.concatenate([upper, np.ones((TM, LANES), np.float32)], axis=1), BF16)

    k, kx, qt, cq, vt, za, sga, mb = _proj(
        x, ada3, norm_g.reshape(1, d), wn, wt, bft, gq, gk, conv_w, wb, ut)
    logit_bound = HEAD_DIM ** 0.5 * jnp.max(jnp.abs(q_norm_g)) * jnp.max(jnp.abs(k_norm_g))
    cend = cq[:, :, TK - 1::TK].reshape(-1)
    args = (cend, logit_bound.reshape(1), qt, cq, k, kx, vt, za, sga, mb, x, ada3, wa, wo)
    return lax.cond(logit_bound <= MAX_RAW_LOGIT,
                    functools.partial(_attn, False), functools.partial(_attn, True), *args)


@jax.jit
def kernel(x, c, w_ada, b_ada, norm_g, w_in, b_f, q_norm_g, k_norm_g, conv_w,
           w_attn_out, w_conv_out, w_o):
    for i in range(w_ada.shape[0]):
        x = _layer(x, c, w_ada[i], b_ada[i], norm_g[i], w_in[i], b_f[i], q_norm_g[i],
                   k_norm_g[i], conv_w[i], w_attn_out[i], w_conv_out[i], w_o[i])
    return x
```

```python
import functools

import jax
import jax.numpy as jnp
import numpy as np
from jax import lax
from jax.experimental import pallas as pl
from jax.experimental.pallas import tpu as pltpu

D_MODEL = 1024
HEADS = 8
HEAD_DIM = 64
ATTN_W = HEADS * HEAD_DIM
CONV_W = 512
EPS = 1e-6

LANES = 128
TM = 1024
TQ = 512
TK = 256
CUM_PARTS = 3
ONES_LANE = CUM_PARTS * HEADS
V_ROWS = HEAD_DIM + 16
SCORE_LOOKAHEAD = 4
UNITS_PER_LONG_ITER = 16
UNITS_PER_ITER = 8
NEG_BIG = -1e30
MAX_RAW_LOGIT = 40.0
SKIP_LOG_WEIGHT = 30.0
LOGIT_BOUND_MARGIN = 1.05
VMEM_LIMIT = 62 * 1024 * 1024

F32 = jnp.float32
BF16 = jnp.bfloat16


def _log_sigmoid(x):
    return jnp.minimum(x, 0.0) - jnp.log(1.0 + jnp.exp(-jnp.abs(x)))


def _sigmoid_of_twice(hx):
    return 0.5 * jnp.tanh(hx) + 0.5


def _silu_of_twice(hx):
    return hx * jnp.tanh(hx) + hx


def _dot(a, b):
    return jnp.dot(a, b, preferred_element_type=F32)


def _dot_nt(a, b):
    return lax.dot_general(a, b, (((1,), (1,)), ((), ())), preferred_element_type=F32)


def _dot_tn(a, b):
    return lax.dot_general(a, b, (((0,), (0,)), ((), ())), preferred_element_type=F32)


def _ada_kernel(c_ref, w_ref, b_ref, o_ref):
    o_ref[:, 0, :] = _dot(c_ref[...].astype(BF16), w_ref[...].astype(BF16)) + b_ref[...]


def _ada(c, w_ada, b_ada):
    b, d = c.shape
    n = w_ada.shape[1]
    return pl.pallas_call(
        _ada_kernel,
        grid=(n // d,),
        in_specs=[pl.BlockSpec((b, d), lambda j: (0, 0)),
                  pl.BlockSpec((d, d), lambda j: (0, j)),
                  pl.BlockSpec((1, d), lambda j: (0, j))],
        out_specs=pl.BlockSpec((b, 1, d), lambda j: (0, 0, j)),
        out_shape=jax.ShapeDtypeStruct((b, 1, n), F32),
        name="ada",
    )(c, w_ada, b_ada.reshape(1, n))


N_GB = 0
N_GC = N_GB + CONV_W
N_U = N_GC + CONV_W
N_ZB = N_U + CONV_W
N_GA = N_ZB + CONV_W
N_GB2 = N_GA + D_MODEL
N_END = N_GB2 + D_MODEL
T_Q = 0
T_K = T_Q + ATTN_W
T_V = T_K + ATTN_W
T_ZA = T_V + ATTN_W
T_F = T_ZA + ATTN_W
T_END = T_F + 2 * HEADS

IN_SIZES = (ATTN_W, ATTN_W, ATTN_W, HEADS, ATTN_W, CONV_W, CONV_W, CONV_W, CONV_W, D_MODEL, D_MODEL)
IN_OFFS = tuple(sum(IN_SIZES[:i]) for i in range(len(IN_SIZES) + 1))
PREP_COLS = 256


def _prep_kernel(w_ref, wa_ref, wb_ref, wo_ref, wn_ref, wt_ref, wa16_ref, wb16_ref, wo16_ref):
    rows = lambda i, j: w_ref[IN_OFFS[i]:IN_OFFS[j], :]
    f = rows(3, 4)
    wn_ref[...] = jnp.concatenate([rows(5, 8), 0.5 * rows(8, 11)], axis=0).astype(BF16).T
    wt_ref[T_Q:T_ZA, :] = rows(0, 3).astype(BF16)
    wt_ref[T_ZA:T_F, :] = (0.5 * rows(4, 5)).astype(BF16)
    wt_ref[T_F:T_END, :] = jnp.concatenate([f, f], axis=0).astype(BF16)
    wa16_ref[...] = wa_ref[...].astype(BF16)
    wb16_ref[...] = wb_ref[...].astype(BF16)
    wo16_ref[...] = wo_ref[...].astype(BF16)


def _prep(w_t, w_attn_out, w_conv_out, w_o):
    n, d = w_t.shape
    steps = d // PREP_COLS
    row_block = lambda w: pl.BlockSpec((w.shape[0] // steps, w.shape[1]), lambda i: (i, 0))
    sides = (w_attn_out, w_conv_out, w_o)
    return pl.pallas_call(
        _prep_kernel,
        grid=(steps,),
        in_specs=[pl.BlockSpec((n, PREP_COLS), lambda i: (0, i))] + [row_block(w) for w in sides],
        out_specs=[pl.BlockSpec((PREP_COLS, N_END), lambda i: (i, 0)),
                   pl.BlockSpec((T_END, PREP_COLS), lambda i: (0, i))] + [row_block(w) for w in sides],
        out_shape=[jax.ShapeDtypeStruct((d, N_END), BF16),
                   jax.ShapeDtypeStruct((T_END, d), BF16)]
                  + [jax.ShapeDtypeStruct(w.shape, BF16) for w in sides],
        compiler_params=pltpu.CompilerParams(vmem_limit_bytes=VMEM_LIMIT),
        name="prep",
    )(w_t, *sides)


def _proj_kernel(xfirst_ref, xnext_ref, ada_ref, adanext_ref, ng_ref, wn_ref, wt_ref, bft_ref, gqk_ref,
                 cw_ref, wb_ref, ut_ref,
                 k_ref, kx_ref, qt_ref, cq_ref, vt_ref, za_ref, sga_ref, mb_ref,
                 ct_ref, cu_ref, h_ref, gc_ref):
    s = pl.program_id(1)
    slot = lax.rem(s, 2)

    def modulated_norm(xr, ar):
        x = xr[0]
        shift = ar[0, :, 0:D_MODEL]
        scale = ar[0, :, D_MODEL:2 * D_MODEL]
        xn = x * lax.rsqrt(jnp.mean(x * x, axis=-1, keepdims=True) + EPS)
        return (xn * (ng_ref[...] * (1.0 + scale)) + shift).astype(BF16)

    @pl.when(s == 0)
    def _():
        ct_ref[...] = jnp.zeros_like(ct_ref)
        cu_ref[0:8, :] = jnp.zeros((8, CONV_W), F32)

    @pl.when((pl.program_id(0) == 0) & (s == 0))
    def _():
        h0 = modulated_norm(xfirst_ref, ada_ref)
        h_ref[0] = h0
        gc_ref[...] = _dot(h0, wn_ref[:, N_GC:N_U])

    nat = lambda lo, hi: _dot(h_ref[slot], wn_ref[:, lo:hi])
    trn = lambda lo, hi: _dot_nt(wt_ref[lo:hi, :], h_ref[slot])

    gc = gc_ref[...]
    u = nat(N_U, N_ZB)

    cu = gc * u
    cu_ref[8:8 + TM, :] = cu
    conv = (cw_ref[2:3, :] * cu + cw_ref[1:2, :] * cu_ref[7:7 + TM, :]
            + cw_ref[0:1, :] * cu_ref[6:6 + TM, :])
    cu_ref[0:8, :] = cu[TM - 8:TM, :]
    gb = nat(N_GB, N_GC)
    zb = nat(N_ZB, N_GA)

    ga = nat(N_GA, N_GB2)

    gb2 = nat(N_GB2, N_END)
    ob = (gb * conv * _silu_of_twice(zb)).astype(BF16)
    mbr = _dot(ob, wb_ref[...])

    sga_ref[0] = _sigmoid_of_twice(ga).astype(BF16)
    zf = trn(T_ZA, T_END)
    qraw = trn(T_Q, T_K)

    mb_ref[0] = (_sigmoid_of_twice(gb2) * mbr).astype(BF16)
    lft = _log_sigmoid(zf[ATTN_W:, :] + bft_ref[...])
    lft_lo = lft - lft.astype(BF16).astype(F32)
    row = lax.broadcasted_iota(jnp.int32, (2 * HEADS, TM), 0)
    ct = _dot(jnp.where(row < HEADS, lft, lft_lo).astype(BF16), ut_ref[...])

    za_ref[0] = _silu_of_twice(zf[0:ATTN_W, :]).astype(BF16)
    ct = ct[0:HEADS, :] + ct[HEADS:2 * HEADS, :]
    carry = ct_ref[...]
    cq = ct[:, 0:TM] + jnp.concatenate([carry] * (TM // LANES), axis=1)
    cq_ref[0] = cq
    ct_ref[...] = carry + ct[:, TM:TM + LANES]
    kraw = trn(T_K, T_V)

    ck = jnp.concatenate([cq, jnp.zeros((LANES - HEADS, TM), F32)], axis=0).T
    ck = ck + pltpu.roll(ck, HEADS, 1) + pltpu.roll(ck, 2 * HEADS, 1)
    lane = lax.broadcasted_iota(jnp.int32, (TM, LANES), 1)
    r1 = ck - ck.astype(BF16).astype(F32)
    r2 = r1 - r1.astype(BF16).astype(F32)
    piece = jnp.where(lane < HEADS, ck, jnp.where(lane < 2 * HEADS, r1, r2))
    is_one = (lane >= ONES_LANE) & (lane < ONES_LANE + CUM_PARTS)
    kx_ref[0] = jnp.where(lane < ONES_LANE, -piece, jnp.where(is_one, 1.0, 0.0)).astype(BF16)

    def head_norm(raw):
        r3 = raw.reshape(HEADS, HEAD_DIM, TM)
        rn = r3 * lax.rsqrt(jnp.mean(r3 * r3, axis=1, keepdims=True) + EPS)
        return rn.reshape(ATTN_W, TM)

    qt_ref[0] = (head_norm(qraw) * gqk_ref[...]).astype(BF16)
    vraw = trn(T_V, T_ZA)

    kt = head_norm(kraw).astype(BF16)
    for pair in range(HEADS // 2):
        k_ref[0, pair] = kt[pair * LANES:(pair + 1) * LANES, :].T

    vt = vraw.astype(BF16)
    for i in range(TM // TK):
        for hd in range(HEADS):
            vt_ref[0, i, hd, 0:HEAD_DIM, :] = (
                vt[hd * HEAD_DIM:(hd + 1) * HEAD_DIM, i * TK:(i + 1) * TK])
            vt_ref[0, i, hd, HEAD_DIM:V_ROWS, :] = jnp.ones((V_ROWS - HEAD_DIM, TK), BF16)

    h_next = modulated_norm(xnext_ref, adanext_ref)
    h_ref[1 - slot] = h_next
    gc_ref[...] = _dot(h_next, wn_ref[:, N_GC:N_U])


def _proj(x, ada3, ng, wn, wt, bft, gqk, cw, wb, ut):
    b, s, d = x.shape
    const = lambda shape: pl.BlockSpec(shape, lambda bi, si: (0,) * len(shape))
    out_shape = [
        jax.ShapeDtypeStruct((b, HEADS // 2, s, LANES), BF16),
        jax.ShapeDtypeStruct((b, s, LANES), BF16),
        jax.ShapeDtypeStruct((b, ATTN_W, s), BF16),
        jax.ShapeDtypeStruct((b, HEADS, s), F32),
        jax.ShapeDtypeStruct((b, s // TK, HEADS, V_ROWS, TK), BF16),
        jax.ShapeDtypeStruct((b, ATTN_W, s), BF16),
        jax.ShapeDtypeStruct((b, s, d), BF16),
        jax.ShapeDtypeStruct((b, s, d), BF16),
    ]
    out_specs = [
        pl.BlockSpec((1, HEADS // 2, TM, LANES), lambda bi, si: (bi, 0, si, 0)),
        pl.BlockSpec((1, TM, LANES), lambda bi, si: (bi, si, 0)),
        pl.BlockSpec((1, ATTN_W, TM), lambda bi, si: (bi, 0, si)),
        pl.BlockSpec((1, HEADS, TM), lambda bi, si: (bi, 0, si)),
        pl.BlockSpec((1, TM // TK, HEADS, V_ROWS, TK), lambda bi, si: (bi, si, 0, 0, 0)),
        pl.BlockSpec((1, ATTN_W, TM), lambda bi, si: (bi, 0, si)),
        pl.BlockSpec((1, TM, d), lambda bi, si: (bi, si, 0)),
        pl.BlockSpec((1, TM, d), lambda bi, si: (bi, si, 0)),
    ]
    tiles = s // TM
    assert tiles % 2 == 0
    next_b = lambda bi, si: jnp.minimum(bi + (si + 1) // tiles, b - 1)
    next_s = lambda bi, si: jnp.where((bi == b - 1) & (si == tiles - 1), si, (si + 1) % tiles)
    in_specs = [
        pl.BlockSpec((1, TM, d), lambda bi, si: (0, 0, 0)),
        pl.BlockSpec((1, TM, d), lambda bi, si: (next_b(bi, si), next_s(bi, si), 0)),
        pl.BlockSpec((1, 1, 3 * d), lambda bi, si: (bi, 0, 0)),
        pl.BlockSpec((1, 1, 3 * d), lambda bi, si: (next_b(bi, si), 0, 0)),
        const(ng.shape), const(wn.shape), const(wt.shape), const(bft.shape), const(gqk.shape),
        const(cw.shape), const(wb.shape), const(ut.shape),
    ]
    return pl.pallas_call(
        _proj_kernel,
        grid=(b, s // TM),
        in_specs=in_specs,
        out_specs=out_specs,
        out_shape=out_shape,
        scratch_shapes=[pltpu.VMEM((HEADS, LANES), F32),
                        pltpu.VMEM((TM + 8, CONV_W), F32),
                        pltpu.VMEM((2, TM, D_MODEL), BF16),
                        pltpu.VMEM((TM, CONV_W), F32)],
        compiler_params=pltpu.CompilerParams(
            dimension_semantics=("arbitrary", "arbitrary"),
            vmem_limit_bytes=VMEM_LIMIT),
        name="proj",
    )(x, x, ada3, ada3, ng, wn, wt, bft, gqk, cw, wb, ut)


def _attn_kernel(cend_ref, bound_ref, qt_ref, cq_ref, k_ref, kx_ref, vt_ref, za_ref, sga_ref, mb_ref,
                 x_ref, gate_ref, wa_ref, wo_ref, o_ref, rhs_ref, m_ref, acc_ref, *, running_max):
    bi = pl.program_id(0)
    qi = pl.program_id(1)
    blocks_per_tile = TQ // TK
    n_blocks = kx_ref.shape[1] // TK
    i32 = jnp.int32

    last_before = qi * blocks_per_tile - 1
    slack = LOGIT_BOUND_MARGIN * 2.0 * bound_ref[0] + SKIP_LOG_WEIGHT

    first = []
    for h in range(HEADS):
        base = (bi * HEADS + h) * n_blocks
        threshold = cend_ref[base + jnp.maximum(last_before, 0)] + slack
        skipped = i32(0)
        for j in range(n_blocks - blocks_per_tile - 1):
            skipped = skipped + (cend_ref[base + j] >= threshold).astype(i32)
        first.append(skipped)
    upto, shift, total = [], [], i32(0)
    for h in range(HEADS):
        shift.append(first[h] - total)
        total = total + (qi * blocks_per_tile - first[h])
        upto.append(total)

    row = lax.broadcasted_iota(jnp.int32, (LANES, TQ), 0)
    for h in range(HEADS):
        pair, half = divmod(h, 2)
        qp = qt_ref[0, pair * LANES:(pair + 1) * LANES, :]
        mine = (row >= half * HEAD_DIM) & (row < (half + 1) * HEAD_DIM)
        rhs_ref[h, 0:LANES, :] = qp * jnp.where(mine, 1.0, 0.0).astype(BF16)
        cq = cq_ref[0, h:h + 1, :]
        r1 = cq - cq.astype(BF16).astype(F32)
        r2 = r1 - r1.astype(BF16).astype(F32)
        sel = (row == h) | (row == HEADS + h) | (row == 2 * HEADS + h)
        f = jnp.where(row == ONES_LANE, cq,
                      jnp.where(row == ONES_LANE + 1, r1,
                                jnp.where(row == ONES_LANE + 2, r2, 0.0)))
        rhs_ref[h, LANES:2 * LANES, :] = jnp.where(sel, 1.0, f).astype(BF16)
    rhs_ref[HEADS] = jnp.zeros(rhs_ref.shape[1:], BF16)

    if running_max:
        m_ref[...] = jnp.full(m_ref.shape, NEG_BIG, F32)
    acc_ref[...] = jnp.zeros_like(acc_ref)

    def scores(j, h, q_lo):
        start = pl.multiple_of(j * TK, TK)
        pair = h // 2 if isinstance(h, int) else jnp.minimum(h, HEADS - 1) // 2
        lhs = jnp.concatenate([k_ref[0, pair, pl.ds(start, TK), :],
                               kx_ref[0, pl.ds(start, TK), :]], axis=1)
        return _dot(lhs, rhs_ref[h, :, q_lo:TQ])

    def run_units(units):
        ahead = [scores(j, h, q_lo) for j, h, _, q_lo in units[:SCORE_LOOKAHEAD]]
        for i, (j, h, masked, q_lo) in enumerate(units):
            sc = ahead.pop(0)
            if i + SCORE_LOOKAHEAD < len(units):
                jn, hn, _, qn = units[i + SCORE_LOOKAHEAD]
                ahead.append(scores(jn, hn, qn))
            if masked:
                kpos = lax.broadcasted_iota(jnp.int32, sc.shape, 0)
                qpos = lax.broadcasted_iota(jnp.int32, sc.shape, 1)
                sc = jnp.where(kpos <= qpos, sc, NEG_BIG)
            hv = h if isinstance(h, int) else jnp.minimum(h, HEADS - 1)
            if running_max:
                m_prev = m_ref[h, :, q_lo:TQ]
                m_new = jnp.maximum(m_prev, jnp.max(sc, axis=0, keepdims=True))
                m_ref[h, :, q_lo:TQ] = m_new
                p = jnp.exp(sc - m_new).astype(BF16)
                acc_ref[h, :, q_lo:TQ] = (jnp.exp(m_prev - m_new) * acc_ref[h, :, q_lo:TQ]
                                          + _dot(vt_ref[0, j, hv], p))
            else:
                p = jnp.exp(sc).astype(BF16)
                acc_ref[h, :, q_lo:TQ] += _dot(vt_ref[0, j, hv], p)

    def list_units(start, count):
        units = []
        for t in range(count):
            u = start + t
            h = i32(0)
            off = shift[0]
            for g in range(HEADS):
                past = u >= upto[g]
                h = h + past.astype(i32)
                if g + 1 < HEADS:
                    off = jnp.where(past, shift[g + 1], off)
            j = jnp.where(u >= total, 0, u + off)
            units.append((j, h, False, 0))
        return units

    def long_body(i, carry):
        run_units(list_units(i * UNITS_PER_LONG_ITER, UNITS_PER_LONG_ITER))
        return carry

    n_short = (total + UNITS_PER_ITER - 1) // UNITS_PER_ITER
    n_long = n_short // (UNITS_PER_LONG_ITER // UNITS_PER_ITER)
    lax.fori_loop(0, n_long, long_body, 0)
    done = n_long * UNITS_PER_LONG_ITER

    def short_body(i, carry):
        run_units(list_units(done + i * UNITS_PER_ITER, UNITS_PER_ITER))
        return carry

    lax.fori_loop(0, n_short - n_long * (UNITS_PER_LONG_ITER // UNITS_PER_ITER), short_body, 0)
    run_units([(qi * blocks_per_tile + t, h, True, t * TK)
               for t in range(blocks_per_tile) for h in range(HEADS)])

    acc = jnp.concatenate(
        [acc_ref[h, 0:HEAD_DIM, :] * (1.0 / acc_ref[h, HEAD_DIM:HEAD_DIM + 1, :])
         for h in range(HEADS)], axis=0)
    at = (acc * za_ref[0].astype(F32)).astype(BF16)
    ya = _dot_tn(at, wa_ref[...])
    merged = (sga_ref[0].astype(F32) * ya + mb_ref[0].astype(F32)).astype(BF16)
    o_ref[0] = x_ref[0] + gate_ref[0] * _dot(merged, wo_ref[...])


def _attn(running_max, cend, bound, qt, cq, k, kx, vt, za, sga, mb, x, ada3, wa, wo):
    b, s, d = x.shape
    const = lambda shape: pl.BlockSpec(shape, lambda bi, qi: (0,) * len(shape))
    in_specs = [
        pl.BlockSpec(memory_space=pltpu.SMEM),
        pl.BlockSpec(memory_space=pltpu.SMEM),
        pl.BlockSpec((1, ATTN_W, TQ), lambda bi, qi: (bi, 0, qi)),
        pl.BlockSpec((1, HEADS, TQ), lambda bi, qi: (bi, 0, qi)),
        pl.BlockSpec((1, HEADS // 2, s, LANES), lambda bi, qi: (bi, 0, 0, 0)),
        pl.BlockSpec((1, s, LANES), lambda bi, qi: (bi, 0, 0)),
        pl.BlockSpec((1, s // TK, HEADS, V_ROWS, TK), lambda bi, qi: (bi, 0, 0, 0, 0)),
        pl.BlockSpec((1, ATTN_W, TQ), lambda bi, qi: (bi, 0, qi)),
        pl.BlockSpec((1, TQ, d), lambda bi, qi: (bi, qi, 0)),
        pl.BlockSpec((1, TQ, d), lambda bi, qi: (bi, qi, 0)),
        pl.BlockSpec((1, TQ, d), lambda bi, qi: (bi, qi, 0)),
        pl.BlockSpec((1, 1, d), lambda bi, qi: (bi, 0, 2)),
        const(wa.shape), const(wo.shape),
    ]
    return pl.pallas_call(
        functools.partial(_attn_kernel, running_max=running_max),
        grid=(b, s // TQ),
        in_specs=in_specs,
        out_specs=pl.BlockSpec((1, TQ, d), lambda bi, qi: (bi, qi, 0)),
        out_shape=jax.ShapeDtypeStruct((b, s, d), F32),
        scratch_shapes=[pltpu.VMEM((HEADS + 1, 2 * LANES, TQ), BF16),
                        pltpu.VMEM((HEADS + 1, 1, TQ), F32),
                        pltpu.VMEM((HEADS + 1, V_ROWS, TQ), F32)],
        compiler_params=pltpu.CompilerParams(
            dimension_semantics=("arbitrary", "arbitrary"),
            vmem_limit_bytes=VMEM_LIMIT),
        name="attn_running_max" if running_max else "attn",
    )(cend, bound, qt, cq, k, kx, vt, za, sga, mb, x, ada3, wa, wo)


def _layer(x, c, w_ada, b_ada, norm_g, w_in, b_f, q_norm_g, k_norm_g, conv_w,
           w_attn_out, w_conv_out, w_o):
    b, s, d = x.shape
    ada3 = _ada(c, w_ada, b_ada)

    wn, wt, wa, wb, wo = _prep(w_in.T, w_attn_out, w_conv_out, w_o)

    bft = jnp.broadcast_to(jnp.concatenate([b_f, b_f])[:, None], (2 * HEADS, TM))
    gqk = jnp.tile(q_norm_g * k_norm_g * HEAD_DIM ** -0.5, HEADS).reshape(ATTN_W, 1)

    tok = np.arange(TM)
    upper = (tok[:, None] <= tok[None, :]).astype(np.float32)
    ut = jnp.asarray(np.concatenate([upper, np.ones((TM, LANES), np.float32)], axis=1), BF16)

    k, kx, qt, cq, vt, za, sga, mb = _proj(
        x, ada3, norm_g.reshape(1, d), wn, wt, bft, gqk, conv_w, wb, ut)
    logit_bound = HEAD_DIM ** 0.5 * jnp.max(jnp.abs(q_norm_g)) * jnp.max(jnp.abs(k_norm_g))
    cend = cq[:, :, TK - 1::TK].reshape(-1)
    args = (cend, logit_bound.reshape(1), qt, cq, k, kx, vt, za, sga, mb, x, ada3, wa, wo)
    return lax.cond(logit_bound <= MAX_RAW_LOGIT,
                    functools.partial(_attn, False), functools.partial(_attn, True), *args)


@jax.jit
def kernel(x, c, w_ada, b_ada, norm_g, w_in, b_f, q_norm_g, k_norm_g, conv_w,
           w_attn_out, w_conv_out, w_o):
    for i in range(w_ada.shape[0]):
        x = _layer(x, c, w_ada[i], b_ada[i], norm_g[i], w_in[i], b_f[i], q_norm_g[i],
                   k_norm_g[i], conv_w[i], w_attn_out[i], w_conv_out[i], w_o[i])
    return x
```

```python
import functools

import jax
import jax.numpy as jnp
import numpy as np
from jax import lax
from jax.experimental import pallas as pl
from jax.experimental.pallas import tpu as pltpu

D_MODEL = 1024
HEADS = 8
HEAD_DIM = 64
ATTN_W = HEADS * HEAD_DIM
CONV_W = 512
EPS = 1e-6

LANES = 128
TM = 1024
TQ = 512
TK = 256
CUM_PARTS = 3
ONES_LANE = CUM_PARTS * HEADS
V_ROWS = HEAD_DIM + 16
SCORE_LOOKAHEAD = 4
UNITS_PER_LONG_ITER = 16
UNITS_PER_ITER = 8
NEG_BIG = -1e30
MAX_RAW_LOGIT = 40.0
SKIP_LOG_WEIGHT = 30.0
LOGIT_BOUND_MARGIN = 1.05
DIAG_LOGIT_MARGIN = 0.05
VMEM_LIMIT = 62 * 1024 * 1024

F32 = jnp.float32
BF16 = jnp.bfloat16


def _log_sigmoid(x):
    return jnp.minimum(x, 0.0) - jnp.log(1.0 + jnp.exp(-jnp.abs(x)))


def _sigmoid_of_twice(hx):
    return 0.5 * jnp.tanh(hx) + 0.5


def _silu_of_twice(hx):
    return hx * jnp.tanh(hx) + hx


def _dot(a, b):
    return jnp.dot(a, b, preferred_element_type=F32)


def _dot_nt(a, b):
    return lax.dot_general(a, b, (((1,), (1,)), ((), ())), preferred_element_type=F32)


def _dot_tn(a, b):
    return lax.dot_general(a, b, (((0,), (0,)), ((), ())), preferred_element_type=F32)


def _ada_kernel(c_ref, w_ref, b_ref, o_ref):
    o_ref[:, 0, :] = _dot(c_ref[...].astype(BF16), w_ref[...].astype(BF16)) + b_ref[...]


def _ada(c, w_ada, b_ada):
    b, d = c.shape
    n = w_ada.shape[1]
    return pl.pallas_call(
        _ada_kernel,
        grid=(n // d,),
        in_specs=[pl.BlockSpec((b, d), lambda j: (0, 0)),
                  pl.BlockSpec((d, d), lambda j: (0, j)),
                  pl.BlockSpec((1, d), lambda j: (0, j))],
        out_specs=pl.BlockSpec((b, 1, d), lambda j: (0, 0, j)),
        out_shape=jax.ShapeDtypeStruct((b, 1, n), F32),
        name="ada",
    )(c, w_ada, b_ada.reshape(1, n))


N_GB = 0
N_GC = N_GB + CONV_W
N_U = N_GC + CONV_W
N_ZB = N_U + CONV_W
N_GA = N_ZB + CONV_W
N_GB2 = N_GA + D_MODEL
N_END = N_GB2 + D_MODEL
T_Q = 0
T_K = T_Q + ATTN_W
T_V = T_K + ATTN_W
T_ZA = T_V + ATTN_W
T_F = T_ZA + ATTN_W
T_END = T_F + 2 * HEADS

IN_SIZES = (ATTN_W, ATTN_W, ATTN_W, HEADS, ATTN_W, CONV_W, CONV_W, CONV_W, CONV_W, D_MODEL, D_MODEL)
IN_OFFS = tuple(sum(IN_SIZES[:i]) for i in range(len(IN_SIZES) + 1))
PREP_COLS = 256


def _prep_kernel(w_ref, wa_ref, wb_ref, wo_ref, wn_ref, wt_ref, wa16_ref, wb16_ref, wo16_ref):
    rows = lambda i, j: w_ref[IN_OFFS[i]:IN_OFFS[j], :]
    f = rows(3, 4)
    wn_ref[...] = jnp.concatenate([rows(5, 8), 0.5 * rows(8, 11)], axis=0).astype(BF16).T
    wt_ref[T_Q:T_ZA, :] = rows(0, 3).astype(BF16)
    wt_ref[T_ZA:T_F, :] = (0.5 * rows(4, 5)).astype(BF16)
    wt_ref[T_F:T_END, :] = jnp.concatenate([f, f], axis=0).astype(BF16)
    wa16_ref[...] = wa_ref[...].astype(BF16)
    wb16_ref[...] = wb_ref[...].astype(BF16)
    wo16_ref[...] = wo_ref[...].astype(BF16)


def _prep(w_t, w_attn_out, w_conv_out, w_o):
    n, d = w_t.shape
    steps = d // PREP_COLS
    row_block = lambda w: pl.BlockSpec((w.shape[0] // steps, w.shape[1]), lambda i: (i, 0))
    sides = (w_attn_out, w_conv_out, w_o)
    return pl.pallas_call(
        _prep_kernel,
        grid=(steps,),
        in_specs=[pl.BlockSpec((n, PREP_COLS), lambda i: (0, i))] + [row_block(w) for w in sides],
        out_specs=[pl.BlockSpec((PREP_COLS, N_END), lambda i: (i, 0)),
                   pl.BlockSpec((T_END, PREP_COLS), lambda i: (0, i))] + [row_block(w) for w in sides],
        out_shape=[jax.ShapeDtypeStruct((d, N_END), BF16),
                   jax.ShapeDtypeStruct((T_END, d), BF16)]
                  + [jax.ShapeDtypeStruct(w.shape, BF16) for w in sides],
        compiler_params=pltpu.CompilerParams(vmem_limit_bytes=VMEM_LIMIT),
        name="prep",
    )(w_t, *sides)


def _proj_kernel(xfirst_ref, xnext_ref, ada_ref, adanext_ref, ng_ref, wn_ref, wt_ref, bft_ref, gqk_ref,
                 cw_ref, wb_ref, ut_ref,
                 k_ref, kx_ref, qt_ref, cq_ref, vt_ref, za_ref, sga_ref, mb_ref, dmin_ref,
                 ct_ref, cu_ref, h_ref, gc_ref):
    s = pl.program_id(1)
    slot = lax.rem(s, 2)

    def modulated_norm(xr, ar):
        x = xr[0]
        shift = ar[0, :, 0:D_MODEL]
        scale = ar[0, :, D_MODEL:2 * D_MODEL]
        xn = x * lax.rsqrt(jnp.mean(x * x, axis=-1, keepdims=True) + EPS)
        return (xn * (ng_ref[...] * (1.0 + scale)) + shift).astype(BF16)

    @pl.when(s == 0)
    def _():
        ct_ref[...] = jnp.zeros_like(ct_ref)
        cu_ref[0:8, :] = jnp.zeros((8, CONV_W), F32)

    @pl.when((pl.program_id(0) == 0) & (s == 0))
    def _():
        h0 = modulated_norm(xfirst_ref, ada_ref)
        h_ref[0] = h0
        gc_ref[...] = _dot(h0, wn_ref[:, N_GC:N_U])

    nat = lambda lo, hi: _dot(h_ref[slot], wn_ref[:, lo:hi])
    trn = lambda lo, hi: _dot_nt(wt_ref[lo:hi, :], h_ref[slot])

    gc = gc_ref[...]
    u = nat(N_U, N_ZB)

    cu = gc * u
    cu_ref[8:8 + TM, :] = cu
    conv = (cw_ref[2:3, :] * cu + cw_ref[1:2, :] * cu_ref[7:7 + TM, :]
            + cw_ref[0:1, :] * cu_ref[6:6 + TM, :])
    cu_ref[0:8, :] = cu[TM - 8:TM, :]
    gb = nat(N_GB, N_GC)
    zb = nat(N_ZB, N_GA)

    ga = nat(N_GA, N_GB2)

    gb2 = nat(N_GB2, N_END)
    ob = (gb * conv * _silu_of_twice(zb)).astype(BF16)
    mbr = _dot(ob, wb_ref[...])

    sga_ref[0] = _sigmoid_of_twice(ga).astype(BF16)
    zf = trn(T_ZA, T_END)
    qraw = trn(T_Q, T_K)

    mb_ref[0] = (_sigmoid_of_twice(gb2) * mbr).astype(BF16)
    lft = _log_sigmoid(zf[ATTN_W:, :] + bft_ref[...])
    lft_lo = lft - lft.astype(BF16).astype(F32)
    row = lax.broadcasted_iota(jnp.int32, (2 * HEADS, TM), 0)
    ct = _dot(jnp.where(row < HEADS, lft, lft_lo).astype(BF16), ut_ref[...])

    za_ref[0] = _silu_of_twice(zf[0:ATTN_W, :]).astype(BF16)
    ct = ct[0:HEADS, :] + ct[HEADS:2 * HEADS, :]
    carry = ct_ref[...]
    cq = ct[:, 0:TM] + jnp.concatenate([carry] * (TM // LANES), axis=1)
    cq_ref[0] = cq
    ct_ref[...] = carry + ct[:, TM:TM + LANES]
    kraw = trn(T_K, T_V)

    ck = jnp.concatenate([cq, jnp.zeros((LANES - HEADS, TM), F32)], axis=0).T
    ck = ck + pltpu.roll(ck, HEADS, 1) + pltpu.roll(ck, 2 * HEADS, 1)
    lane = lax.broadcasted_iota(jnp.int32, (TM, LANES), 1)
    r1 = ck - ck.astype(BF16).astype(F32)
    r2 = r1 - r1.astype(BF16).astype(F32)
    piece = jnp.where(lane < HEADS, ck, jnp.where(lane < 2 * HEADS, r1, r2))
    is_one = (lane >= ONES_LANE) & (lane < ONES_LANE + CUM_PARTS)
    kx_ref[0] = jnp.where(lane < ONES_LANE, -piece, jnp.where(is_one, 1.0, 0.0)).astype(BF16)

    def head_norm(raw):
        r3 = raw.reshape(HEADS, HEAD_DIM, TM)
        rn = r3 * lax.rsqrt(jnp.mean(r3 * r3, axis=1, keepdims=True) + EPS)
        return rn.reshape(ATTN_W, TM)

    qf = head_norm(qraw) * gqk_ref[...]
    qt_ref[0] = qf.astype(BF16)
    vraw = trn(T_V, T_ZA)

    kf = head_norm(kraw)
    kt = kf.astype(BF16)
    for pair in range(HEADS // 2):
        k_ref[0, pair] = kt[pair * LANES:(pair + 1) * LANES, :].T

    diag = (qf * kf).reshape(HEADS, HEAD_DIM, TM).sum(axis=1)
    lane8 = lax.broadcasted_iota(jnp.int32, (HEADS, LANES), 1)
    dmin = jnp.zeros((HEADS, LANES), F32)
    for t in range(TM // TQ):
        tile_min = jnp.min(diag[:, t * TQ:(t + 1) * TQ], axis=1, keepdims=True)
        dmin = jnp.where(lane8 == t, tile_min, dmin)
    dmin_ref[0, 0] = dmin

    vt = vraw.astype(BF16)
    for i in range(TM // TK):
        for hd in range(HEADS):
            vt_ref[0, i, hd, 0:HEAD_DIM, :] = (
                vt[hd * HEAD_DIM:(hd + 1) * HEAD_DIM, i * TK:(i + 1) * TK])
            vt_ref[0, i, hd, HEAD_DIM:V_ROWS, :] = jnp.ones((V_ROWS - HEAD_DIM, TK), BF16)

    h_next = modulated_norm(xnext_ref, adanext_ref)
    h_ref[1 - slot] = h_next
    gc_ref[...] = _dot(h_next, wn_ref[:, N_GC:N_U])


def _proj(x, ada3, ng, wn, wt, bft, gqk, cw, wb, ut):
    b, s, d = x.shape
    const = lambda shape: pl.BlockSpec(shape, lambda bi, si: (0,) * len(shape))
    out_shape = [
        jax.ShapeDtypeStruct((b, HEADS // 2, s, LANES), BF16),
        jax.ShapeDtypeStruct((b, s, LANES), BF16),
        jax.ShapeDtypeStruct((b, ATTN_W, s), BF16),
        jax.ShapeDtypeStruct((b, HEADS, s), F32),
        jax.ShapeDtypeStruct((b, s // TK, HEADS, V_ROWS, TK), BF16),
        jax.ShapeDtypeStruct((b, ATTN_W, s), BF16),
        jax.ShapeDtypeStruct((b, s, d), BF16),
        jax.ShapeDtypeStruct((b, s, d), BF16),
        jax.ShapeDtypeStruct((b, s // TM, HEADS, LANES), F32),
    ]
    out_specs = [
        pl.BlockSpec((1, HEADS // 2, TM, LANES), lambda bi, si: (bi, 0, si, 0)),
        pl.BlockSpec((1, TM, LANES), lambda bi, si: (bi, si, 0)),
        pl.BlockSpec((1, ATTN_W, TM), lambda bi, si: (bi, 0, si)),
        pl.BlockSpec((1, HEADS, TM), lambda bi, si: (bi, 0, si)),
        pl.BlockSpec((1, TM // TK, HEADS, V_ROWS, TK), lambda bi, si: (bi, si, 0, 0, 0)),
        pl.BlockSpec((1, ATTN_W, TM), lambda bi, si: (bi, 0, si)),
        pl.BlockSpec((1, TM, d), lambda bi, si: (bi, si, 0)),
        pl.BlockSpec((1, TM, d), lambda bi, si: (bi, si, 0)),
        pl.BlockSpec((1, 1, HEADS, LANES), lambda bi, si: (bi, si, 0, 0)),
    ]
    tiles = s // TM
    assert tiles % 2 == 0
    next_b = lambda bi, si: jnp.minimum(bi + (si + 1) // tiles, b - 1)
    next_s = lambda bi, si: jnp.where((bi == b - 1) & (si == tiles - 1), si, (si + 1) % tiles)
    in_specs = [
        pl.BlockSpec((1, TM, d), lambda bi, si: (0, 0, 0)),
        pl.BlockSpec((1, TM, d), lambda bi, si: (next_b(bi, si), next_s(bi, si), 0)),
        pl.BlockSpec((1, 1, 3 * d), lambda bi, si: (bi, 0, 0)),
        pl.BlockSpec((1, 1, 3 * d), lambda bi, si: (next_b(bi, si), 0, 0)),
        const(ng.shape), const(wn.shape), const(wt.shape), const(bft.shape), const(gqk.shape),
        const(cw.shape), const(wb.shape), const(ut.shape),
    ]
    return pl.pallas_call(
        _proj_kernel,
        grid=(b, s // TM),
        in_specs=in_specs,
        out_specs=out_specs,
        out_shape=out_shape,
        scratch_shapes=[pltpu.VMEM((HEADS, LANES), F32),
                        pltpu.VMEM((TM + 8, CONV_W), F32),
                        pltpu.VMEM((2, TM, D_MODEL), BF16),
                        pltpu.VMEM((TM, CONV_W), F32)],
        compiler_params=pltpu.CompilerParams(
            dimension_semantics=("arbitrary", "arbitrary"),
            vmem_limit_bytes=VMEM_LIMIT),
        name="proj",
    )(x, x, ada3, ada3, ng, wn, wt, bft, gqk, cw, wb, ut)


def _attn_kernel(cend_ref, dmin_ref, bound_ref, qt_ref, cq_ref, k_ref, kx_ref, vt_ref, za_ref, sga_ref, mb_ref,
                 x_ref, gate_ref, wa_ref, wo_ref, o_ref, rhs_ref, m_ref, acc_ref, *, running_max):
    bi = pl.program_id(0)
    qi = pl.program_id(1)
    blocks_per_tile = TQ // TK
    n_blocks = kx_ref.shape[1] // TK
    i32 = jnp.int32

    last_before = qi * blocks_per_tile - 1
    n_tiles = n_blocks // blocks_per_tile
    slack = (LOGIT_BOUND_MARGIN + DIAG_LOGIT_MARGIN) * bound_ref[0] + SKIP_LOG_WEIGHT

    first = []
    for h in range(HEADS):
        base = (bi * HEADS + h) * n_blocks
        threshold = (cend_ref[base + jnp.maximum(last_before, 0)] + slack
                     - dmin_ref[(bi * HEADS + h) * n_tiles + qi])
        skipped = i32(0)
        for j in range(n_blocks - blocks_per_tile - 1):
            skipped = skipped + (cend_ref[base + j] >= threshold).astype(i32)
        first.append(skipped)
    upto, shift, total = [], [], i32(0)
    for h in range(HEADS):
        shift.append(first[h] - total)
        total = total + (qi * blocks_per_tile - first[h])
        upto.append(total)

    row = lax.broadcasted_iota(jnp.int32, (LANES, TQ), 0)
    for h in range(HEADS):
        pair, half = divmod(h, 2)
        qp = qt_ref[0, pair * LANES:(pair + 1) * LANES, :]
        mine = (row >= half * HEAD_DIM) & (row < (half + 1) * HEAD_DIM)
        rhs_ref[h, 0:LANES, :] = qp * jnp.where(mine, 1.0, 0.0).astype(BF16)
        cq = cq_ref[0, h:h + 1, :]
        r1 = cq - cq.astype(BF16).astype(F32)
        r2 = r1 - r1.astype(BF16).astype(F32)
        sel = (row == h) | (row == HEADS + h) | (row == 2 * HEADS + h)
        f = jnp.where(row == ONES_LANE, cq,
                      jnp.where(row == ONES_LANE + 1, r1,
                                jnp.where(row == ONES_LANE + 2, r2, 0.0)))
        rhs_ref[h, LANES:2 * LANES, :] = jnp.where(sel, 1.0, f).astype(BF16)
    rhs_ref[HEADS] = jnp.zeros(rhs_ref.shape[1:], BF16)

    if running_max:
        m_ref[...] = jnp.full(m_ref.shape, NEG_BIG, F32)
    acc_ref[...] = jnp.zeros_like(acc_ref)

    def scores(j, h, q_lo):
        start = pl.multiple_of(j * TK, TK)
        pair = h // 2 if isinstance(h, int) else jnp.minimum(h, HEADS - 1) // 2
        lhs = jnp.concatenate([k_ref[0, pair, pl.ds(start, TK), :],
                               kx_ref[0, pl.ds(start, TK), :]], axis=1)
        return _dot(lhs, rhs_ref[h, :, q_lo:TQ])

    def run_units(units):
        ahead = [scores(j, h, q_lo) for j, h, _, q_lo in units[:SCORE_LOOKAHEAD]]
        for i, (j, h, masked, q_lo) in enumerate(units):
            sc = ahead.pop(0)
            if i + SCORE_LOOKAHEAD < len(units):
                jn, hn, _, qn = units[i + SCORE_LOOKAHEAD]
                ahead.append(scores(jn, hn, qn))
            if masked:
                kpos = lax.broadcasted_iota(jnp.int32, sc.shape, 0)
                qpos = lax.broadcasted_iota(jnp.int32, sc.shape, 1)
                sc = jnp.where(kpos <= qpos, sc, NEG_BIG)
            hv = h if isinstance(h, int) else jnp.minimum(h, HEADS - 1)
            if running_max:
                m_prev = m_ref[h, :, q_lo:TQ]
                m_new = jnp.maximum(m_prev, jnp.max(sc, axis=0, keepdims=True))
                m_ref[h, :, q_lo:TQ] = m_new
                p = jnp.exp(sc - m_new).astype(BF16)
                acc_ref[h, :, q_lo:TQ] = (jnp.exp(m_prev - m_new) * acc_ref[h, :, q_lo:TQ]
                                          + _dot(vt_ref[0, j, hv], p))
            else:
                p = jnp.exp(sc).astype(BF16)
                acc_ref[h, :, q_lo:TQ] += _dot(vt_ref[0, j, hv], p)

    def list_units(start, count):
        units = []
        for t in range(count):
            u = start + t
            h = i32(0)
            off = shift[0]
            for g in range(HEADS):
                past = u >= upto[g]
                h = h + past.astype(i32)
                if g + 1 < HEADS:
                    off = jnp.where(past, shift[g + 1], off)
            j = jnp.where(u >= total, 0, u + off)
            units.append((j, h, False, 0))
        return units

    def long_body(i, carry):
        run_units(list_units(i * UNITS_PER_LONG_ITER, UNITS_PER_LONG_ITER))
        return carry

    n_short = (total + UNITS_PER_ITER - 1) // UNITS_PER_ITER
    n_long = n_short // (UNITS_PER_LONG_ITER // UNITS_PER_ITER)
    lax.fori_loop(0, n_long, long_body, 0)
    done = n_long * UNITS_PER_LONG_ITER

    def short_body(i, carry):
        run_units(list_units(done + i * UNITS_PER_ITER, UNITS_PER_ITER))
        return carry

    lax.fori_loop(0, n_short - n_long * (UNITS_PER_LONG_ITER // UNITS_PER_ITER), short_body, 0)
    run_units([(qi * blocks_per_tile + t, h, True, t * TK)
               for t in range(blocks_per_tile) for h in range(HEADS)])

    acc = jnp.concatenate(
        [acc_ref[h, 0:HEAD_DIM, :] * (1.0 / acc_ref[h, HEAD_DIM:HEAD_DIM + 1, :])
         for h in range(HEADS)], axis=0)
    at = (acc * za_ref[0].astype(F32)).astype(BF16)
    ya = _dot_tn(at, wa_ref[...])
    merged = (sga_ref[0].astype(F32) * ya + mb_ref[0].astype(F32)).astype(BF16)
    o_ref[0] = x_ref[0] + gate_ref[0] * _dot(merged, wo_ref[...])


def _attn(running_max, cend, dmin, bound, qt, cq, k, kx, vt, za, sga, mb, x, ada3, wa, wo):
    b, s, d = x.shape
    const = lambda shape: pl.BlockSpec(shape, lambda bi, qi: (0,) * len(shape))
    in_specs = [
        pl.BlockSpec(memory_space=pltpu.SMEM),
        pl.BlockSpec(memory_space=pltpu.SMEM),
        pl.BlockSpec(memory_space=pltpu.SMEM),
        pl.BlockSpec((1, ATTN_W, TQ), lambda bi, qi: (bi, 0, qi)),
        pl.BlockSpec((1, HEADS, TQ), lambda bi, qi: (bi, 0, qi)),
        pl.BlockSpec((1, HEADS // 2, s, LANES), lambda bi, qi: (bi, 0, 0, 0)),
        pl.BlockSpec((1, s, LANES), lambda bi, qi: (bi, 0, 0)),
        pl.BlockSpec((1, s // TK, HEADS, V_ROWS, TK), lambda bi, qi: (bi, 0, 0, 0, 0)),
        pl.BlockSpec((1, ATTN_W, TQ), lambda bi, qi: (bi, 0, qi)),
        pl.BlockSpec((1, TQ, d), lambda bi, qi: (bi, qi, 0)),
        pl.BlockSpec((1, TQ, d), lambda bi, qi: (bi, qi, 0)),
        pl.BlockSpec((1, TQ, d), lambda bi, qi: (bi, qi, 0)),
        pl.BlockSpec((1, 1, d), lambda bi, qi: (bi, 0, 2)),
        const(wa.shape), const(wo.shape),
    ]
    return pl.pallas_call(
        functools.partial(_attn_kernel, running_max=running_max),
        grid=(b, s // TQ),
        in_specs=in_specs,
        out_specs=pl.BlockSpec((1, TQ, d), lambda bi, qi: (bi, qi, 0)),
        out_shape=jax.ShapeDtypeStruct((b, s, d), F32),
        scratch_shapes=[pltpu.VMEM((HEADS + 1, 2 * LANES, TQ), BF16),
                        pltpu.VMEM((HEADS + 1, 1, TQ), F32),
                        pltpu.VMEM((HEADS + 1, V_ROWS, TQ), F32)],
        compiler_params=pltpu.CompilerParams(
            dimension_semantics=("arbitrary", "arbitrary"),
            vmem_limit_bytes=VMEM_LIMIT),
        name="attn_running_max" if running_max else "attn",
    )(cend, dmin, bound, qt, cq, k, kx, vt, za, sga, mb, x, ada3, wa, wo)


def _layer(x, c, w_ada, b_ada, norm_g, w_in, b_f, q_norm_g, k_norm_g, conv_w,
           w_attn_out, w_conv_out, w_o):
    b, s, d = x.shape
    ada3 = _ada(c, w_ada, b_ada)

    wn, wt, wa, wb, wo = _prep(w_in.T, w_attn_out, w_conv_out, w_o)

    bft = jnp.broadcast_to(jnp.concatenate([b_f, b_f])[:, None], (2 * HEADS, TM))
    gqk = jnp.tile(q_norm_g * k_norm_g * HEAD_DIM ** -0.5, HEADS).reshape(ATTN_W, 1)

    tok = np.arange(TM)
    upper = (tok[:, None] <= tok[None, :]).astype(np.float32)
    ut = jnp.asarray(np.concatenate([upper, np.ones((TM, LANES), np.float32)], axis=1), BF16)

    k, kx, qt, cq, vt, za, sga, mb, dmin4 = _proj(
        x, ada3, norm_g.reshape(1, d), wn, wt, bft, gqk, conv_w, wb, ut)
    logit_bound = HEAD_DIM ** 0.5 * jnp.max(jnp.abs(q_norm_g)) * jnp.max(jnp.abs(k_norm_g))
    cend = cq[:, :, TK - 1::TK].reshape(-1)
    dmin = jnp.transpose(dmin4[..., :TM // TQ], (0, 2, 1, 3)).reshape(-1)
    args = (cend, dmin, logit_bound.reshape(1), qt, cq, k, kx, vt, za, sga, mb, x, ada3, wa, wo)
    return lax.cond(logit_bound <= MAX_RAW_LOGIT,
                    functools.partial(_attn, False), functools.partial(_attn, True), *args)


@jax.jit
def kernel(x, c, w_ada, b_ada, norm_g, w_in, b_f, q_norm_g, k_norm_g, conv_w,
           w_attn_out, w_conv_out, w_o):
    for i in range(w_ada.shape[0]):
        x = _layer(x, c, w_ada[i], b_ada[i], norm_g[i], w_in[i], b_f[i], q_norm_g[i],
                   k_norm_g[i], conv_w[i], w_attn_out[i], w_conv_out[i], w_o[i])
    return x
```

```python
import functools

import jax
import jax.numpy as jnp
import numpy as np
from jax import lax
from jax.experimental import pallas as pl
from jax.experimental.pallas import tpu as pltpu

D_MODEL = 1024
HEADS = 8
HEAD_DIM = 64
ATTN_W = HEADS * HEAD_DIM
CONV_W = 512
EPS = 1e-6

LANES = 128
TM = 1024
TQ = 512
TK = 256
CUM_PARTS = 3
ONES_LANE = CUM_PARTS * HEADS
V_ROWS = HEAD_DIM + 16
SCORE_LOOKAHEAD = 4
UNITS_PER_LONG_ITER = 16
UNITS_PER_ITER = 8
NEG_BIG = -1e30
MAX_RAW_LOGIT = 40.0
SKIP_LOG_WEIGHT = 30.0
LOGIT_BOUND_MARGIN = 1.05
DIAG_LOGIT_MARGIN = 0.05
VMEM_LIMIT = 62 * 1024 * 1024

F32 = jnp.float32
BF16 = jnp.bfloat16


def _log_sigmoid(x):
    return jnp.minimum(x, 0.0) - jnp.log(1.0 + jnp.exp(-jnp.abs(x)))


def _sigmoid_of_twice(hx):
    return 0.5 * jnp.tanh(hx) + 0.5


def _silu_of_twice(hx):
    return hx * jnp.tanh(hx) + hx


def _dot(a, b):
    return jnp.dot(a, b, preferred_element_type=F32)


def _dot_nt(a, b):
    return lax.dot_general(a, b, (((1,), (1,)), ((), ())), preferred_element_type=F32)


def _dot_tn(a, b):
    return lax.dot_general(a, b, (((0,), (0,)), ((), ())), preferred_element_type=F32)


def _ada_kernel(c_ref, w_ref, b_ref, o_ref):
    o_ref[:, 0, :] = _dot(c_ref[...].astype(BF16), w_ref[...].astype(BF16)) + b_ref[...]


def _ada(c, w_ada, b_ada):
    b, d = c.shape
    n = w_ada.shape[1]
    return pl.pallas_call(
        _ada_kernel,
        grid=(n // d,),
        in_specs=[pl.BlockSpec((b, d), lambda j: (0, 0)),
                  pl.BlockSpec((d, d), lambda j: (0, j)),
                  pl.BlockSpec((1, d), lambda j: (0, j))],
        out_specs=pl.BlockSpec((b, 1, d), lambda j: (0, 0, j)),
        out_shape=jax.ShapeDtypeStruct((b, 1, n), F32),
        name="ada",
    )(c, w_ada, b_ada.reshape(1, n))


N_GB = 0
N_GC = N_GB + CONV_W
N_U = N_GC + CONV_W
N_ZB = N_U + CONV_W
N_GA = N_ZB + CONV_W
N_GB2 = N_GA + D_MODEL
N_END = N_GB2 + D_MODEL
T_Q = 0
T_K = T_Q + ATTN_W
T_V = T_K + ATTN_W
T_ZA = T_V + ATTN_W
T_F = T_ZA + ATTN_W
T_END = T_F + 2 * HEADS

IN_SIZES = (ATTN_W, ATTN_W, ATTN_W, HEADS, ATTN_W, CONV_W, CONV_W, CONV_W, CONV_W, D_MODEL, D_MODEL)
IN_OFFS = tuple(sum(IN_SIZES[:i]) for i in range(len(IN_SIZES) + 1))
PREP_COLS = 256


def _prep_kernel(w_ref, wa_ref, wb_ref, wo_ref, wn_ref, wt_ref, wa16_ref, wb16_ref, wo16_ref):
    rows = lambda i, j: w_ref[IN_OFFS[i]:IN_OFFS[j], :]
    f = rows(3, 4)
    wn_ref[...] = jnp.concatenate([rows(5, 8), 0.5 * rows(8, 11)], axis=0).astype(BF16).T
    wt_ref[T_Q:T_ZA, :] = rows(0, 3).astype(BF16)
    wt_ref[T_ZA:T_F, :] = (0.5 * rows(4, 5)).astype(BF16)
    wt_ref[T_F:T_END, :] = jnp.concatenate([f, f], axis=0).astype(BF16)
    wa16_ref[...] = wa_ref[...].astype(BF16)
    wb16_ref[...] = wb_ref[...].astype(BF16)
    wo16_ref[...] = wo_ref[...].astype(BF16)


def _prep(w_t, w_attn_out, w_conv_out, w_o):
    n, d = w_t.shape
    steps = d // PREP_COLS
    row_block = lambda w: pl.BlockSpec((w.shape[0] // steps, w.shape[1]), lambda i: (i, 0))
    sides = (w_attn_out, w_conv_out, w_o)
    return pl.pallas_call(
        _prep_kernel,
        grid=(steps,),
        in_specs=[pl.BlockSpec((n, PREP_COLS), lambda i: (0, i))] + [row_block(w) for w in sides],
        out_specs=[pl.BlockSpec((PREP_COLS, N_END), lambda i: (i, 0)),
                   pl.BlockSpec((T_END, PREP_COLS), lambda i: (0, i))] + [row_block(w) for w in sides],
        out_shape=[jax.ShapeDtypeStruct((d, N_END), BF16),
                   jax.ShapeDtypeStruct((T_END, d), BF16)]
                  + [jax.ShapeDtypeStruct(w.shape, BF16) for w in sides],
        compiler_params=pltpu.CompilerParams(vmem_limit_bytes=VMEM_LIMIT),
        name="prep",
    )(w_t, *sides)


def _proj_kernel(xfirst_ref, xnext_ref, ada_ref, adanext_ref, ng_ref, wn_ref, wt_ref, bft_ref, gqk_ref,
                 cw_ref, wb_ref, ut_ref,
                 k_ref, kx_ref, qt_ref, cq_ref, vt_ref, za_ref, sga_ref, mb_ref, dmin_ref,
                 ct_ref, cu_ref, h_ref, gc_ref):
    s = pl.program_id(1)
    slot = lax.rem(s, 2)

    def modulated_norm(xr, ar):
        x = xr[0]
        shift = ar[0, :, 0:D_MODEL]
        scale = ar[0, :, D_MODEL:2 * D_MODEL]
        xn = x * lax.rsqrt(jnp.mean(x * x, axis=-1, keepdims=True) + EPS)
        return (xn * (ng_ref[...] * (1.0 + scale)) + shift).astype(BF16)

    @pl.when(s == 0)
    def _():
        ct_ref[...] = jnp.zeros_like(ct_ref)
        cu_ref[0:8, :] = jnp.zeros((8, CONV_W), F32)

    @pl.when((pl.program_id(0) == 0) & (s == 0))
    def _():
        h0 = modulated_norm(xfirst_ref, ada_ref)
        h_ref[0] = h0
        gc_ref[...] = _dot(h0, wn_ref[:, N_GC:N_U])

    nat = lambda lo, hi: _dot(h_ref[slot], wn_ref[:, lo:hi])
    trn = lambda lo, hi: _dot_nt(wt_ref[lo:hi, :], h_ref[slot])

    def head_norm(raw):
        r3 = raw.reshape(HEADS, HEAD_DIM, TM)
        rn = r3 * lax.rsqrt(jnp.mean(r3 * r3, axis=1, keepdims=True) + EPS)
        return rn.reshape(ATTN_W, TM)

    gc = gc_ref[...]
    u = nat(N_U, N_ZB)
    qraw = trn(T_Q, T_K)
    kraw = trn(T_K, T_V)

    cu = gc * u
    cu_ref[8:8 + TM, :] = cu
    conv = (cw_ref[2:3, :] * cu + cw_ref[1:2, :] * cu_ref[7:7 + TM, :]
            + cw_ref[0:1, :] * cu_ref[6:6 + TM, :])
    cu_ref[0:8, :] = cu[TM - 8:TM, :]
    gb = nat(N_GB, N_GC)
    zb = nat(N_ZB, N_GA)

    qf = head_norm(qraw) * gqk_ref[...]
    qt_ref[0] = qf.astype(BF16)
    zf = trn(T_ZA, T_END)

    kf = head_norm(kraw)
    kt = kf.astype(BF16)
    for pair in range(HEADS // 2):
        k_ref[0, pair] = kt[pair * LANES:(pair + 1) * LANES, :].T

    diag = (qf * kf).reshape(HEADS, HEAD_DIM, TM).sum(axis=1)
    lane8 = lax.broadcasted_iota(jnp.int32, (HEADS, LANES), 1)
    dmin = jnp.zeros((HEADS, LANES), F32)
    for t in range(TM // TQ):
        tile_min = jnp.min(diag[:, t * TQ:(t + 1) * TQ], axis=1, keepdims=True)
        dmin = jnp.where(lane8 == t, tile_min, dmin)
    dmin_ref[0, 0] = dmin
    ga = nat(N_GA, N_GB2)

    lft = _log_sigmoid(zf[ATTN_W:, :] + bft_ref[...])
    lft_lo = lft - lft.astype(BF16).astype(F32)
    row = lax.broadcasted_iota(jnp.int32, (2 * HEADS, TM), 0)
    ct = _dot(jnp.where(row < HEADS, lft, lft_lo).astype(BF16), ut_ref[...])
    gb2 = nat(N_GB2, N_END)

    za_ref[0] = _silu_of_twice(zf[0:ATTN_W, :]).astype(BF16)
    ob = (gb * conv * _silu_of_twice(zb)).astype(BF16)
    mbr = _dot(ob, wb_ref[...])
    vraw = trn(T_V, T_ZA)

    ct = ct[0:HEADS, :] + ct[HEADS:2 * HEADS, :]
    carry = ct_ref[...]
    cq = ct[:, 0:TM] + jnp.concatenate([carry] * (TM // LANES), axis=1)
    cq_ref[0] = cq
    ct_ref[...] = carry + ct[:, TM:TM + LANES]

    ck = jnp.concatenate([cq, jnp.zeros((LANES - HEADS, TM), F32)], axis=0).T
    ck = ck + pltpu.roll(ck, HEADS, 1) + pltpu.roll(ck, 2 * HEADS, 1)
    lane = lax.broadcasted_iota(jnp.int32, (TM, LANES), 1)
    r1 = ck - ck.astype(BF16).astype(F32)
    r2 = r1 - r1.astype(BF16).astype(F32)
    piece = jnp.where(lane < HEADS, ck, jnp.where(lane < 2 * HEADS, r1, r2))
    is_one = (lane >= ONES_LANE) & (lane < ONES_LANE + CUM_PARTS)
    kx_ref[0] = jnp.where(lane < ONES_LANE, -piece, jnp.where(is_one, 1.0, 0.0)).astype(BF16)

    sga_ref[0] = _sigmoid_of_twice(ga).astype(BF16)
    mb_ref[0] = (_sigmoid_of_twice(gb2) * mbr).astype(BF16)

    vt = vraw.astype(BF16)
    for i in range(TM // TK):
        for hd in range(HEADS):
            vt_ref[0, i, hd, 0:HEAD_DIM, :] = (
                vt[hd * HEAD_DIM:(hd + 1) * HEAD_DIM, i * TK:(i + 1) * TK])
            vt_ref[0, i, hd, HEAD_DIM:V_ROWS, :] = jnp.ones((V_ROWS - HEAD_DIM, TK), BF16)

    h_next = modulated_norm(xnext_ref, adanext_ref)
    h_ref[1 - slot] = h_next
    gc_ref[...] = _dot(h_next, wn_ref[:, N_GC:N_U])


def _proj(x, ada3, ng, wn, wt, bft, gqk, cw, wb, ut):
    b, s, d = x.shape
    const = lambda shape: pl.BlockSpec(shape, lambda bi, si: (0,) * len(shape))
    out_shape = [
        jax.ShapeDtypeStruct((b, HEADS // 2, s, LANES), BF16),
        jax.ShapeDtypeStruct((b, s, LANES), BF16),
        jax.ShapeDtypeStruct((b, ATTN_W, s), BF16),
        jax.ShapeDtypeStruct((b, HEADS, s), F32),
        jax.ShapeDtypeStruct((b, s // TK, HEADS, V_ROWS, TK), BF16),
        jax.ShapeDtypeStruct((b, ATTN_W, s), BF16),
        jax.ShapeDtypeStruct((b, s, d), BF16),
        jax.ShapeDtypeStruct((b, s, d), BF16),
        jax.ShapeDtypeStruct((b, s // TM, HEADS, LANES), F32),
    ]
    out_specs = [
        pl.BlockSpec((1, HEADS // 2, TM, LANES), lambda bi, si: (bi, 0, si, 0)),
        pl.BlockSpec((1, TM, LANES), lambda bi, si: (bi, si, 0)),
        pl.BlockSpec((1, ATTN_W, TM), lambda bi, si: (bi, 0, si)),
        pl.BlockSpec((1, HEADS, TM), lambda bi, si: (bi, 0, si)),
        pl.BlockSpec((1, TM // TK, HEADS, V_ROWS, TK), lambda bi, si: (bi, si, 0, 0, 0)),
        pl.BlockSpec((1, ATTN_W, TM), lambda bi, si: (bi, 0, si)),
        pl.BlockSpec((1, TM, d), lambda bi, si: (bi, si, 0)),
        pl.BlockSpec((1, TM, d), lambda bi, si: (bi, si, 0)),
        pl.BlockSpec((1, 1, HEADS, LANES), lambda bi, si: (bi, si, 0, 0)),
    ]
    tiles = s // TM
    assert tiles % 2 == 0
    next_b = lambda bi, si: jnp.minimum(bi + (si + 1) // tiles, b - 1)
    next_s = lambda bi, si: jnp.where((bi == b - 1) & (si == tiles - 1), si, (si + 1) % tiles)
    in_specs = [
        pl.BlockSpec((1, TM, d), lambda bi, si: (0, 0, 0)),
        pl.BlockSpec((1, TM, d), lambda bi, si: (next_b(bi, si), next_s(bi, si), 0)),
        pl.BlockSpec((1, 1, 3 * d), lambda bi, si: (bi, 0, 0)),
        pl.BlockSpec((1, 1, 3 * d), lambda bi, si: (next_b(bi, si), 0, 0)),
        const(ng.shape), const(wn.shape), const(wt.shape), const(bft.shape), const(gqk.shape),
        const(cw.shape), const(wb.shape), const(ut.shape),
    ]
    return pl.pallas_call(
        _proj_kernel,
        grid=(b, s // TM),
        in_specs=in_specs,
        out_specs=out_specs,
        out_shape=out_shape,
        scratch_shapes=[pltpu.VMEM((HEADS, LANES), F32),
                        pltpu.VMEM((TM + 8, CONV_W), F32),
                        pltpu.VMEM((2, TM, D_MODEL), BF16),
                        pltpu.VMEM((TM, CONV_W), F32)],
        compiler_params=pltpu.CompilerParams(
            dimension_semantics=("arbitrary", "arbitrary"),
            vmem_limit_bytes=VMEM_LIMIT),
        name="proj",
    )(x, x, ada3, ada3, ng, wn, wt, bft, gqk, cw, wb, ut)


def _attn_kernel(cend_ref, dmin_ref, bound_ref, qt_ref, cq_ref, k_ref, kx_ref, vt_ref, za_ref, sga_ref, mb_ref,
                 x_ref, gate_ref, wa_ref, wo_ref, o_ref, rhs_ref, m_ref, acc_ref, *, running_max):
    bi = pl.program_id(0)
    qi = pl.program_id(1)
    blocks_per_tile = TQ // TK
    n_blocks = kx_ref.shape[1] // TK
    i32 = jnp.int32

    last_before = qi * blocks_per_tile - 1
    n_tiles = n_blocks // blocks_per_tile
    slack = (LOGIT_BOUND_MARGIN + DIAG_LOGIT_MARGIN) * bound_ref[0] + SKIP_LOG_WEIGHT

    first = []
    for h in range(HEADS):
        base = (bi * HEADS + h) * n_blocks
        threshold = (cend_ref[base + jnp.maximum(last_before, 0)] + slack
                     - dmin_ref[(bi * HEADS + h) * n_tiles + qi])
        skipped = i32(0)
        for j in range(n_blocks - blocks_per_tile - 1):
            skipped = skipped + (cend_ref[base + j] >= threshold).astype(i32)
        first.append(skipped)
    upto, shift, total = [], [], i32(0)
    for h in range(HEADS):
        shift.append(first[h] - total)
        total = total + (qi * blocks_per_tile - first[h])
        upto.append(total)

    row = lax.broadcasted_iota(jnp.int32, (LANES, TQ), 0)
    for h in range(HEADS):
        pair, half = divmod(h, 2)
        qp = qt_ref[0, pair * LANES:(pair + 1) * LANES, :]
        mine = (row >= half * HEAD_DIM) & (row < (half + 1) * HEAD_DIM)
        rhs_ref[h, 0:LANES, :] = qp * jnp.where(mine, 1.0, 0.0).astype(BF16)
        cq = cq_ref[0, h:h + 1, :]
        r1 = cq - cq.astype(BF16).astype(F32)
        r2 = r1 - r1.astype(BF16).astype(F32)
        sel = (row == h) | (row == HEADS + h) | (row == 2 * HEADS + h)
        f = jnp.where(row == ONES_LANE, cq,
                      jnp.where(row == ONES_LANE + 1, r1,
                                jnp.where(row == ONES_LANE + 2, r2, 0.0)))
        rhs_ref[h, LANES:2 * LANES, :] = jnp.where(sel, 1.0, f).astype(BF16)
    rhs_ref[HEADS] = jnp.zeros(rhs_ref.shape[1:], BF16)

    if running_max:
        m_ref[...] = jnp.full(m_ref.shape, NEG_BIG, F32)
    acc_ref[...] = jnp.zeros_like(acc_ref)

    def scores(j, h, q_lo):
        start = pl.multiple_of(j * TK, TK)
        pair = h // 2 if isinstance(h, int) else jnp.minimum(h, HEADS - 1) // 2
        lhs = jnp.concatenate([k_ref[0, pair, pl.ds(start, TK), :],
                               kx_ref[0, pl.ds(start, TK), :]], axis=1)
        return _dot(lhs, rhs_ref[h, :, q_lo:TQ])

    def run_units(units):
        ahead = [scores(j, h, q_lo) for j, h, _, q_lo in units[:SCORE_LOOKAHEAD]]
        for i, (j, h, masked, q_lo) in enumerate(units):
            sc = ahead.pop(0)
            if i + SCORE_LOOKAHEAD < len(units):
                jn, hn, _, qn = units[i + SCORE_LOOKAHEAD]
                ahead.append(scores(jn, hn, qn))
            if masked:
                kpos = lax.broadcasted_iota(jnp.int32, sc.shape, 0)
                qpos = lax.broadcasted_iota(jnp.int32, sc.shape, 1)
                sc = jnp.where(kpos <= qpos, sc, NEG_BIG)
            hv = h if isinstance(h, int) else jnp.minimum(h, HEADS - 1)
            if running_max:
                m_prev = m_ref[h, :, q_lo:TQ]
                m_new = jnp.maximum(m_prev, jnp.max(sc, axis=0, keepdims=True))
                m_ref[h, :, q_lo:TQ] = m_new
                p = jnp.exp(sc - m_new).astype(BF16)
                acc_ref[h, :, q_lo:TQ] = (jnp.exp(m_prev - m_new) * acc_ref[h, :, q_lo:TQ]
                                          + _dot(vt_ref[0, j, hv], p))
            else:
                p = jnp.exp(sc).astype(BF16)
                acc_ref[h, :, q_lo:TQ] += _dot(vt_ref[0, j, hv], p)

    def list_units(start, count):
        units = []
        for t in range(count):
            u = start + t
            h = i32(0)
            off = shift[0]
            for g in range(HEADS):
                past = u >= upto[g]
                h = h + past.astype(i32)
                if g + 1 < HEADS:
                    off = jnp.where(past, shift[g + 1], off)
            j = jnp.where(u >= total, 0, u + off)
            units.append((j, h, False, 0))
        return units

    def long_body(i, carry):
        run_units(list_units(i * UNITS_PER_LONG_ITER, UNITS_PER_LONG_ITER))
        return carry

    n_short = (total + UNITS_PER_ITER - 1) // UNITS_PER_ITER
    n_long = n_short // (UNITS_PER_LONG_ITER // UNITS_PER_ITER)
    lax.fori_loop(0, n_long, long_body, 0)
    done = n_long * UNITS_PER_LONG_ITER

    def short_body(i, carry):
        run_units(list_units(done + i * UNITS_PER_ITER, UNITS_PER_ITER))
        return carry

    lax.fori_loop(0, n_short - n_long * (UNITS_PER_LONG_ITER // UNITS_PER_ITER), short_body, 0)
    run_units([(qi * blocks_per_tile + t, h, True, t * TK)
               for t in range(blocks_per_tile) for h in range(HEADS)])

    acc = jnp.concatenate(
        [acc_ref[h, 0:HEAD_DIM, :] * (1.0 / acc_ref[h, HEAD_DIM:HEAD_DIM + 1, :])
         for h in range(HEADS)], axis=0)
    at = (acc * za_ref[0].astype(F32)).astype(BF16)
    ya = _dot_tn(at, wa_ref[...])
    merged = (sga_ref[0].astype(F32) * ya + mb_ref[0].astype(F32)).astype(BF16)
    o_ref[0] = x_ref[0] + gate_ref[0] * _dot(merged, wo_ref[...])


def _attn(running_max, cend, dmin, bound, qt, cq, k, kx, vt, za, sga, mb, x, ada3, wa, wo):
    b, s, d = x.shape
    const = lambda shape: pl.BlockSpec(shape, lambda bi, qi: (0,) * len(shape))
    in_specs = [
        pl.BlockSpec(memory_space=pltpu.SMEM),
        pl.BlockSpec(memory_space=pltpu.SMEM),
        pl.BlockSpec(memory_space=pltpu.SMEM),
        pl.BlockSpec((1, ATTN_W, TQ), lambda bi, qi: (bi, 0, qi)),
        pl.BlockSpec((1, HEADS, TQ), lambda bi, qi: (bi, 0, qi)),
        pl.BlockSpec((1, HEADS // 2, s, LANES), lambda bi, qi: (bi, 0, 0, 0)),
        pl.BlockSpec((1, s, LANES), lambda bi, qi: (bi, 0, 0)),
        pl.BlockSpec((1, s // TK, HEADS, V_ROWS, TK), lambda bi, qi: (bi, 0, 0, 0, 0)),
        pl.BlockSpec((1, ATTN_W, TQ), lambda bi, qi: (bi, 0, qi)),
        pl.BlockSpec((1, TQ, d), lambda bi, qi: (bi, qi, 0)),
        pl.BlockSpec((1, TQ, d), lambda bi, qi: (bi, qi, 0)),
        pl.BlockSpec((1, TQ, d), lambda bi, qi: (bi, qi, 0)),
        pl.BlockSpec((1, 1, d), lambda bi, qi: (bi, 0, 2)),
        const(wa.shape), const(wo.shape),
    ]
    return pl.pallas_call(
        functools.partial(_attn_kernel, running_max=running_max),
        grid=(b, s // TQ),
        in_specs=in_specs,
        out_specs=pl.BlockSpec((1, TQ, d), lambda bi, qi: (bi, qi, 0)),
        out_shape=jax.ShapeDtypeStruct((b, s, d), F32),
        scratch_shapes=[pltpu.VMEM((HEADS + 1, 2 * LANES, TQ), BF16),
                        pltpu.VMEM((HEADS + 1, 1, TQ), F32),
                        pltpu.VMEM((HEADS + 1, V_ROWS, TQ), F32)],
        compiler_params=pltpu.CompilerParams(
            dimension_semantics=("arbitrary", "arbitrary"),
            vmem_limit_bytes=VMEM_LIMIT),
        name="attn_running_max" if running_max else "attn",
    )(cend, dmin, bound, qt, cq, k, kx, vt, za, sga, mb, x, ada3, wa, wo)


def _layer(x, c, w_ada, b_ada, norm_g, w_in, b_f, q_norm_g, k_norm_g, conv_w,
           w_attn_out, w_conv_out, w_o):
    b, s, d = x.shape
    ada3 = _ada(c, w_ada, b_ada)

    wn, wt, wa, wb, wo = _prep(w_in.T, w_attn_out, w_conv_out, w_o)

    bft = jnp.broadcast_to(jnp.concatenate([b_f, b_f])[:, None], (2 * HEADS, TM))
    gqk = jnp.tile(q_norm_g * k_norm_g * HEAD_DIM ** -0.5, HEADS).reshape(ATTN_W, 1)

    tok = np.arange(TM)
    upper = (tok[:, None] <= tok[None, :]).astype(np.float32)
    ut = jnp.asarray(np.concatenate([upper, np.ones((TM, LANES), np.float32)], axis=1), BF16)

    k, kx, qt, cq, vt, za, sga, mb, dmin4 = _proj(
        x, ada3, norm_g.reshape(1, d), wn, wt, bft, gqk, conv_w, wb, ut)
    logit_bound = HEAD_DIM ** 0.5 * jnp.max(jnp.abs(q_norm_g)) * jnp.max(jnp.abs(k_norm_g))
    cend = cq[:, :, TK - 1::TK].reshape(-1)
    dmin = jnp.transpose(dmin4[..., :TM // TQ], (0, 2, 1, 3)).reshape(-1)
    args = (cend, dmin, logit_bound.reshape(1), qt, cq, k, kx, vt, za, sga, mb, x, ada3, wa, wo)
    return lax.cond(logit_bound <= MAX_RAW_LOGIT,
                    functools.partial(_attn, False), functools.partial(_attn, True), *args)


@jax.jit
def kernel(x, c, w_ada, b_ada, norm_g, w_in, b_f, q_norm_g, k_norm_g, conv_w,
           w_attn_out, w_conv_out, w_o):
    for i in range(w_ada.shape[0]):
        x = _layer(x, c, w_ada[i], b_ada[i], norm_g[i], w_in[i], b_f[i], q_norm_g[i],
                   k_norm_g[i], conv_w[i], w_attn_out[i], w_conv_out[i], w_o[i])
    return x
```

```python
import functools

import jax
import jax.numpy as jnp
import numpy as np
from jax import lax
from jax.experimental import pallas as pl
from jax.experimental.pallas import tpu as pltpu

D_MODEL = 1024
HEADS = 8
HEAD_DIM = 64
ATTN_W = HEADS * HEAD_DIM
CONV_W = 512
EPS = 1e-6

LANES = 128
TM = 1024
TQ = 512
TK = 256
CUM_PARTS = 3
ONES_LANE = CUM_PARTS * HEADS
V_ROWS = HEAD_DIM + 16
SCORE_LOOKAHEAD = 4
UNITS_PER_LONG_ITER = 16
UNITS_PER_ITER = 8
NEG_BIG = -1e30
MAX_RAW_LOGIT = 40.0
SKIP_LOG_WEIGHT = 30.0
LOGIT_BOUND_MARGIN = 1.05
DIAG_LOGIT_MARGIN = 0.05
VMEM_LIMIT = 62 * 1024 * 1024

F32 = jnp.float32
BF16 = jnp.bfloat16


def _log_sigmoid(x):
    return jnp.minimum(x, 0.0) - jnp.log(1.0 + jnp.exp(-jnp.abs(x)))


def _sigmoid_of_twice(hx):
    return 0.5 * jnp.tanh(hx) + 0.5


def _silu_of_twice(hx):
    return hx * jnp.tanh(hx) + hx


def _dot(a, b):
    return jnp.dot(a, b, preferred_element_type=F32)


def _dot_nt(a, b):
    return lax.dot_general(a, b, (((1,), (1,)), ((), ())), preferred_element_type=F32)


def _dot_tn(a, b):
    return lax.dot_general(a, b, (((0,), (0,)), ((), ())), preferred_element_type=F32)


N_GB = 0
N_GC = N_GB + CONV_W
N_U = N_GC + CONV_W
N_ZB = N_U + CONV_W
N_GA = N_ZB + CONV_W
N_GB2 = N_GA + D_MODEL
N_END = N_GB2 + D_MODEL
T_Q = 0
T_K = T_Q + ATTN_W
T_V = T_K + ATTN_W
T_ZA = T_V + ATTN_W
T_F = T_ZA + ATTN_W
T_END = T_F + 2 * HEADS

IN_SIZES = (ATTN_W, ATTN_W, ATTN_W, HEADS, ATTN_W, CONV_W, CONV_W, CONV_W, CONV_W, D_MODEL, D_MODEL)
IN_OFFS = tuple(sum(IN_SIZES[:i]) for i in range(len(IN_SIZES) + 1))
PREP_COLS = 256


def _prep_kernel(w_ref, wa_ref, wb_ref, wo_ref, c_ref, wada_ref, bada_ref,
                 wn_ref, wt_ref, wa16_ref, wb16_ref, wo16_ref, ada_ref):
    ada_ref[:, 0, :] = (_dot(c_ref[...].astype(BF16), wada_ref[...].astype(BF16)) + bada_ref[...])

    rows = lambda i, j: w_ref[IN_OFFS[i]:IN_OFFS[j], :]
    f = rows(3, 4)
    wn_ref[...] = jnp.concatenate([rows(5, 8), 0.5 * rows(8, 11)], axis=0).astype(BF16).T
    wt_ref[T_Q:T_ZA, :] = rows(0, 3).astype(BF16)
    wt_ref[T_ZA:T_F, :] = (0.5 * rows(4, 5)).astype(BF16)
    wt_ref[T_F:T_END, :] = jnp.concatenate([f, f], axis=0).astype(BF16)
    wa16_ref[...] = wa_ref[...].astype(BF16)
    wb16_ref[...] = wb_ref[...].astype(BF16)
    wo16_ref[...] = wo_ref[...].astype(BF16)


def _prep(w_t, w_attn_out, w_conv_out, w_o, c, w_ada, b_ada):
    n, d = w_t.shape
    steps = d // PREP_COLS
    row_block = lambda w: pl.BlockSpec((w.shape[0] // steps, w.shape[1]), lambda i: (i, 0))
    sides = (w_attn_out, w_conv_out, w_o)
    b, n_ada = c.shape[0], w_ada.shape[1]
    ada_cols = n_ada // steps
    return pl.pallas_call(
        _prep_kernel,
        grid=(steps,),
        in_specs=[pl.BlockSpec((n, PREP_COLS), lambda i: (0, i))] + [row_block(w) for w in sides]
                 + [pl.BlockSpec((b, d), lambda i: (0, 0)),
                    pl.BlockSpec((d, ada_cols), lambda i: (0, i)),
                    pl.BlockSpec((1, ada_cols), lambda i: (0, i))],
        out_specs=[pl.BlockSpec((PREP_COLS, N_END), lambda i: (i, 0)),
                   pl.BlockSpec((T_END, PREP_COLS), lambda i: (0, i))] + [row_block(w) for w in sides]
                  + [pl.BlockSpec((b, 1, ada_cols), lambda i: (0, 0, i))],
        out_shape=[jax.ShapeDtypeStruct((d, N_END), BF16),
                   jax.ShapeDtypeStruct((T_END, d), BF16)]
                  + [jax.ShapeDtypeStruct(w.shape, BF16) for w in sides]
                  + [jax.ShapeDtypeStruct((b, 1, n_ada), F32)],
        compiler_params=pltpu.CompilerParams(vmem_limit_bytes=VMEM_LIMIT),
        name="prep",
    )(w_t, *sides, c, w_ada, b_ada.reshape(1, n_ada))


def _proj_kernel(xfirst_ref, xnext_ref, ada_ref, adanext_ref, ng_ref, wn_ref, wt_ref, bft_ref, gqk_ref,
                 cw_ref, wb_ref, ut_ref,
                 k_ref, kx_ref, qt_ref, cq_ref, vt_ref, za_ref, sga_ref, mb_ref, dmin_ref,
                 ct_ref, cu_ref, h_ref, gc_ref):
    s = pl.program_id(1)
    slot = lax.rem(s, 2)

    def modulated_norm(xr, ar):
        x = xr[0]
        shift = ar[0, :, 0:D_MODEL]
        scale = ar[0, :, D_MODEL:2 * D_MODEL]
        xn = x * lax.rsqrt(jnp.mean(x * x, axis=-1, keepdims=True) + EPS)
        return (xn * (ng_ref[...] * (1.0 + scale)) + shift).astype(BF16)

    @pl.when(s == 0)
    def _():
        ct_ref[...] = jnp.zeros_like(ct_ref)
        cu_ref[0:8, :] = jnp.zeros((8, CONV_W), F32)

    @pl.when((pl.program_id(0) == 0) & (s == 0))
    def _():
        h0 = modulated_norm(xfirst_ref, ada_ref)
        h_ref[0] = h0
        gc_ref[...] = _dot(h0, wn_ref[:, N_GC:N_U])

    nat = lambda lo, hi: _dot(h_ref[slot], wn_ref[:, lo:hi])
    trn = lambda lo, hi: _dot_nt(wt_ref[lo:hi, :], h_ref[slot])

    def head_norm(raw):
        r3 = raw.reshape(HEADS, HEAD_DIM, TM)
        rn = r3 * lax.rsqrt(jnp.mean(r3 * r3, axis=1, keepdims=True) + EPS)
        return rn.reshape(ATTN_W, TM)

    gc = gc_ref[...]
    u = nat(N_U, N_ZB)
    qraw = trn(T_Q, T_K)
    kraw = trn(T_K, T_V)

    cu = gc * u
    cu_ref[8:8 + TM, :] = cu
    conv = (cw_ref[2:3, :] * cu + cw_ref[1:2, :] * cu_ref[7:7 + TM, :]
            + cw_ref[0:1, :] * cu_ref[6:6 + TM, :])
    cu_ref[0:8, :] = cu[TM - 8:TM, :]
    gb = nat(N_GB, N_GC)
    zb = nat(N_ZB, N_GA)

    qf = head_norm(qraw) * gqk_ref[...]
    qt_ref[0] = qf.astype(BF16)
    zf = trn(T_ZA, T_END)

    kf = head_norm(kraw)
    kt = kf.astype(BF16)
    for pair in range(HEADS // 2):
        k_ref[0, pair] = kt[pair * LANES:(pair + 1) * LANES, :].T

    diag = (qf * kf).reshape(HEADS, HEAD_DIM, TM).sum(axis=1)
    lane8 = lax.broadcasted_iota(jnp.int32, (HEADS, LANES), 1)
    dmin = jnp.zeros((HEADS, LANES), F32)
    for t in range(TM // TQ):
        tile_min = jnp.min(diag[:, t * TQ:(t + 1) * TQ], axis=1, keepdims=True)
        dmin = jnp.where(lane8 == t, tile_min, dmin)
    dmin_ref[0, 0] = dmin
    ga = nat(N_GA, N_GB2)

    lft = _log_sigmoid(zf[ATTN_W:, :] + bft_ref[...])
    lft_lo = lft - lft.astype(BF16).astype(F32)
    row = lax.broadcasted_iota(jnp.int32, (2 * HEADS, TM), 0)
    ct = _dot(jnp.where(row < HEADS, lft, lft_lo).astype(BF16), ut_ref[...])
    gb2 = nat(N_GB2, N_END)

    za_ref[0] = _silu_of_twice(zf[0:ATTN_W, :]).astype(BF16)
    ob = (gb * conv * _silu_of_twice(zb)).astype(BF16)
    mbr = _dot(ob, wb_ref[...])
    vraw = trn(T_V, T_ZA)

    ct = ct[0:HEADS, :] + ct[HEADS:2 * HEADS, :]
    carry = ct_ref[...]
    cq = ct[:, 0:TM] + jnp.concatenate([carry] * (TM // LANES), axis=1)
    cq_ref[0] = cq
    ct_ref[...] = carry + ct[:, TM:TM + LANES]

    ck = jnp.concatenate([cq, jnp.zeros((LANES - HEADS, TM), F32)], axis=0).T
    ck = ck + pltpu.roll(ck, HEADS, 1) + pltpu.roll(ck, 2 * HEADS, 1)
    lane = lax.broadcasted_iota(jnp.int32, (TM, LANES), 1)
    r1 = ck - ck.astype(BF16).astype(F32)
    r2 = r1 - r1.astype(BF16).astype(F32)
    piece = jnp.where(lane < HEADS, ck, jnp.where(lane < 2 * HEADS, r1, r2))
    is_one = (lane >= ONES_LANE) & (lane < ONES_LANE + CUM_PARTS)
    kx_ref[0] = jnp.where(lane < ONES_LANE, -piece, jnp.where(is_one, 1.0, 0.0)).astype(BF16)

    sga_ref[0] = _sigmoid_of_twice(ga).astype(BF16)
    mb_ref[0] = (_sigmoid_of_twice(gb2) * mbr).astype(BF16)

    vt = vraw.astype(BF16)
    for i in range(TM // TK):
        for hd in range(HEADS):
            vt_ref[0, i, hd, 0:HEAD_DIM, :] = (
                vt[hd * HEAD_DIM:(hd + 1) * HEAD_DIM, i * TK:(i + 1) * TK])
            vt_ref[0, i, hd, HEAD_DIM:V_ROWS, :] = jnp.ones((V_ROWS - HEAD_DIM, TK), BF16)

    h_next = modulated_norm(xnext_ref, adanext_ref)
    h_ref[1 - slot] = h_next
    gc_ref[...] = _dot(h_next, wn_ref[:, N_GC:N_U])


def _proj(x, ada3, ng, wn, wt, bft, gqk, cw, wb, ut):
    b, s, d = x.shape
    const = lambda shape: pl.BlockSpec(shape, lambda bi, si: (0,) * len(shape))
    out_shape = [
        jax.ShapeDtypeStruct((b, HEADS // 2, s, LANES), BF16),
        jax.ShapeDtypeStruct((b, s, LANES), BF16),
        jax.ShapeDtypeStruct((b, ATTN_W, s), BF16),
        jax.ShapeDtypeStruct((b, HEADS, s), F32),
        jax.ShapeDtypeStruct((b, s // TK, HEADS, V_ROWS, TK), BF16),
        jax.ShapeDtypeStruct((b, ATTN_W, s), BF16),
        jax.ShapeDtypeStruct((b, s, d), BF16),
        jax.ShapeDtypeStruct((b, s, d), BF16),
        jax.ShapeDtypeStruct((b, s // TM, HEADS, LANES), F32),
    ]
    out_specs = [
        pl.BlockSpec((1, HEADS // 2, TM, LANES), lambda bi, si: (bi, 0, si, 0)),
        pl.BlockSpec((1, TM, LANES), lambda bi, si: (bi, si, 0)),
        pl.BlockSpec((1, ATTN_W, TM), lambda bi, si: (bi, 0, si)),
        pl.BlockSpec((1, HEADS, TM), lambda bi, si: (bi, 0, si)),
        pl.BlockSpec((1, TM // TK, HEADS, V_ROWS, TK), lambda bi, si: (bi, si, 0, 0, 0)),
        pl.BlockSpec((1, ATTN_W, TM), lambda bi, si: (bi, 0, si)),
        pl.BlockSpec((1, TM, d), lambda bi, si: (bi, si, 0)),
        pl.BlockSpec((1, TM, d), lambda bi, si: (bi, si, 0)),
        pl.BlockSpec((1, 1, HEADS, LANES), lambda bi, si: (bi, si, 0, 0)),
    ]
    tiles = s // TM
    assert tiles % 2 == 0
    next_b = lambda bi, si: jnp.minimum(bi + (si + 1) // tiles, b - 1)
    next_s = lambda bi, si: jnp.where((bi == b - 1) & (si == tiles - 1), si, (si + 1) % tiles)
    in_specs = [
        pl.BlockSpec((1, TM, d), lambda bi, si: (0, 0, 0)),
        pl.BlockSpec((1, TM, d), lambda bi, si: (next_b(bi, si), next_s(bi, si), 0)),
        pl.BlockSpec((1, 1, 3 * d), lambda bi, si: (bi, 0, 0)),
        pl.BlockSpec((1, 1, 3 * d), lambda bi, si: (next_b(bi, si), 0, 0)),
        const(ng.shape), const(wn.shape), const(wt.shape), const(bft.shape), const(gqk.shape),
        const(cw.shape), const(wb.shape), const(ut.shape),
    ]
    return pl.pallas_call(
        _proj_kernel,
        grid=(b, s // TM),
        in_specs=in_specs,
        out_specs=out_specs,
        out_shape=out_shape,
        scratch_shapes=[pltpu.VMEM((HEADS, LANES), F32),
                        pltpu.VMEM((TM + 8, CONV_W), F32),
                        pltpu.VMEM((2, TM, D_MODEL), BF16),
                        pltpu.VMEM((TM, CONV_W), F32)],
        compiler_params=pltpu.CompilerParams(
            dimension_semantics=("arbitrary", "arbitrary"),
            vmem_limit_bytes=VMEM_LIMIT),
        name="proj",
    )(x, x, ada3, ada3, ng, wn, wt, bft, gqk, cw, wb, ut)


def _attn_kernel(cend_ref, dmin_ref, bound_ref, qt_ref, cq_ref, k_ref, kx_ref, vt_ref, za_ref, sga_ref, mb_ref,
                 x_ref, gate_ref, wa_ref, wo_ref, o_ref, rhs_ref, m_ref, acc_ref, *, running_max):
    bi = pl.program_id(0)
    qi = pl.program_id(1)
    blocks_per_tile = TQ // TK
    n_blocks = kx_ref.shape[1] // TK
    i32 = jnp.int32

    last_before = qi * blocks_per_tile - 1
    n_tiles = n_blocks // blocks_per_tile
    slack = (LOGIT_BOUND_MARGIN + DIAG_LOGIT_MARGIN) * bound_ref[0] + SKIP_LOG_WEIGHT

    first = []
    for h in range(HEADS):
        base = (bi * HEADS + h) * n_blocks
        threshold = (cend_ref[base + jnp.maximum(last_before, 0)] + slack
                     - dmin_ref[(bi * HEADS + h) * n_tiles + qi])
        skipped = i32(0)
        for j in range(n_blocks - blocks_per_tile - 1):
            skipped = skipped + (cend_ref[base + j] >= threshold).astype(i32)
        first.append(skipped)
    upto, shift, total = [], [], i32(0)
    for h in range(HEADS):
        shift.append(first[h] - total)
        total = total + (qi * blocks_per_tile - first[h])
        upto.append(total)

    row = lax.broadcasted_iota(jnp.int32, (LANES, TQ), 0)
    for h in range(HEADS):
        pair, half = divmod(h, 2)
        qp = qt_ref[0, pair * LANES:(pair + 1) * LANES, :]
        mine = (row >= half * HEAD_DIM) & (row < (half + 1) * HEAD_DIM)
        rhs_ref[h, 0:LANES, :] = qp * jnp.where(mine, 1.0, 0.0).astype(BF16)
        cq = cq_ref[0, h:h + 1, :]
        r1 = cq - cq.astype(BF16).astype(F32)
        r2 = r1 - r1.astype(BF16).astype(F32)
        sel = (row == h) | (row == HEADS + h) | (row == 2 * HEADS + h)
        f = jnp.where(row == ONES_LANE, cq,
                      jnp.where(row == ONES_LANE + 1, r1,
                                jnp.where(row == ONES_LANE + 2, r2, 0.0)))
        rhs_ref[h, LANES:2 * LANES, :] = jnp.where(sel, 1.0, f).astype(BF16)
    rhs_ref[HEADS] = jnp.zeros(rhs_ref.shape[1:], BF16)

    if running_max:
        m_ref[...] = jnp.full(m_ref.shape, NEG_BIG, F32)
    acc_ref[...] = jnp.zeros_like(acc_ref)

    def scores(j, h, q_lo):
        start = pl.multiple_of(j * TK, TK)
        pair = h // 2 if isinstance(h, int) else jnp.minimum(h, HEADS - 1) // 2
        lhs = jnp.concatenate([k_ref[0, pair, pl.ds(start, TK), :],
                               kx_ref[0, pl.ds(start, TK), :]], axis=1)
        return _dot(lhs, rhs_ref[h, :, q_lo:TQ])

    def run_units(units):
        ahead = [scores(j, h, q_lo) for j, h, _, q_lo in units[:SCORE_LOOKAHEAD]]
        for i, (j, h, masked, q_lo) in enumerate(units):
            sc = ahead.pop(0)
            if i + SCORE_LOOKAHEAD < len(units):
                jn, hn, _, qn = units[i + SCORE_LOOKAHEAD]
                ahead.append(scores(jn, hn, qn))
            if masked:
                kpos = lax.broadcasted_iota(jnp.int32, sc.shape, 0)
                qpos = lax.broadcasted_iota(jnp.int32, sc.shape, 1)
                sc = jnp.where(kpos <= qpos, sc, NEG_BIG)
            hv = h if isinstance(h, int) else jnp.minimum(h, HEADS - 1)
            if running_max:
                m_prev = m_ref[h, :, q_lo:TQ]
                m_new = jnp.maximum(m_prev, jnp.max(sc, axis=0, keepdims=True))
                m_ref[h, :, q_lo:TQ] = m_new
                p = jnp.exp(sc - m_new).astype(BF16)
                acc_ref[h, :, q_lo:TQ] = (jnp.exp(m_prev - m_new) * acc_ref[h, :, q_lo:TQ]
                                          + _dot(vt_ref[0, j, hv], p))
            else:
                p = jnp.exp(sc).astype(BF16)
                acc_ref[h, :, q_lo:TQ] += _dot(vt_ref[0, j, hv], p)

    def list_units(start, count):
        units = []
        for t in range(count):
            u = start + t
            h = i32(0)
            off = shift[0]
            for g in range(HEADS):
                past = u >= upto[g]
                h = h + past.astype(i32)
                if g + 1 < HEADS:
                    off = jnp.where(past, shift[g + 1], off)
            j = jnp.where(u >= total, 0, u + off)
            units.append((j, h, False, 0))
        return units

    def long_body(i, carry):
        run_units(list_units(i * UNITS_PER_LONG_ITER, UNITS_PER_LONG_ITER))
        return carry

    n_short = (total + UNITS_PER_ITER - 1) // UNITS_PER_ITER
    n_long = n_short // (UNITS_PER_LONG_ITER // UNITS_PER_ITER)
    lax.fori_loop(0, n_long, long_body, 0)
    done = n_long * UNITS_PER_LONG_ITER

    def short_body(i, carry):
        run_units(list_units(done + i * UNITS_PER_ITER, UNITS_PER_ITER))
        return carry

    lax.fori_loop(0, n_short - n_long * (UNITS_PER_LONG_ITER // UNITS_PER_ITER), short_body, 0)
    run_units([(qi * blocks_per_tile + t, h, True, t * TK)
               for t in range(blocks_per_tile) for h in range(HEADS)])

    acc = jnp.concatenate(
        [acc_ref[h, 0:HEAD_DIM, :] * (1.0 / acc_ref[h, HEAD_DIM:HEAD_DIM + 1, :])
         for h in range(HEADS)], axis=0)
    at = (acc * za_ref[0].astype(F32)).astype(BF16)
    ya = _dot_tn(at, wa_ref[...])
    merged = (sga_ref[0].astype(F32) * ya + mb_ref[0].astype(F32)).astype(BF16)
    o_ref[0] = x_ref[0] + gate_ref[0] * _dot(merged, wo_ref[...])


def _attn(running_max, cend, dmin, bound, qt, cq, k, kx, vt, za, sga, mb, x, ada3, wa, wo):
    b, s, d = x.shape
    const = lambda shape: pl.BlockSpec(shape, lambda bi, qi: (0,) * len(shape))
    in_specs = [
        pl.BlockSpec(memory_space=pltpu.SMEM),
        pl.BlockSpec(memory_space=pltpu.SMEM),
        pl.BlockSpec(memory_space=pltpu.SMEM),
        pl.BlockSpec((1, ATTN_W, TQ), lambda bi, qi: (bi, 0, qi)),
        pl.BlockSpec((1, HEADS, TQ), lambda bi, qi: (bi, 0, qi)),
        pl.BlockSpec((1, HEADS // 2, s, LANES), lambda bi, qi: (bi, 0, 0, 0)),
        pl.BlockSpec((1, s, LANES), lambda bi, qi: (bi, 0, 0)),
        pl.BlockSpec((1, s // TK, HEADS, V_ROWS, TK), lambda bi, qi: (bi, 0, 0, 0, 0)),
        pl.BlockSpec((1, ATTN_W, TQ), lambda bi, qi: (bi, 0, qi)),
        pl.BlockSpec((1, TQ, d), lambda bi, qi: (bi, qi, 0)),
        pl.BlockSpec((1, TQ, d), lambda bi, qi: (bi, qi, 0)),
        pl.BlockSpec((1, TQ, d), lambda bi, qi: (bi, qi, 0)),
        pl.BlockSpec((1, 1, d), lambda bi, qi: (bi, 0, 2)),
        const(wa.shape), const(wo.shape),
    ]
    return pl.pallas_call(
        functools.partial(_attn_kernel, running_max=running_max),
        grid=(b, s // TQ),
        in_specs=in_specs,
        out_specs=pl.BlockSpec((1, TQ, d), lambda bi, qi: (bi, qi, 0)),
        out_shape=jax.ShapeDtypeStruct((b, s, d), F32),
        scratch_shapes=[pltpu.VMEM((HEADS + 1, 2 * LANES, TQ), BF16),
                        pltpu.VMEM((HEADS + 1, 1, TQ), F32),
                        pltpu.VMEM((HEADS + 1, V_ROWS, TQ), F32)],
        compiler_params=pltpu.CompilerParams(
            dimension_semantics=("arbitrary", "arbitrary"),
            vmem_limit_bytes=VMEM_LIMIT),
        name="attn_running_max" if running_max else "attn",
    )(cend, dmin, bound, qt, cq, k, kx, vt, za, sga, mb, x, ada3, wa, wo)


def _layer(x, c, w_ada, b_ada, norm_g, w_in, b_f, q_norm_g, k_norm_g, conv_w,
           w_attn_out, w_conv_out, w_o):
    b, s, d = x.shape
    wn, wt, wa, wb, wo, ada3 = _prep(w_in.T, w_attn_out, w_conv_out, w_o, c, w_ada, b_ada)

    bft = jnp.broadcast_to(jnp.concatenate([b_f, b_f])[:, None], (2 * HEADS, TM))
    gqk = jnp.tile(q_norm_g * k_norm_g * HEAD_DIM ** -0.5, HEADS).reshape(ATTN_W, 1)

    tok = np.arange(TM)
    upper = (tok[:, None] <= tok[None, :]).astype(np.float32)
    ut = jnp.asarray(np.concatenate([upper, np.ones((TM, LANES), np.float32)], axis=1), BF16)

    k, kx, qt, cq, vt, za, sga, mb, dmin4 = _proj(
        x, ada3, norm_g.reshape(1, d), wn, wt, bft, gqk, conv_w, wb, ut)
    logit_bound = HEAD_DIM ** 0.5 * jnp.max(jnp.abs(q_norm_g)) * jnp.max(jnp.abs(k_norm_g))
    cend = cq[:, :, TK - 1::TK].reshape(-1)
    dmin = jnp.transpose(dmin4[..., :TM // TQ], (0, 2, 1, 3)).reshape(-1)
    args = (cend, dmin, logit_bound.reshape(1), qt, cq, k, kx, vt, za, sga, mb, x, ada3, wa, wo)
    return lax.cond(logit_bound <= MAX_RAW_LOGIT,
                    functools.partial(_attn, False), functools.partial(_attn, True), *args)


@jax.jit
def kernel(x, c, w_ada, b_ada, norm_g, w_in, b_f, q_norm_g, k_norm_g, conv_w,
           w_attn_out, w_conv_out, w_o):
    for i in range(w_ada.shape[0]):
        x = _layer(x, c, w_ada[i], b_ada[i], norm_g[i], w_in[i], b_f[i], q_norm_g[i],
                   k_norm_g[i], conv_w[i], w_attn_out[i], w_conv_out[i], w_o[i])
    return x
```
